```python
import math
import jax
import jax.numpy as jnp
from jax import lax
import numpy as np

D_MODEL = 1024
BATCH = 1
SEQ = 16384
DEPTH = 2
DEC_BATCH = 32
DEC_SEQ = 4
PAST_LEN = 16384
PAGE_SIZE = 128

D_MIX = D_MODEL
SSD_HEADS = 8
SSD_HEAD_DIM = 64
D_SSD = SSD_HEADS * SSD_HEAD_DIM
SSD_GROUPS = 2
SSD_STATE = 128
CONV_W = 4
CONV_DIM = D_SSD + 2 * SSD_GROUPS * SSD_STATE
SSD_CHUNK = 128
NSA_HEADS = 8
HEAD_DIM = 64
KV_HEADS = 2
D_NSA = NSA_HEADS * HEAD_DIM
CMP_BLOCK = 32
CMP_STRIDE = 16
CMP_HIDDEN = 128
SLC_BLOCK = 64
TOP_N = 16
WINDOW = 512
Q_BLOCK = 128
ROPE_THETA = 10000.0
FORCE_BONUS = 1.0e4
IN_COLS = D_SSD + CONV_DIM + SSD_HEADS + D_NSA + 6 * KV_HEADS * HEAD_DIM + 3 * NSA_HEADS
D_FF = 2816
N_EXPERTS = 8
TOP_K = 2
D_EXPERT = 3584
MOE_BLOCK = 256
N_DENSE = (DEPTH + 1) // 2
N_MOE = DEPTH // 2
EPS = 1e-6

kernel_name = 'hymba_ssd_nsa_moe_step'


def rmsnorm(x, g):
    xf = x.astype(jnp.float32)
    y = xf * lax.rsqrt(jnp.mean(xf * xf, axis=-1, keepdims=True) + EPS)
    return (y * g.astype(jnp.float32)).astype(x.dtype)


def rope(x, pos):
    half = x.shape[-1] // 2
    inv = 1.0 / (ROPE_THETA ** (jnp.arange(0, x.shape[-1], 2, dtype=jnp.float32) / x.shape[-1]))
    ang = pos.astype(jnp.float32)[:, None] * inv[None, :]
    cos = jnp.cos(ang)[None, :, None, :]
    sin = jnp.sin(ang)[None, :, None, :]
    x1 = x[..., :half].astype(jnp.float32)
    x2 = x[..., half:].astype(jnp.float32)
    return jnp.concatenate([x1 * cos - x2 * sin, x2 * cos + x1 * sin], axis=-1).astype(x.dtype)


def masked_softmax(s, mask):
    s = jnp.where(mask, s.astype(jnp.float32), -1e30)
    m = jnp.max(s, axis=-1, keepdims=True)
    e = jnp.where(mask, jnp.exp(s - m), 0.0)
    return e / jnp.maximum(jnp.sum(e, axis=-1, keepdims=True), 1e-30)


def causal_conv(xbc, hist, w, b):
    L = xbc.shape[1]
    xp = jnp.concatenate([hist, xbc], axis=1)
    out = b
    for k in range(CONV_W):
        out = out + xp[:, k:k + L] * w[k]
    return jax.nn.silu(out), xp[:, L:]


def ssd_scan(x, dt, a, bm, cm, h0):
    bsz, L = x.shape[:2]
    q = math.gcd(L, SSD_CHUNK)
    nc = L // q
    R = SSD_HEADS // SSD_GROUPS
    f32 = jnp.float32
    xc = x.reshape(bsz, nc, q, SSD_GROUPS, R, SSD_HEAD_DIM).astype(f32)
    dtc = dt.reshape(bsz, nc, q, SSD_GROUPS, R)
    bc = bm.reshape(bsz, nc, q, SSD_GROUPS, SSD_STATE).astype(f32)
    cc = cm.reshape(bsz, nc, q, SSD_GROUPS, SSD_STATE).astype(f32)
    cum = jnp.cumsum(dtc * a.reshape(SSD_GROUPS, R), axis=2)
    causal = jnp.tril(jnp.ones((q, q), dtype=bool))[None, None, :, :, None, None]
    seg = cum[:, :, :, None] - cum[:, :, None, :]
    decay = jnp.exp(jnp.where(causal, seg, -jnp.inf))
    cb = jnp.einsum('bcign,bcjgn->bcijg', cc, bc)
    xdt = xc * dtc[..., None]
    y_intra = jnp.einsum('bcijgr,bcjgrp->bcigrp', decay * cb[..., None], xdt)
    to_end = jnp.exp(cum[:, :, -1:] - cum)
    s_chunk = jnp.einsum('bcjgn,bcjgrp->bcgrpn', bc, xdt * to_end[..., None])
    chunk_decay = jnp.exp(cum[:, :, -1])

    def step(h, inp):
        dec, s = inp
        return dec[..., None, None] * h + s, h

    h0g = h0.reshape(bsz, SSD_GROUPS, R, SSD_HEAD_DIM, SSD_STATE).astype(f32)
    h_fin, h_start = lax.scan(step, h0g, (jnp.moveaxis(chunk_decay, 1, 0), jnp.moveaxis(s_chunk, 1, 0)))
    h_start = jnp.moveaxis(h_start, 0, 1)
    y_inter = jnp.einsum('bcign,bcgrpn->bcigrp', cc, h_start) * jnp.exp(cum)[..., None]
    y = (y_intra + y_inter).reshape(bsz, L, SSD_HEADS, SSD_HEAD_DIM)
    return y, h_fin.reshape(bsz, SSD_HEADS, SSD_HEAD_DIM, SSD_STATE)


def ssd_mixer(z, xbc, dt_raw, hist, h0, conv_w, conv_b, dt_bias, a_log, d_skip, norm_g):
    bsz, L = z.shape[:2]
    xbc_c, new_hist = causal_conv(xbc, hist, conv_w, conv_b)
    xs = xbc_c[..., :D_SSD].reshape(bsz, L, SSD_HEADS, SSD_HEAD_DIM)
    bm = xbc_c[..., D_SSD:D_SSD + SSD_GROUPS * SSD_STATE].reshape(bsz, L, SSD_GROUPS, SSD_STATE)
    cm = xbc_c[..., D_SSD + SSD_GROUPS * SSD_STATE:].reshape(bsz, L, SSD_GROUPS, SSD_STATE)
    dt = jax.nn.softplus(dt_raw.astype(jnp.float32) + dt_bias.astype(jnp.float32))
    a = -jnp.exp(a_log.astype(jnp.float32))
    y, h_fin = ssd_scan(xs, dt, a, bm, cm, h0)
    y = y + d_skip.astype(jnp.float32)[:, None] * xs.astype(jnp.float32)
    y = y.reshape(bsz, L, D_SSD) * jax.nn.silu(z.astype(jnp.float32))
    return rmsnorm(y, norm_g).astype(z.dtype), new_hist, h_fin.astype(h0.dtype)


def compress(rows, pe, w1, b1, w2):
    bsz, t_pad = rows.shape[:2]
    r = CMP_BLOCK // CMP_STRIDE
    nh = t_pad // CMP_STRIDE
    nc = nh - r + 1
    halves = rows.reshape(bsz, nh, CMP_STRIDE, KV_HEADS, HEAD_DIM).transpose(0, 1, 3, 2, 4)
    pe_s = pe.reshape(r, CMP_STRIDE, HEAD_DIM)
    w1_s = w1.reshape(r, CMP_STRIDE * HEAD_DIM, CMP_HIDDEN)
    hidden = b1
    for s in range(r):
        part = (halves[:, s:s + nc] + pe_s[s]).reshape(bsz, nc, KV_HEADS, CMP_STRIDE * HEAD_DIM)
        hidden = hidden + part @ w1_s[s]
    return jax.nn.gelu(hidden) @ w2


def nsa_prepare(kv_full, l, cmp_pos, cmp_w1, cmp_b1, cmp_w2):
    bsz, t = kv_full.shape[:2]
    t_pad = -(-t // SLC_BLOCK) * SLC_BLOCK
    kvp = jnp.pad(kv_full, ((0, 0), (0, t_pad - t), (0, 0), (0, 0), (0, 0)))
    kc = compress(kvp[:, :, 0], cmp_pos[l, 0], cmp_w1[l, 0], cmp_b1[l, 0], cmp_w2[l, 0])
    vc = compress(kvp[:, :, 1], cmp_pos[l, 1], cmp_w1[l, 1], cmp_b1[l, 1], cmp_w2[l, 1])
    ks = kvp[:, :, 2].reshape(bsz, t_pad // SLC_BLOCK, SLC_BLOCK, KV_HEADS, HEAD_DIM)
    vs = kvp[:, :, 3].reshape(bsz, t_pad // SLC_BLOCK, SLC_BLOCK, KV_HEADS, HEAD_DIM)
    return kc, vc, ks, vs


def nsa_attend(q_nope, q_rot, gates, pos, kc, vc, ks, vs, kw, vw, kw_pos):
    bsz, lq = q_nope.shape[:2]
    R = NSA_HEADS // KV_HEADS
    f32 = jnp.float32
    scale = HEAD_DIM ** -0.5
    qn = q_nope.reshape(bsz, lq, KV_HEADS, R, HEAD_DIM) * scale
    qr = q_rot.reshape(bsz, lq, KV_HEADS, R, HEAD_DIM) * scale
    nc = kc.shape[1]
    c_start = jnp.arange(nc, dtype=jnp.int32) * CMP_STRIDE
    c_end = c_start + CMP_BLOCK - 1
    s_c = jnp.einsum('bqgrd,bcgd->bqgrc', qn, kc)
    mask_c = (c_end[None, :] <= pos[:, None])[None, :, None, None, :]
    p_c = masked_softmax(s_c, mask_c)
    o_c = jnp.einsum('bqgrc,bcgd->bqgrd', p_c, vc.astype(f32))
    ns = ks.shape[1]
    s_start = jnp.arange(ns, dtype=jnp.int32) * SLC_BLOCK
    overlap = ((c_start[:, None] <= s_start[None, :] + SLC_BLOCK - 1)
               & (c_end[:, None] >= s_start[None, :])).astype(f32)
    imp = jnp.einsum('bqgrc,cs->bqgs', p_c, overlap)
    cur = (pos // SLC_BLOCK)[:, None]
    j = jnp.arange(ns, dtype=jnp.int32)[None, :]
    valid = j <= cur
    forced = valid & ((j == 0) | (j == cur) | (j == cur - 1))
    score = jnp.where(forced[None, :, None, :], FORCE_BONUS,
                      jnp.where(valid[None, :, None, :], imp, -1.0))
    n_sel = min(TOP_N, ns)
    _, idx = lax.top_k(score, n_sel)
    idx_t = idx.transpose(0, 2, 1, 3)
    idx_f = idx_t.reshape(bsz, KV_HEADS, lq * n_sel)
    bi = jnp.arange(bsz)[:, None, None]
    gi = jnp.arange(KV_HEADS)[None, :, None]
    k_sel = ks[bi, idx_f, :, gi].reshape(bsz, KV_HEADS, lq, n_sel * SLC_BLOCK, HEAD_DIM)
    v_sel = vs[bi, idx_f, :, gi].reshape(bsz, KV_HEADS, lq, n_sel * SLC_BLOCK, HEAD_DIM)
    tok = (idx_t[..., None] * SLC_BLOCK + jnp.arange(SLC_BLOCK, dtype=jnp.int32)).reshape(
        bsz, KV_HEADS, lq, n_sel * SLC_BLOCK)
    mask_s = (tok <= pos[None, None, :, None]).transpose(0, 2, 1, 3)[:, :, :, None, :]
    s_s = jnp.einsum('bqgrd,bgqkd->bqgrk', qr, k_sel)
    p_s = masked_softmax(s_s, mask_s)
    o_s = jnp.einsum('bqgrk,bgqkd->bqgrd', p_s, v_sel.astype(f32))
    dpos = pos[:, None] - kw_pos[None, :]
    mask_w = ((dpos >= 0) & (dpos < WINDOW) & (kw_pos >= 0)[None, :])[None, :, None, None, :]
    s_w = jnp.einsum('bqgrd,bkgd->bqgrk', qr, kw)
    p_w = masked_softmax(s_w, mask_w)
    o_w = jnp.einsum('bqgrk,bkgd->bqgrd', p_w, vw.astype(f32))
    g = gates.reshape(bsz, lq, KV_HEADS, R, 3).astype(f32)
    out = g[..., 0:1] * o_c + g[..., 1:2] * o_s + g[..., 2:3] * o_w
    return out.reshape(bsz, lq, D_NSA).astype(q_nope.dtype)


def nsa_prompt(q_nope, q_rot, gates, kc, vc, ks, vs, win_rows):
    bsz, L = q_nope.shape[:2]
    qb = math.gcd(L, Q_BLOCK)
    kw_pad = jnp.pad(win_rows, ((0, 0), (WINDOW, 0), (0, 0), (0, 0), (0, 0)))

    def block(i):
        st = i * qb
        qn = lax.dynamic_slice_in_dim(q_nope, st, qb, axis=1)
        qr = lax.dynamic_slice_in_dim(q_rot, st, qb, axis=1)
        gt = lax.dynamic_slice_in_dim(gates, st, qb, axis=1)
        kwb = lax.dynamic_slice_in_dim(kw_pad, st, WINDOW + qb, axis=1)
        pos = st + jnp.arange(qb, dtype=jnp.int32)
        kpos = st - WINDOW + jnp.arange(WINDOW + qb, dtype=jnp.int32)
        return nsa_attend(qn, qr, gt, pos, kc, vc, ks, vs, kwb[:, :, 0], kwb[:, :, 1], kpos)

    out = lax.map(block, jnp.arange(L // qb, dtype=jnp.int32))
    return jnp.moveaxis(out, 0, 1).reshape(bsz, L, D_NSA)


def mixer_front(x, pos, conv_hist, h0, l, ln_mix, w_in, conv_w, conv_b, dt_bias, a_log, d_skip, ssd_norm):
    bsz, L = x.shape[:2]
    p = rmsnorm(x, ln_mix[l]) @ w_in[l]
    o1 = D_SSD
    o2 = o1 + CONV_DIM
    o3 = o2 + SSD_HEADS
    o4 = o3 + D_NSA
    o5 = o4 + 6 * KV_HEADS * HEAD_DIM
    y_ssd, hist, h_fin = ssd_mixer(p[..., :o1], p[..., o1:o2], p[..., o2:o3], conv_hist, h0,
                                   conv_w[l], conv_b[l], dt_bias[l], a_log[l], d_skip[l], ssd_norm[l])
    q = p[..., o3:o4].reshape(bsz, L, NSA_HEADS, HEAD_DIM)
    kv6 = p[..., o4:o5].reshape(bsz, L, 6, KV_HEADS, HEAD_DIM)
    gates = jax.nn.sigmoid(p[..., o5:]).reshape(bsz, L, NSA_HEADS, 3)
    q_rot = rope(q, pos)
    kv_rows = jnp.stack([kv6[:, :, 0], kv6[:, :, 1], rope(kv6[:, :, 2], pos), kv6[:, :, 3]], axis=2)
    win_rows = jnp.stack([rope(kv6[:, :, 4], pos), kv6[:, :, 5]], axis=2)
    return y_ssd, hist, h_fin, q, q_rot, gates, kv_rows, win_rows


def swiglu(h, wg, wu, wd):
    return (jax.nn.silu(h @ wg) * (h @ wu)) @ wd


def moe_ffn(h, router, w_gate, w_up, w_down):
    bsz, L, d = h.shape
    x2 = h.reshape(-1, d)
    t = x2.shape[0]
    logits = (x2 @ router).astype(jnp.float32)
    top_v, top_e = lax.top_k(logits, TOP_K)
    gate = jax.nn.softmax(top_v, axis=-1)
    e_flat = top_e.reshape(-1)
    tok = jnp.repeat(jnp.arange(t, dtype=jnp.int32), TOP_K)
    g_flat = gate.reshape(-1)
    order = jnp.argsort(e_flat)
    e_s = e_flat[order]
    tok_s = tok[order]
    g_s = g_flat[order]
    counts = jnp.bincount(e_flat, length=N_EXPERTS)
    starts = jnp.cumsum(counts) - counts
    padded = (counts + MOE_BLOCK - 1) // MOE_BLOCK * MOE_BLOCK
    pends = jnp.cumsum(padded)
    pstarts = pends - padded
    dest = pstarts[e_s] + jnp.arange(t * TOP_K, dtype=jnp.int32) - starts[e_s]
    n_blk = -(-(t * TOP_K) // MOE_BLOCK) + N_EXPERTS
    buf_tok = jnp.full((n_blk * MOE_BLOCK,), t, dtype=jnp.int32).at[dest].set(tok_s)
    blk_e = jnp.minimum(jnp.searchsorted(pends, jnp.arange(n_blk, dtype=jnp.int32) * MOE_BLOCK,
                                         side='right'), N_EXPERTS - 1)
    x_pad = jnp.concatenate([x2, jnp.zeros((1, d), x2.dtype)], axis=0)

    def run(args):
        toks, e = args
        xb = x_pad[toks]
        return swiglu(xb, w_gate[e], w_up[e], w_down[e])

    out = lax.map(run, (buf_tok.reshape(n_blk, MOE_BLOCK), blk_e)).reshape(-1, d)
    contrib = out[dest] * g_s[:, None].astype(out.dtype)
    y = jax.ops.segment_sum(contrib, tok_s, num_segments=t)
    return y.reshape(bsz, L, d)


def mixer_back(x, y_ssd, y_nsa, l, w_out, ln_ffn, ffn_w_gate, ffn_w_up, ffn_w_down,
               moe_router, moe_w_gate, moe_w_up, moe_w_down):
    x = x + jnp.concatenate([y_ssd, y_nsa], axis=-1) @ w_out[l]
    h = rmsnorm(x, ln_ffn[l])
    if l % 2 == 0:
        i = l // 2
        out = swiglu(h, ffn_w_gate[i], ffn_w_up[i], ffn_w_down[i])
    else:
        i = l // 2
        out = moe_ffn(h, moe_router[i], moe_w_gate[i], moe_w_up[i], moe_w_down[i])
    return x + out


def setup_inputs(seed: int = 0) -> dict:
    key = jax.random.key(seed)
    k = jax.random.split(key, 32)
    f32 = jnp.float32

    def nrm(kk, shape, scale):
        return jax.random.normal(kk, shape, f32) * scale

    n_pages = PAST_LEN // PAGE_SIZE
    n_used = DEC_BATCH * n_pages
    n_pool = n_used + max(1, n_used // 4)
    win_buf = min(WINDOW, PAST_LEN)
    page_table = jax.random.permutation(k[0], n_pool)[:n_used].reshape(DEC_BATCH, n_pages).astype(jnp.int32)
    dt0 = jnp.exp(jax.random.uniform(k[1], (DEPTH, SSD_HEADS), f32, math.log(1e-3), math.log(1e-1)))
    dt_bias = dt0 + jnp.log(-jnp.expm1(-dt0))
    a_log = jnp.log(jax.random.uniform(k[2], (DEPTH, SSD_HEADS), f32, 1.0, 16.0))
    return {
        'x_prompt': nrm(k[3], (BATCH, SEQ, D_MODEL), 1.0),
        'x_sample': nrm(k[4], (DEC_BATCH, DEC_SEQ, D_MODEL), 1.0),
        'cache_kv': nrm(k[5], (DEPTH, n_pool, PAGE_SIZE, 4, KV_HEADS, HEAD_DIM), 1.0),
        'cache_win': nrm(k[6], (DEPTH, DEC_BATCH, win_buf, 2, KV_HEADS, HEAD_DIM), 1.0),
        'state_ssm': nrm(k[7], (DEPTH, DEC_BATCH, SSD_HEADS, SSD_HEAD_DIM, SSD_STATE), 0.1),
        'state_conv': nrm(k[8], (DEPTH, DEC_BATCH, CONV_W - 1, CONV_DIM), 1.0),
        'page_table': page_table,
        'ln_mix': 1.0 + nrm(k[9], (DEPTH, D_MODEL), 0.02),
        'w_in': nrm(k[10], (DEPTH, D_MODEL, IN_COLS), D_MODEL ** -0.5),
        'conv_w': nrm(k[11], (DEPTH, CONV_W, CONV_DIM), CONV_W ** -0.5),
        'conv_b': nrm(k[12], (DEPTH, CONV_DIM), 0.02),
        'dt_bias': dt_bias,
        'a_log': a_log,
        'd_skip': 1.0 + nrm(k[13], (DEPTH, SSD_HEADS), 0.02),
        'ssd_norm': 1.0 + nrm(k[14], (DEPTH, D_SSD), 0.02),
        'cmp_pos': nrm(k[15], (DEPTH, 2, CMP_BLOCK, HEAD_DIM), 0.1),
        'cmp_w1': nrm(k[16], (DEPTH, 2, CMP_BLOCK * HEAD_DIM, CMP_HIDDEN), (CMP_BLOCK * HEAD_DIM) ** -0.5),
        'cmp_b1': nrm(k[17], (DEPTH, 2, CMP_HIDDEN), 0.02),
        'cmp_w2': nrm(k[18], (DEPTH, 2, CMP_HIDDEN, HEAD_DIM), CMP_HIDDEN ** -0.5),
        'w_out': nrm(k[19], (DEPTH, D_MIX, D_MODEL), D_MIX ** -0.5),
        'ln_ffn': 1.0 + nrm(k[20], (DEPTH, D_MODEL), 0.02),
        'ffn_w_gate': nrm(k[21], (N_DENSE, D_MODEL, D_FF), D_MODEL ** -0.5),
        'ffn_w_up': nrm(k[22], (N_DENSE, D_MODEL, D_FF), D_MODEL ** -0.5),
        'ffn_w_down': nrm(k[23], (N_DENSE, D_FF, D_MODEL), D_FF ** -0.5),
        'moe_router': nrm(k[24], (N_MOE, D_MODEL, N_EXPERTS), D_MODEL ** -0.5),
        'moe_w_gate': nrm(k[25], (N_MOE, N_EXPERTS, D_MODEL, D_EXPERT), D_MODEL ** -0.5),
        'moe_w_up': nrm(k[26], (N_MOE, N_EXPERTS, D_MODEL, D_EXPERT), D_MODEL ** -0.5),
        'moe_w_down': nrm(k[27], (N_MOE, N_EXPERTS, D_EXPERT, D_MODEL), D_EXPERT ** -0.5),
        'ln_final': 1.0 + nrm(k[28], (D_MODEL,), 0.02),
    }


def reference(x_prompt, x_sample, cache_kv, cache_win, state_ssm, state_conv, page_table,
              ln_mix, w_in, conv_w, conv_b, dt_bias, a_log, d_skip, ssd_norm,
              cmp_pos, cmp_w1, cmp_b1, cmp_w2, w_out, ln_ffn,
              ffn_w_gate, ffn_w_up, ffn_w_down, moe_router, moe_w_gate, moe_w_up, moe_w_down,
              ln_final):
    bsz, seq = x_prompt.shape[:2]
    dbsz, dseq = x_sample.shape[:2]
    n_pages = page_table.shape[1]
    past = n_pages * cache_kv.shape[2]
    win_buf = cache_win.shape[2]
    keep_p = min(WINDOW, seq)
    pos_p = jnp.arange(seq, dtype=jnp.int32)
    pos_s = past + jnp.arange(dseq, dtype=jnp.int32)
    kpos_s = past - win_buf + jnp.arange(win_buf + dseq, dtype=jnp.int32)
    xp = x_prompt
    xs = x_sample
    kv_p, kv_s, win_p, win_s, ssm_p, ssm_s, conv_p, conv_s = [], [], [], [], [], [], [], []
    for l in range(DEPTH):
        hist0 = jnp.zeros((bsz, CONV_W - 1, CONV_DIM), xp.dtype)
        h00 = jnp.zeros((bsz, SSD_HEADS, SSD_HEAD_DIM, SSD_STATE), xp.dtype)
        y_ssd, hist, h_fin, q, q_rot, gates, kv_rows, win_rows = mixer_front(
            xp, pos_p, hist0, h00, l, ln_mix, w_in, conv_w, conv_b, dt_bias, a_log, d_skip, ssd_norm)
        kc, vc, ks, vs = nsa_prepare(kv_rows, l, cmp_pos, cmp_w1, cmp_b1, cmp_w2)
        y_nsa = nsa_prompt(q, q_rot, gates, kc, vc, ks, vs, win_rows)
        xp = mixer_back(xp, y_ssd, y_nsa, l, w_out, ln_ffn, ffn_w_gate, ffn_w_up, ffn_w_down,
                        moe_router, moe_w_gate, moe_w_up, moe_w_down)
        kv_p.append(kv_rows)
        win_p.append(win_rows[:, seq - keep_p:])
        ssm_p.append(h_fin)
        conv_p.append(hist)
        y_ssd_s, hist_s, h_fin_s, q_s, q_rot_s, gates_s, kv_rows_s, win_rows_s = mixer_front(
            xs, pos_s, state_conv[l], state_ssm[l], l, ln_mix, w_in, conv_w, conv_b, dt_bias, a_log,
            d_skip, ssd_norm)
        past_rows = cache_kv[l][page_table].reshape(dbsz, past, 4, KV_HEADS, HEAD_DIM)
        kv_full = jnp.concatenate([past_rows, kv_rows_s], axis=1)
        kc_s, vc_s, ks_s, vs_s = nsa_prepare(kv_full, l, cmp_pos, cmp_w1, cmp_b1, cmp_w2)
        win_all = jnp.concatenate([cache_win[l], win_rows_s], axis=1)
        y_nsa_s = nsa_attend(q_s, q_rot_s, gates_s, pos_s, kc_s, vc_s, ks_s, vs_s,
                             win_all[:, :, 0], win_all[:, :, 1], kpos_s)
        xs = mixer_back(xs, y_ssd_s, y_nsa_s, l, w_out, ln_ffn, ffn_w_gate, ffn_w_up, ffn_w_down,
                        moe_router, moe_w_gate, moe_w_up, moe_w_down)
        kv_s.append(kv_rows_s)
        win_s.append(win_all[:, dseq:])
        ssm_s.append(h_fin_s)
        conv_s.append(hist_s)
    y_prompt = rmsnorm(xp, ln_final)
    y_sample = rmsnorm(xs, ln_final)
    kv_prompt = jnp.stack(kv_p)
    kv_sample = jnp.stack(kv_s)
    win_prompt = jnp.stack(win_p)
    win_sample = jnp.stack(win_s)
    ssm_prompt = jnp.stack(ssm_p)
    ssm_sample = jnp.stack(ssm_s)
    conv_prompt = jnp.stack(conv_p)
    conv_sample = jnp.stack(conv_s)
    return (y_prompt, y_sample, kv_prompt, kv_sample, win_prompt, win_sample,
            ssm_prompt, ssm_sample, conv_prompt, conv_sample)
```

```python
import functools
import math

import jax
import jax.numpy as jnp
from jax import lax
from jax.experimental import pallas as pl
from jax.experimental.pallas import tpu as pltpu

F32 = jnp.float32
BF16 = jnp.bfloat16

D_MODEL = 1024
SSD_HEADS = 8
SSD_HEAD_DIM = 64
D_SSD = 512
SSD_GROUPS = 2
SSD_STATE = 128
CONV_W = 4
CONV_DIM = 1024
SSD_CHUNK = 128
NSA_HEADS = 8
HEAD_DIM = 64
KV_HEADS = 2
D_NSA = 512
CMP_BLOCK = 32
CMP_STRIDE = 16
CMP_HIDDEN = 128
SLC_BLOCK = 64
TOP_N = 16
WINDOW = 512
ROPE_THETA = 10000.0
FORCE_BONUS = 1.0e4
N_EXPERTS = 8
TOP_K = 2
EPS = 1e-6

PAGE = 128
LANES = 128
NEG = -1.0e30
SLAB_BLOCKS = 128
VMEM_LIMIT = 56 * 1024 * 1024

C_Z, C_XBC, C_Q, C_KV, C_DT, C_END = 0, 512, 1536, 2048, 2816, 2944


def _cparams(sem):
    return pltpu.CompilerParams(dimension_semantics=sem, vmem_limit_bytes=VMEM_LIMIT)


def _dot(a, b):
    return jnp.dot(a, b, preferred_element_type=F32)


def _dot_nt(a, b):
    return lax.dot_general(a, b, (((1,), (1,)), ((), ())), preferred_element_type=F32)


def _split3(x):
    h1 = x.astype(BF16)
    r1 = x - h1.astype(F32)
    h2 = r1.astype(BF16)
    h3 = (r1 - h2.astype(F32)).astype(BF16)
    return h1, h2, h3


def _dot_nt_hi(a, b):
    a1, a2, _ = _split3(a)
    b1, b2, _ = _split3(b)
    return _dot_nt(a1, b1) + (_dot_nt(a1, b2) + _dot_nt(a2, b1))


def _dot_hi(a, b):
    a1, a2, _ = _split3(a)
    b1, b2, _ = _split3(b)
    return _dot(a1, b1) + (_dot(a1, b2) + _dot(a2, b1))


def _silu(x):
    return x * jax.nn.sigmoid(x)


def _rms(x, g):
    return x * lax.rsqrt(jnp.mean(x * x, axis=-1, keepdims=True) + EPS) * g


def _front_kernel(x_ref, g_ref, w_ref, cs_ref, sn_ref,
                  z_ref, xbc_ref, dtg_ref, q_ref, qr_ref, kv_ref, win_ref):
    xn = _rms(x_ref[...], g_ref[...])
    p = _dot(xn.astype(BF16), w_ref[...])
    cs = cs_ref[...]
    sn = sn_ref[...]
    lane = lax.broadcasted_iota(jnp.int32, cs.shape, 1)
    first = (lane % HEAD_DIM) < (HEAD_DIM // 2)

    def rope(v):
        sw = jnp.where(first, pltpu.roll(v, LANES - HEAD_DIM // 2, 1), pltpu.roll(v, HEAD_DIM // 2, 1))
        return v * cs + sw * sn

    z_ref[...] = p[:, C_Z:C_XBC]
    xbc_ref[...] = p[:, C_XBC:C_Q]
    dtg_ref[...] = p[:, C_DT:C_END]
    q_ref[...] = p[:, C_Q:C_KV]
    for s in range(4):
        qr_ref[:, s * LANES:(s + 1) * LANES] = rope(p[:, C_Q + s * LANES:C_Q + (s + 1) * LANES])
    kv_ref[:, 0:256] = p[:, C_KV:C_KV + 256]
    kv_ref[:, 256:384] = rope(p[:, C_KV + 256:C_KV + 384])
    kv_ref[:, 384:512] = p[:, C_KV + 384:C_KV + 512]
    win_ref[:, 0:128] = rope(p[:, C_KV + 512:C_KV + 640])
    win_ref[:, 128:256] = p[:, C_KV + 640:C_KV + 768]


def _front(x, g, w, cs, sn, tm):
    t = x.shape[0]
    row = lambda n: pl.BlockSpec((tm, n), lambda i: (i, 0))
    full = lambda a: pl.BlockSpec(a.shape, lambda i: (0,) * a.ndim)
    outs = [512, 1024, 128, 512, 512, 512, 256]
    return pl.pallas_call(
        _front_kernel,
        grid=(t // tm,),
        in_specs=[row(D_MODEL), full(g), full(w), row(LANES), row(LANES)],
        out_specs=[row(n) for n in outs],
        out_shape=[jax.ShapeDtypeStruct((t, n), F32) for n in outs],
        compiler_params=_cparams(("arbitrary",)),
    )(x, g, w, cs, sn)


def _expand_heads(v):
    r = v.shape[0]
    lane = lax.broadcasted_iota(jnp.int32, (r, LANES), 1)
    outs = []
    for k in range(4):
        a = jnp.broadcast_to(v[:, 2 * k:2 * k + 1], (r, LANES))
        b = jnp.broadcast_to(v[:, 2 * k + 1:2 * k + 2], (r, LANES))
        outs.append(jnp.where(lane < SSD_HEAD_DIM, a, b))
    return jnp.concatenate(outs, axis=1)


def _ssd_kernel(z_ref, xbc_ref, dtg_ref, hist_ref, h0_ref, cw_ref, cb_ref, dtb_ref, a_ref, dsk_ref, ng_ref,
                y_ref, hist_o_ref, hfin_ref, xpad, ht, *, nchunks, valid_last):
    c = pl.program_id(1)
    q = SSD_CHUNK

    @pl.when(c == 0)
    def _():
        xpad[0:8, :] = hist_ref[0]
        ht[...] = h0_ref[0].T

    xpad[8:8 + q, :] = xbc_ref[0]
    conv = cb_ref[...]
    for k in range(CONV_W):
        conv = conv + xpad[5 + k:5 + k + q, :] * cw_ref[k:k + 1, :]
    xc = _silu(conv)
    xs = xc[:, 0:D_SSD]
    bm = xc[:, D_SSD:D_SSD + 256]
    cm = xc[:, D_SSD + 256:D_SSD + 512]

    row = lax.broadcasted_iota(jnp.int32, (q, q), 0)
    col = lax.broadcasted_iota(jnp.int32, (q, q), 1)
    nvalid = jnp.where(c == nchunks - 1, valid_last, q)
    xdt_in = dtg_ref[0] + dtb_ref[...]
    dt = jnp.maximum(xdt_in, 0.0) + jnp.log1p(jnp.exp(-jnp.abs(xdt_in)))
    dt = jnp.where(row < nvalid, dt, 0.0)
    dta = dt * a_ref[...]
    causal = row >= col
    cum = jnp.dot(causal.astype(F32), dta, preferred_element_type=F32, precision=lax.Precision.HIGHEST)
    cum_t = cum.T
    ecum = jnp.exp(cum)
    toend = jnp.exp(cum[q - 1:q, :] - cum)
    dt_e = _expand_heads(dt)
    ecum_e = _expand_heads(ecum)
    toend_e = _expand_heads(toend)
    xdt = xs * dt_e
    xdt_b = xdt.astype(BF16)
    xw_b = (xdt * toend_e).astype(BF16)
    lane = lax.broadcasted_iota(jnp.int32, (q, LANES), 1)

    y_intra = []
    y_inter = []
    for g in range(SSD_GROUPS):
        bg = bm[:, g * 128:(g + 1) * 128]
        cg_b = cm[:, g * 128:(g + 1) * 128].astype(BF16)
        bg_b = bg.astype(BF16)
        cb = _dot_nt(cg_b, bg_b)
        htg = ht[:, g * 256:(g + 1) * 256]
        y_inter.append(_dot(cg_b, htg.astype(BF16)))
        for kk in range(2):
            slab = xdt_b[:, (2 * g + kk) * LANES:(2 * g + kk + 1) * LANES]
            res = []
            for hh in range(2):
                h = 4 * g + 2 * kk + hh
                seg = cum[:, h:h + 1] - cum_t[h:h + 1, :]
                decay = jnp.exp(jnp.where(causal, seg, NEG))
                res.append(_dot((decay * cb).astype(BF16), slab))
            y_intra.append(jnp.where(lane < SSD_HEAD_DIM, res[0], res[1]))
        s_new = _dot(bg.T.astype(BF16), xw_b[:, g * 256:(g + 1) * 256])
        ht[:, g * 256:(g + 1) * 256] = htg * ecum_e[q - 1:q, g * 256:(g + 1) * 256] + s_new

    y = jnp.concatenate(y_intra, axis=1) + jnp.concatenate(y_inter, axis=1) * ecum_e + dsk_ref[...] * xs
    y = y * _silu(z_ref[0])
    y_ref[0] = _rms(y, ng_ref[...])

    xpad[0:8, :] = xpad[q:q + 8, :]

    @pl.when(c == nchunks - 1)
    def _():
        hist_o_ref[0] = xpad[valid_last:valid_last + 8, :]
        hfin_ref[0] = ht[...].T


def _ssd(z, xbc, dtg, hist8, h0, cw8, cb, dtb, a, dsk, ng, valid_last):
    b, l = z.shape[:2]
    nchunks = l // SSD_CHUNK
    rows = lambda n: pl.BlockSpec((1, SSD_CHUNK, n), lambda i, c: (i, c, 0))
    perb = lambda r, n: pl.BlockSpec((1, r, n), lambda i, c: (i, 0, 0))
    full = lambda arr: pl.BlockSpec(arr.shape, lambda i, c: (0,) * arr.ndim)
    kern = functools.partial(_ssd_kernel, nchunks=nchunks, valid_last=valid_last)
    return pl.pallas_call(
        kern,
        grid=(b, nchunks),
        in_specs=[rows(512), rows(1024), rows(128), perb(8, 1024), perb(512, 128),
                  full(cw8), full(cb), full(dtb), full(a), full(dsk), full(ng)],
        out_specs=[rows(512), perb(8, 1024), perb(512, 128)],
        out_shape=[jax.ShapeDtypeStruct((b, l, 512), F32),
                   jax.ShapeDtypeStruct((b, 8, 1024), F32),
                   jax.ShapeDtypeStruct((b, 512, 128), F32)],
        scratch_shapes=[pltpu.VMEM((SSD_CHUNK + 8, CONV_DIM), F32), pltpu.VMEM((SSD_STATE, D_SSD), F32)],
        compiler_params=_cparams(("arbitrary", "arbitrary")),
    )(z, xbc, dtg, hist8, h0, cw8, cb, dtb, a, dsk, ng)


CMP_GROUP_PAGES = 16


def _compress_kernel(pt_ref, pgk_ref, pgv_ref, w1_ref, pe_ref, b1_ref, w2_ref, out_ref, xs, ps, *, npages):
    p = pl.program_id(1)
    sub = p % CMP_GROUP_PAGES
    hb = PAGE // CMP_STRIDE
    r0 = pl.multiple_of(sub * hb, hb)
    for kv, pg_ref in enumerate((pgk_ref, pgv_ref)):
        for pos in range(CMP_STRIDE):
            xs[kv, pl.ds(r0, hb), pos * LANES:(pos + 1) * LANES] = pg_ref[0, pl.ds(pos, hb, stride=CMP_STRIDE), :]

    @pl.when(sub == CMP_GROUP_PAGES - 1)
    def _():
        rows = CMP_GROUP_PAGES * hb
        g0 = pl.multiple_of((p // CMP_GROUP_PAGES) * rows, rows)
        for kv in range(2):
            x = xs[kv]
            for s in range(2):
                ps[kv, s, pl.ds(g0, rows), :] = _dot((x + pe_ref[kv, s:s + 1, :]).astype(BF16), w1_ref[kv, s])

    @pl.when(p == npages - 1)
    def _():
        nh = npages * hb
        for kv in range(2):
            hid = b1_ref[kv:kv + 1, :] + ps[kv, 0] + pltpu.roll(ps[kv, 1], nh - 1, 0)
            out_ref[0, :, kv * LANES:(kv + 1) * LANES] = _dot(jax.nn.gelu(hid).astype(BF16), w2_ref[kv])


def _compress(pages, table, w1e, pee, b1e, w2e):
    b, npages = table.shape
    nh = npages * (PAGE // CMP_STRIDE)
    kern = functools.partial(_compress_kernel, npages=npages)
    full = lambda arr: pl.BlockSpec(arr.shape, lambda i, p, pt: (0,) * arr.ndim)
    return pl.pallas_call(
        kern,
        grid_spec=pltpu.PrefetchScalarGridSpec(
            num_scalar_prefetch=1,
            grid=(b, npages),
            in_specs=[pl.BlockSpec((1, PAGE, LANES), lambda i, p, pt: (pt[i, p], 0, 0)),
                      pl.BlockSpec((1, PAGE, LANES), lambda i, p, pt: (pt[i, p], 0, 1)),
                      full(w1e), full(pee), full(b1e), full(w2e)],
            out_specs=pl.BlockSpec((1, nh, 256), lambda i, p, pt: (i, 0, 0)),
            scratch_shapes=[pltpu.VMEM((2, CMP_GROUP_PAGES * 8, CMP_STRIDE * LANES), F32),
                            pltpu.VMEM((2, 2, nh, 256), F32)]),
        out_shape=jax.ShapeDtypeStruct((b, nh, 256), F32),
        compiler_params=_cparams(("arbitrary", "arbitrary")),
    )(table, pages, pages, w1e, pee, b1e, w2e)


def _kvfmt_kernel(pt_ref, pg_ref, tail_ref, k_ref, v_ref, *, npages):
    p = pl.program_id(1)
    src = jnp.where(p < npages, pg_ref[0], jnp.where(p == npages, tail_ref[0], 0.0))
    r = lax.broadcasted_iota(jnp.int32, (PAGE, LANES), 0)
    lane = lax.broadcasted_iota(jnp.int32, (PAGE, LANES), 1)
    blk = (2 * p + r // SLC_BLOCK) % SLAB_BLOCKS
    k_ref[0, :, 0:128] = src[:, 0:128].astype(BF16)
    k_ref[0, :, 128:256] = jnp.where(lane == blk, 1.0, 0.0).astype(BF16)
    v_ref[0] = src[:, 128:256].astype(BF16)


def _kvfmt(pages, table, tail, npages_out):
    b, npages = table.shape
    kern = functools.partial(_kvfmt_kernel, npages=npages)
    return pl.pallas_call(
        kern,
        grid_spec=pltpu.PrefetchScalarGridSpec(
            num_scalar_prefetch=1,
            grid=(b, npages_out),
            in_specs=[pl.BlockSpec((1, PAGE, 256), lambda i, p, pt: (pt[i, jnp.minimum(p, npages - 1)], 0, 1)),
                      pl.BlockSpec((1, PAGE, 256), lambda i, p, pt: (i, 0, 1))],
            out_specs=[pl.BlockSpec((1, PAGE, 256), lambda i, p, pt: (i, p, 0)),
                       pl.BlockSpec((1, PAGE, 128), lambda i, p, pt: (i, p, 0))]),
        out_shape=[jax.ShapeDtypeStruct((b, npages_out * PAGE, 256), BF16),
                   jax.ShapeDtypeStruct((b, npages_out * PAGE, 128), BF16)],
        compiler_params=_cparams(("arbitrary", "arbitrary")),
    )(table, pages, tail)


def _group_rows(qfull, g, tq):
    lane = lax.broadcasted_iota(jnp.int32, (tq, LANES), 1)
    keep = (lane >= HEAD_DIM) if g else (lane < HEAD_DIM)
    pieces = []
    for r in range(4):
        h = 4 * g + r
        slab = qfull[:, (h // 2) * LANES:(h // 2 + 1) * LANES]
        if h % 2 != g:
            slab = pltpu.roll(slab, HEAD_DIM, 1)
        pieces.append(jnp.where(keep, slab, 0.0))
    return jnp.concatenate(pieces, axis=0)


def _masked_softmax(s, mask):
    s = jnp.where(mask, s, NEG)
    m = jnp.max(s, axis=-1, keepdims=True)
    e = jnp.where(mask, jnp.exp(s - m), 0.0)
    return e / jnp.maximum(jnp.sum(e, axis=-1, keepdims=True), 1e-30)


def _nsa_kernel(qn_ref, qr_ref, dtg_ref, kcvc_ref, ov_ref, kaug_ref, vsl_ref, kwin_ref, vwin_ref,
                y_ref, maskb, m_sc, l_sc, acc_sc, *, tq, ns, nsp, tk, wl, pos0, kw0, prompt):
    t = pl.program_id(1)
    scale = HEAD_DIM ** -0.5
    r4 = 4 * tq
    qpos = pos0 + t * tq + lax.broadcasted_iota(jnp.int32, (tq, 1), 0)
    pos4 = jnp.concatenate([qpos] * 4, axis=0)
    nslab = nsp // LANES

    qn = qn_ref[0] * scale
    qr = qr_ref[0] * scale
    kc = kcvc_ref[0, :, 0:128]
    vc_b = kcvc_ref[0, :, 128:256].astype(BF16)
    ncb = kc.shape[0]
    c_end = lax.broadcasted_iota(jnp.int32, (1, ncb), 1) * CMP_STRIDE + (CMP_BLOCK - 1)
    mask_c = c_end <= pos4

    jf = lax.broadcasted_iota(jnp.int32, (tq, nsp), 1)
    cur = qpos >> 6
    valid = jf <= cur
    forced = valid & ((jf == 0) | (jf == cur) | (jf == cur - 1))
    jff = jf.astype(F32)

    qr_g = []
    o_cmp = []
    for g in range(KV_HEADS):
        qn_g = _group_rows(qn, g, tq)
        qr_g.append(_group_rows(qr, g, tq).astype(BF16))
        p_c = _masked_softmax(_dot_nt_hi(qn_g, kc), mask_c)
        o_cmp.append(_dot(p_c.astype(BF16), vc_b))
        p4 = p_c[0:tq] + p_c[tq:2 * tq] + p_c[2 * tq:3 * tq] + p_c[3 * tq:4 * tq]
        h1, h2, h3 = _split3(p4)
        ov = ov_ref[...]
        imp = _dot(h1, ov) + (_dot(h2, ov) + _dot(h3, ov))
        score = jnp.where(forced, FORCE_BONUS, jnp.where(valid, imp, -1.0))
        score = jnp.where(jf < ns, score, -2.0)

        def pick(_, carry):
            sc, sb = carry
            mx = jnp.max(sc, axis=-1, keepdims=True)
            jm = jnp.min(jnp.where(sc == mx, jff, float(nsp)), axis=-1, keepdims=True)
            hit = jff == jm
            return jnp.where(hit, -3.0, sc), jnp.where(hit, 0.0, sb)

        _, selb = lax.fori_loop(0, TOP_N, pick, (score, jnp.full((tq, nsp), NEG, F32)))
        for sl in range(nslab):
            piece = selb[:, sl * LANES:(sl + 1) * LANES]
            maskb[sl, g] = jnp.concatenate([piece] * 4, axis=0).astype(BF16)

    m_sc[...] = jnp.full(m_sc.shape, 2.0 * NEG, F32)
    l_sc[...] = jnp.zeros(l_sc.shape, F32)
    acc_sc[...] = jnp.zeros(acc_sc.shape, F32)
    ntile = (pos0 + t * tq + tq - 1) // tk + 1

    def tile(jt, _):
        k0 = pl.multiple_of(jt * tk, tk)
        kt = kaug_ref[0, pl.ds(k0, tk), :]
        vt = vsl_ref[0, pl.ds(k0, tk), :]
        sb = k0 // (SLAB_BLOCKS * SLC_BLOCK)
        kpos = k0 + lax.broadcasted_iota(jnp.int32, (1, tk), 1)
        for g in range(KV_HEADS):
            qa = jnp.concatenate([qr_g[g], maskb[sb, g]], axis=1)
            s = jnp.where(kpos <= pos4, _dot_nt(qa, kt), NEG)
            m_old = m_sc[g]
            m_new = jnp.maximum(m_old, jnp.max(s, axis=-1, keepdims=True))
            alpha = jnp.exp(m_old - m_new)
            pe = jnp.exp(s - m_new)
            l_sc[g] = alpha * l_sc[g] + jnp.sum(pe, axis=-1, keepdims=True)
            acc_sc[g] = alpha * acc_sc[g] + _dot(pe.astype(BF16), vt)
            m_sc[g] = m_new
        return 0

    lax.fori_loop(0, ntile, tile, 0)

    if prompt:
        ws = pl.multiple_of(jnp.maximum(t * tq - WINDOW, 0), LANES)
    else:
        ws = 0
    kw = kwin_ref[0, pl.ds(ws, wl), :]
    vw = vwin_ref[0, pl.ds(ws, wl), :]
    kposw = kw0 + ws + lax.broadcasted_iota(jnp.int32, (1, wl), 1)
    dpos = pos4 - kposw
    mask_w = (dpos >= 0) & (dpos < WINDOW) & (kposw >= 0)

    sg = jax.nn.sigmoid(dtg_ref[0])
    lane = lax.broadcasted_iota(jnp.int32, (tq, LANES), 1)
    outs = []
    for g in range(KV_HEADS):
        p_w = _masked_softmax(_dot_nt(qr_g[g], kw), mask_w)
        o_w = _dot(p_w.astype(BF16), vw)
        o_s = acc_sc[g] / l_sc[g]
        gate = []
        for k in range(3):
            cols = [sg[:, 8 + 3 * (4 * g + r) + k:8 + 3 * (4 * g + r) + k + 1] for r in range(4)]
            gate.append(jnp.concatenate(cols, axis=0))
        outs.append(gate[0] * o_cmp[g] + gate[1] * o_s + gate[2] * o_w)
    for k in range(4):
        g = k // 2
        r0 = (2 * k) % 4
        a = outs[g][r0 * tq:(r0 + 1) * tq]
        b = outs[g][(r0 + 1) * tq:(r0 + 2) * tq]
        if g == 0:
            slab = jnp.where(lane < HEAD_DIM, a, pltpu.roll(b, HEAD_DIM, 1))
        else:
            slab = jnp.where(lane < HEAD_DIM, pltpu.roll(a, HEAD_DIM, 1), b)
        y_ref[0, :, k * LANES:(k + 1) * LANES] = slab


def _nsa(qn, qr, dtg, kcvc, ov, kaug, vsl, kwin, vwin, *, tq, ns, tk, wl, pos0, kw0, prompt):
    b, lq = qn.shape[:2]
    nsp = ov.shape[1]
    nslab = nsp // LANES
    kern = functools.partial(_nsa_kernel, tq=tq, ns=ns, nsp=nsp, tk=tk, wl=wl, pos0=pos0, kw0=kw0, prompt=prompt)
    rows = lambda n: pl.BlockSpec((1, tq, n), lambda i, t: (i, t, 0))
    perb = lambda arr: pl.BlockSpec((1,) + arr.shape[1:], lambda i, t: (i, 0, 0), pipeline_mode=pl.Buffered(1))
    return pl.pallas_call(
        kern,
        grid=(b, lq // tq),
        in_specs=[rows(512), rows(512), rows(128), perb(kcvc),
                  pl.BlockSpec(ov.shape, lambda i, t: (0, 0)),
                  perb(kaug), perb(vsl), perb(kwin), perb(vwin)],
        out_specs=rows(512),
        out_shape=jax.ShapeDtypeStruct((b, lq, 512), F32),
        scratch_shapes=[pltpu.VMEM((nslab, KV_HEADS, 4 * tq, LANES), BF16),
                        pltpu.VMEM((KV_HEADS, 4 * tq, 1), F32),
                        pltpu.VMEM((KV_HEADS, 4 * tq, 1), F32),
                        pltpu.VMEM((KV_HEADS, 4 * tq, LANES), F32)],
        compiler_params=_cparams(("arbitrary", "arbitrary")),
    )(qn, qr, dtg, kcvc, ov, kaug, vsl, kwin, vwin)


def _oproj_kernel(x_ref, ys_ref, yn_ref, w_ref, g_ref, r_ref, x1_ref, h_ref, lg_ref):
    a = jnp.concatenate([ys_ref[...], yn_ref[...]], axis=1).astype(BF16)
    x1 = x_ref[...] + _dot(a, w_ref[...])
    x1_ref[...] = x1
    h = _rms(x1, g_ref[...])
    h_ref[...] = h.astype(BF16)
    lg_ref[...] = _dot_hi(h, r_ref[...])


def _oproj(x, ys, yn, w, g, router, tm):
    t = x.shape[0]
    row = lambda n: pl.BlockSpec((tm, n), lambda i: (i, 0))
    full = lambda a: pl.BlockSpec(a.shape, lambda i: (0,) * a.ndim)
    return pl.pallas_call(
        _oproj_kernel,
        grid=(t // tm,),
        in_specs=[row(D_MODEL), row(512), row(512), full(w), full(g), full(router)],
        out_specs=[row(D_MODEL), row(D_MODEL), row(LANES)],
        out_shape=[jax.ShapeDtypeStruct((t, D_MODEL), F32), jax.ShapeDtypeStruct((t, D_MODEL), BF16),
                   jax.ShapeDtypeStruct((t, LANES), F32)],
        compiler_params=_cparams(("arbitrary",)),
    )(x, ys, yn, w, g, router)


def _ffn_kernel(h_ref, x1_ref, wg_ref, wu_ref, wd_ref, o_ref):
    f = pl.program_id(1)

    @pl.when(f == 0)
    def _():
        o_ref[...] = x1_ref[...]

    hb = h_ref[...]
    act = _silu(_dot(hb, wg_ref[...])) * _dot(hb, wu_ref[...])
    o_ref[...] += _dot(act.astype(BF16), wd_ref[...])


def _ffn(h, x1, wg, wu, wd, tm, tf):
    t = h.shape[0]
    dff = wg.shape[1]
    return pl.pallas_call(
        _ffn_kernel,
        grid=(t // tm, dff // tf),
        in_specs=[pl.BlockSpec((tm, D_MODEL), lambda i, f: (i, 0)),
                  pl.BlockSpec((tm, D_MODEL), lambda i, f: (i, 0)),
                  pl.BlockSpec((D_MODEL, tf), lambda i, f: (0, f)),
                  pl.BlockSpec((D_MODEL, tf), lambda i, f: (0, f)),
                  pl.BlockSpec((tf, D_MODEL), lambda i, f: (f, 0))],
        out_specs=pl.BlockSpec((tm, D_MODEL), lambda i, f: (i, 0)),
        out_shape=jax.ShapeDtypeStruct((t, D_MODEL), F32),
        compiler_params=_cparams(("arbitrary", "arbitrary")),
    )(h, x1, wg, wu, wd)


def _moe_kernel(be_ref, nb_ref, xb_ref, wg_ref, wu_ref, wd_ref, o_ref):
    b = pl.program_id(0)
    f = pl.program_id(1)

    @pl.when(f == 0)
    def _():
        o_ref[...] = jnp.zeros(o_ref.shape, F32)

    @pl.when(b < nb_ref[0])
    def _():
        hb = xb_ref[...]
        act = _silu(_dot(hb, wg_ref[0])) * _dot(hb, wu_ref[0])
        o_ref[...] += _dot(act.astype(BF16), wd_ref[0])


def _moe(blk_e, nblk_used, xb, wg, wu, wd, tmb, tf):
    n_blk = xb.shape[0] // tmb
    dex = wg.shape[2]
    nf = dex // tf

    def feff(b, f, nb):
        return jnp.where(b < nb[0], f, nf - 1)

    return pl.pallas_call(
        _moe_kernel,
        grid_spec=pltpu.PrefetchScalarGridSpec(
            num_scalar_prefetch=2,
            grid=(n_blk, nf),
            in_specs=[pl.BlockSpec((tmb, D_MODEL), lambda b, f, be, nb: (b, 0)),
                      pl.BlockSpec((1, D_MODEL, tf), lambda b, f, be, nb: (be[b], 0, feff(b, f, nb))),
                      pl.BlockSpec((1, D_MODEL, tf), lambda b, f, be, nb: (be[b], 0, feff(b, f, nb))),
                      pl.BlockSpec((1, tf, D_MODEL), lambda b, f, be, nb: (be[b], feff(b, f, nb), 0))],
            out_specs=pl.BlockSpec((tmb, D_MODEL), lambda b, f, be, nb: (b, 0))),
        out_shape=jax.ShapeDtypeStruct((n_blk * tmb, D_MODEL), F32),
        compiler_params=_cparams(("arbitrary", "arbitrary")),
    )(blk_e, nblk_used, xb, wg, wu, wd)


def _moe_layer(h, x1, logits, wg, wu, wd, tmb, tf):
    t = h.shape[0]
    top_v, top_e = lax.top_k(logits[:, :N_EXPERTS], TOP_K)
    gate = jax.nn.softmax(top_v, axis=-1)
    e_flat = top_e.reshape(-1)
    tok = jnp.repeat(jnp.arange(t, dtype=jnp.int32), TOP_K)
    order = jnp.argsort(e_flat)
    e_s = e_flat[order]
    tok_s = tok[order]
    counts = jnp.bincount(e_flat, length=N_EXPERTS)
    starts = jnp.cumsum(counts) - counts
    padded = (counts + tmb - 1) // tmb * tmb
    pends = jnp.cumsum(padded)
    pstarts = pends - padded
    dest_s = (pstarts[e_s] + jnp.arange(t * TOP_K, dtype=jnp.int32) - starts[e_s]).astype(jnp.int32)
    n_blk = -(-(t * TOP_K) // tmb) + N_EXPERTS
    buf_tok = jnp.full((n_blk * tmb,), t, dtype=jnp.int32).at[dest_s].set(tok_s)
    blk_e = jnp.minimum(jnp.searchsorted(pends, jnp.arange(n_blk, dtype=jnp.int32) * tmb, side='right'),
                        N_EXPERTS - 1).astype(jnp.int32)
    nblk_used = (pends[-1] // tmb).astype(jnp.int32).reshape(1)
    h_pad = jnp.concatenate([h, jnp.zeros((1, D_MODEL), h.dtype)], axis=0)
    xb = h_pad[buf_tok]
    out = _moe(blk_e, nblk_used, xb, wg, wu, wd, tmb, tf)
    dest = jnp.zeros((t * TOP_K,), jnp.int32).at[order].set(dest_s).reshape(t, TOP_K)
    return x1 + out[dest[:, 0]] * gate[:, 0:1] + out[dest[:, 1]] * gate[:, 1:2]


def _final_kernel(x_ref, g_ref, o_ref):
    o_ref[...] = _rms(x_ref[...], g_ref[...])


def _final_norm(x, g, tm):
    t = x.shape[0]
    return pl.pallas_call(
        _final_kernel,
        grid=(t // tm,),
        in_specs=[pl.BlockSpec((tm, D_MODEL), lambda i: (i, 0)), pl.BlockSpec(g.shape, lambda i: (0, 0))],
        out_specs=pl.BlockSpec((tm, D_MODEL), lambda i: (i, 0)),
        out_shape=jax.ShapeDtypeStruct((t, D_MODEL), F32),
        compiler_params=_cparams(("arbitrary",)),
    )(x, g)


def _rope_tables(pos):
    inv = 1.0 / (ROPE_THETA ** (jnp.arange(0, HEAD_DIM, 2, dtype=F32) / HEAD_DIM))
    ang = pos.astype(F32)[:, None] * inv[None, :]
    cos = jnp.cos(ang)
    sin = jnp.sin(ang)
    return jnp.concatenate([cos, cos, cos, cos], axis=1), jnp.concatenate([-sin, sin, -sin, sin], axis=1)


def _overlap(ncb, nsp, ns):
    c = jnp.arange(ncb, dtype=jnp.int32)[:, None] * CMP_STRIDE
    s = jnp.arange(nsp, dtype=jnp.int32)[None, :] * SLC_BLOCK
    ov = (c <= s + SLC_BLOCK - 1) & (c + CMP_BLOCK - 1 >= s) & (jnp.arange(nsp)[None, :] < ns)
    return ov.astype(BF16)


def _pad_lanes(v, n):
    return jnp.zeros((1, n), F32).at[0, :v.shape[0]].set(v.astype(F32))


def kernel(x_prompt, x_sample, cache_kv, cache_win, state_ssm, state_conv, page_table, ln_mix, w_in, conv_w,
           conv_b, dt_bias, a_log, d_skip, ssd_norm, cmp_pos, cmp_w1, cmp_b1, cmp_w2, w_out, ln_ffn,
           ffn_w_gate, ffn_w_up, ffn_w_down, moe_router, moe_w_gate, moe_w_up, moe_w_down, ln_final):
    depth = w_in.shape[0]
    bsz, seq = x_prompt.shape[:2]
    dbsz, dseq = x_sample.shape[:2]
    n_pages = page_table.shape[1]
    past = n_pages * cache_kv.shape[2]
    win_buf = cache_win.shape[2]
    npool = cache_kv.shape[1]
    assert bsz == 1 and cache_kv.shape[2] == PAGE and seq % PAGE == 0 and seq >= WINDOW + PAGE
    assert dseq <= 8 and win_buf == WINDOW and past % PAGE == 0
    assert not any(past <= CMP_STRIDE * c + CMP_BLOCK - 1 <= past + dseq - 1
                   for c in range(past // CMP_STRIDE - 2, past // CMP_STRIDE + 2))

    ts = dbsz * dseq
    tq_s = 8
    tk_s = 512
    kpages_s = -(-(past + PAGE) // tk_s) * tk_s // PAGE
    ns_p = seq // SLC_BLOCK
    ns_s = -(-(past + dseq) // SLC_BLOCK)
    nsp_p = -(-ns_p // LANES) * LANES
    nsp_s = -(-ns_s // LANES) * LANES
    wl_s = 640

    cs_p, sn_p = _rope_tables(jnp.arange(seq, dtype=jnp.int32))
    cs_s, sn_s = _rope_tables(past + jnp.arange(ts, dtype=jnp.int32) % dseq)
    ov_p = _overlap(seq // CMP_STRIDE, nsp_p, ns_p)
    ov_s = _overlap(past // CMP_STRIDE, nsp_s, ns_s)
    ident = jnp.arange(seq // PAGE, dtype=jnp.int32).reshape(1, -1)

    xp = x_prompt.reshape(seq, D_MODEL)
    xs = x_sample.reshape(ts, D_MODEL)
    outs = {k: [] for k in ('kv_p', 'kv_s', 'win_p', 'win_s', 'ssm_p', 'ssm_s', 'conv_p', 'conv_s')}

    for l in range(depth):
        w = w_in[l]
        wr = jnp.concatenate([w[:, :1536], w[:, 1544:2824], w[:, 1536:1544], w[:, 2824:2848],
                              jnp.zeros((D_MODEL, C_END - 2848), F32)], axis=1).astype(BF16)
        g_mix = ln_mix[l].reshape(1, -1)
        cw8 = jnp.zeros((8, CONV_DIM), F32).at[:CONV_W].set(conv_w[l])
        cb = conv_b[l].reshape(1, -1)
        dtb = _pad_lanes(dt_bias[l], LANES)
        a_neg = _pad_lanes(-jnp.exp(a_log[l].astype(F32)), LANES)
        dsk = jnp.repeat(d_skip[l].astype(F32), SSD_HEAD_DIM).reshape(1, -1)
        ng = ssd_norm[l].reshape(1, -1)
        w1 = cmp_w1[l].reshape(2, 2, CMP_STRIDE, HEAD_DIM, CMP_HIDDEN)
        eye = jnp.eye(KV_HEADS, dtype=F32)
        w1e = jnp.einsum('vspdh,gk->vspgdkh', w1, eye).reshape(2, 2, CMP_STRIDE * LANES, 256).astype(BF16)
        pe = cmp_pos[l].reshape(2, 2, CMP_STRIDE, 1, HEAD_DIM)
        pee = jnp.broadcast_to(pe, (2, 2, CMP_STRIDE, KV_HEADS, HEAD_DIM)).reshape(2, 2, CMP_STRIDE * LANES)
        b1e = jnp.concatenate([cmp_b1[l], cmp_b1[l]], axis=1)
        w2e = jnp.einsum('vhd,gk->vghkd', cmp_w2[l], eye).reshape(2, 256, LANES).astype(BF16)
        wo = w_out[l].astype(BF16)
        g_ffn = ln_ffn[l].reshape(1, -1)
        if l % 2 == 0:
            router = jnp.zeros((D_MODEL, LANES), F32)
        else:
            router = jnp.zeros((D_MODEL, LANES), F32).at[:, :N_EXPERTS].set(moe_router[l // 2])

        z, xbc, dtg, q, qr, kv, win = _front(xp, g_mix, wr, cs_p, sn_p, 256)
        y_ssd, hist, hfin = _ssd(z[None], xbc[None], dtg[None], jnp.zeros((1, 8, CONV_DIM), F32),
                                 jnp.zeros((1, D_SSD, SSD_STATE), F32), cw8, cb, dtb, a_neg, dsk, ng, SSD_CHUNK)
        kv_pages = kv.reshape(seq // PAGE, PAGE, 512)
        kcvc = _compress(kv_pages, ident, w1e, pee, b1e, w2e)
        kaug, vsl = _kvfmt(kv_pages, ident, jnp.zeros((1, PAGE, 512), F32), seq // PAGE)
        y_nsa = _nsa(q[None], qr[None], dtg[None], kcvc, ov_p, kaug, vsl,
                     win[None, :, 0:128].astype(BF16), win[None, :, 128:256].astype(BF16),
                     tq=128, ns=ns_p, tk=512, wl=WINDOW + 128, pos0=0, kw0=0, prompt=True)
        x1, h, logits = _oproj(xp, y_ssd[0], y_nsa[0], wo, g_ffn, router, 512)
        outs['kv_p'].append(kv.reshape(1, seq, 4, KV_HEADS, HEAD_DIM))
        outs['win_p'].append(win[seq - WINDOW:].reshape(1, WINDOW, 2, KV_HEADS, HEAD_DIM))
        outs['ssm_p'].append(hfin.reshape(1, SSD_HEADS, SSD_HEAD_DIM, SSD_STATE))
        outs['conv_p'].append(hist[:, 5:8])

        zs, xbcs, dtgs, qs, qrs, kvs, wins = _front(xs, g_mix, wr, cs_s, sn_s, ts)
        padl = lambda v: jnp.pad(v.reshape(dbsz, dseq, -1), ((0, 0), (0, SSD_CHUNK - dseq), (0, 0)))
        hist8 = jnp.pad(state_conv[l], ((0, 0), (8 - (CONV_W - 1), 0), (0, 0)))
        y_ssd_s, hist_s, hfin_s = _ssd(padl(zs), padl(xbcs), padl(dtgs), hist8,
                                       state_ssm[l].reshape(dbsz, D_SSD, SSD_STATE),
                                       cw8, cb, dtb, a_neg, dsk, ng, dseq)
        pool = cache_kv[l].reshape(npool, PAGE, 512)
        kcvc_s = _compress(pool, page_table, w1e, pee, b1e, w2e)
        tail = jnp.pad(kvs.reshape(dbsz, dseq, 512), ((0, 0), (0, PAGE - dseq), (0, 0)))
        kaug_s, vsl_s = _kvfmt(pool, page_table, tail, kpages_s)
        win_all = jnp.concatenate([cache_win[l].reshape(dbsz, win_buf, 256), wins.reshape(dbsz, dseq, 256)], axis=1)
        win_pad = jnp.pad(win_all, ((0, 0), (0, wl_s - win_buf - dseq), (0, 0))).astype(BF16)
        padq = lambda v: jnp.pad(v.reshape(dbsz, dseq, -1), ((0, 0), (0, tq_s - dseq), (0, 0)))
        y_nsa_s = _nsa(padq(qs), padq(qrs), padq(dtgs), kcvc_s, ov_s, kaug_s, vsl_s,
                       win_pad[:, :, 0:128], win_pad[:, :, 128:256],
                       tq=tq_s, ns=ns_s, tk=tk_s, wl=wl_s, pos0=past, kw0=past - win_buf, prompt=False)
        y_nsa_s = y_nsa_s[:, :dseq].reshape(ts, D_NSA)
        x1s, hs, logits_s = _oproj(xs, y_ssd_s[:, :dseq].reshape(ts, D_SSD), y_nsa_s, wo, g_ffn, router, ts)
        outs['kv_s'].append(kvs.reshape(dbsz, dseq, 4, KV_HEADS, HEAD_DIM))
        outs['win_s'].append(win_all[:, dseq:].reshape(dbsz, win_buf, 2, KV_HEADS, HEAD_DIM))
        outs['ssm_s'].append(hfin_s.reshape(dbsz, SSD_HEADS, SSD_HEAD_DIM, SSD_STATE))
        outs['conv_s'].append(hist_s[:, 5:8])

        if l % 2 == 0:
            wg = ffn_w_gate[l // 2].astype(BF16)
            wu = ffn_w_up[l // 2].astype(BF16)
            wd = ffn_w_down[l // 2].astype(BF16)
            xp = _ffn(h, x1, wg, wu, wd, 1024, 256)
            xs = _ffn(hs, x1s, wg, wu, wd, ts, 256)
        else:
            wg = moe_w_gate[l // 2].astype(BF16)
            wu = moe_w_up[l // 2].astype(BF16)
            wd = moe_w_down[l // 2].astype(BF16)
            xp = _moe_layer(h, x1, logits, wg, wu, wd, 1024, 512)
            xs = _moe_layer(hs, x1s, logits_s, wg, wu, wd, 128, 512)

    g_fin = ln_final.reshape(1, -1)
    y_prompt = _final_norm(xp, g_fin, 1024).reshape(bsz, seq, D_MODEL)
    y_sample = _final_norm(xs, g_fin, ts).reshape(dbsz, dseq, D_MODEL)
    st = lambda k: jnp.stack(outs[k])
    return (y_prompt, y_sample, st('kv_p'), st('kv_s'), st('win_p'), st('win_s'),
            st('ssm_p'), st('ssm_s'), st('conv_p'), st('conv_s'))
```

```python
import functools
import math

import jax
import jax.numpy as jnp
from jax import lax
from jax.experimental import pallas as pl
from jax.experimental.pallas import tpu as pltpu

F32 = jnp.float32
BF16 = jnp.bfloat16

D_MODEL = 1024
SSD_HEADS = 8
SSD_HEAD_DIM = 64
D_SSD = 512
SSD_GROUPS = 2
SSD_STATE = 128
CONV_W = 4
CONV_DIM = 1024
SSD_CHUNK = 128
NSA_HEADS = 8
HEAD_DIM = 64
KV_HEADS = 2
D_NSA = 512
CMP_BLOCK = 32
CMP_STRIDE = 16
CMP_HIDDEN = 128
SLC_BLOCK = 64
TOP_N = 16
WINDOW = 512
ROPE_THETA = 10000.0
FORCE_BONUS = 1.0e4
N_EXPERTS = 8
TOP_K = 2
EPS = 1e-6

PAGE = 128
LANES = 128
LOG2E = 1.4426950408889634
NEG = -1.0e30
SLAB_BLOCKS = 128
VMEM_LIMIT = 56 * 1024 * 1024

C_Z, C_XBC, C_Q, C_KV, C_DT, C_END = 0, 512, 1536, 2048, 2816, 2944


def _cparams(sem):
    return pltpu.CompilerParams(dimension_semantics=sem, vmem_limit_bytes=VMEM_LIMIT)


def _dot(a, b):
    return jnp.dot(a, b, preferred_element_type=F32)


def _dot_nt(a, b):
    return lax.dot_general(a, b, (((1,), (1,)), ((), ())), preferred_element_type=F32)


def _split3(x):
    h1 = x.astype(BF16)
    r1 = x - h1.astype(F32)
    h2 = r1.astype(BF16)
    h3 = (r1 - h2.astype(F32)).astype(BF16)
    return h1, h2, h3


def _dot_nt_hi(a, b):
    a1, a2, _ = _split3(a)
    b1, b2, _ = _split3(b)
    return _dot_nt(a1, b1) + (_dot_nt(a1, b2) + _dot_nt(a2, b1))


def _dot_hi(a, b):
    a1, a2, _ = _split3(a)
    b1, b2, _ = _split3(b)
    return _dot(a1, b1) + (_dot(a1, b2) + _dot(a2, b1))


def _silu(x):
    return x * jax.nn.sigmoid(x)


def _rms(x, g):
    return x * lax.rsqrt(jnp.mean(x * x, axis=-1, keepdims=True) + EPS) * g


def _front_kernel(x_ref, g_ref, w_ref, cs_ref, sn_ref,
                  z_ref, xbc_ref, dtg_ref, q_ref, qr_ref, kv_ref, win_ref):
    xn = _rms(x_ref[...], g_ref[...])
    p = _dot(xn.astype(BF16), w_ref[...])
    cs = cs_ref[...]
    sn = sn_ref[...]
    lane = lax.broadcasted_iota(jnp.int32, cs.shape, 1)
    first = (lane % HEAD_DIM) < (HEAD_DIM // 2)

    def rope(v):
        sw = jnp.where(first, pltpu.roll(v, LANES - HEAD_DIM // 2, 1), pltpu.roll(v, HEAD_DIM // 2, 1))
        return v * cs + sw * sn

    z_ref[...] = p[:, C_Z:C_XBC]
    xbc_ref[...] = p[:, C_XBC:C_Q]
    dtg_ref[...] = p[:, C_DT:C_END]
    q_ref[...] = p[:, C_Q:C_KV]
    for s in range(4):
        qr_ref[:, s * LANES:(s + 1) * LANES] = rope(p[:, C_Q + s * LANES:C_Q + (s + 1) * LANES])
    kv_ref[:, 0:256] = p[:, C_KV:C_KV + 256]
    kv_ref[:, 256:384] = rope(p[:, C_KV + 256:C_KV + 384])
    kv_ref[:, 384:512] = p[:, C_KV + 384:C_KV + 512]
    win_ref[:, 0:128] = rope(p[:, C_KV + 512:C_KV + 640])
    win_ref[:, 128:256] = p[:, C_KV + 640:C_KV + 768]


def _front(x, g, w, cs, sn, tm):
    t = x.shape[0]
    row = lambda n: pl.BlockSpec((tm, n), lambda i: (i, 0))
    full = lambda a: pl.BlockSpec(a.shape, lambda i: (0,) * a.ndim)
    outs = [512, 1024, 128, 512, 512, 512, 256]
    return pl.pallas_call(
        _front_kernel,
        grid=(t // tm,),
        in_specs=[row(D_MODEL), full(g), full(w), row(LANES), row(LANES)],
        out_specs=[row(n) for n in outs],
        out_shape=[jax.ShapeDtypeStruct((t, n), F32) for n in outs],
        compiler_params=_cparams(("arbitrary",)),
    )(x, g, w, cs, sn)


def _expand_heads(v):
    r = v.shape[0]
    lane = lax.broadcasted_iota(jnp.int32, (r, LANES), 1)
    outs = []
    for k in range(4):
        a = jnp.broadcast_to(v[:, 2 * k:2 * k + 1], (r, LANES))
        b = jnp.broadcast_to(v[:, 2 * k + 1:2 * k + 2], (r, LANES))
        outs.append(jnp.where(lane < SSD_HEAD_DIM, a, b))
    return jnp.concatenate(outs, axis=1)


def _ssd_kernel(z_ref, xbc_ref, dtg_ref, hist_ref, h0_ref, cw_ref, cb_ref, dtb_ref, a_ref, dsk_ref, ng_ref,
                y_ref, hist_o_ref, hfin_ref, xpad, ht, *, nchunks, valid_last):
    c = pl.program_id(1)
    q = SSD_CHUNK

    @pl.when(c == 0)
    def _():
        xpad[0:8, :] = hist_ref[0]
        ht[...] = h0_ref[0].T

    xpad[8:8 + q, :] = xbc_ref[0]
    conv = cb_ref[...]
    for k in range(CONV_W):
        conv = conv + xpad[5 + k:5 + k + q, :] * cw_ref[k:k + 1, :]
    xc = _silu(conv)
    xs = xc[:, 0:D_SSD]
    bm = xc[:, D_SSD:D_SSD + 256]
    cm = xc[:, D_SSD + 256:D_SSD + 512]

    row = lax.broadcasted_iota(jnp.int32, (q, q), 0)
    col = lax.broadcasted_iota(jnp.int32, (q, q), 1)
    nvalid = jnp.where(c == nchunks - 1, valid_last, q)
    xdt_in = dtg_ref[0] + dtb_ref[...]
    dt = jnp.maximum(xdt_in, 0.0) + jnp.log1p(jnp.exp(-jnp.abs(xdt_in)))
    dt = jnp.where(row < nvalid, dt, 0.0)
    dta = dt * a_ref[...]
    causal = row >= col
    cum = jnp.dot(causal.astype(F32), dta, preferred_element_type=F32, precision=lax.Precision.HIGHEST)
    cum_t = cum.T
    ecum = jnp.exp(cum)
    toend = jnp.exp(cum[q - 1:q, :] - cum)
    dt_e = _expand_heads(dt)
    ecum_e = _expand_heads(ecum)
    toend_e = _expand_heads(toend)
    xdt = xs * dt_e
    xdt_b = xdt.astype(BF16)
    xw_b = (xdt * toend_e).astype(BF16)
    lane = lax.broadcasted_iota(jnp.int32, (q, LANES), 1)

    y_intra = []
    y_inter = []
    for g in range(SSD_GROUPS):
        bg = bm[:, g * 128:(g + 1) * 128]
        cg_b = cm[:, g * 128:(g + 1) * 128].astype(BF16)
        bg_b = bg.astype(BF16)
        cb = _dot_nt(cg_b, bg_b)
        htg = ht[:, g * 256:(g + 1) * 256]
        y_inter.append(_dot(cg_b, htg.astype(BF16)))
        for kk in range(2):
            slab = xdt_b[:, (2 * g + kk) * LANES:(2 * g + kk + 1) * LANES]
            res = []
            for hh in range(2):
                h = 4 * g + 2 * kk + hh
                seg = cum[:, h:h + 1] - cum_t[h:h + 1, :]
                decay = jnp.exp(jnp.where(causal, seg, NEG))
                res.append(_dot((decay * cb).astype(BF16), slab))
            y_intra.append(jnp.where(lane < SSD_HEAD_DIM, res[0], res[1]))
        s_new = _dot(bg.T.astype(BF16), xw_b[:, g * 256:(g + 1) * 256])
        ht[:, g * 256:(g + 1) * 256] = htg * ecum_e[q - 1:q, g * 256:(g + 1) * 256] + s_new

    y = jnp.concatenate(y_intra, axis=1) + jnp.concatenate(y_inter, axis=1) * ecum_e + dsk_ref[...] * xs
    y = y * _silu(z_ref[0])
    y_ref[0] = _rms(y, ng_ref[...])

    xpad[0:8, :] = xpad[q:q + 8, :]

    @pl.when(c == nchunks - 1)
    def _():
        hist_o_ref[0] = xpad[valid_last:valid_last + 8, :]
        hfin_ref[0] = ht[...].T


def _ssd(z, xbc, dtg, hist8, h0, cw8, cb, dtb, a, dsk, ng, valid_last):
    b, l = z.shape[:2]
    nchunks = l // SSD_CHUNK
    rows = lambda n: pl.BlockSpec((1, SSD_CHUNK, n), lambda i, c: (i, c, 0))
    perb = lambda r, n: pl.BlockSpec((1, r, n), lambda i, c: (i, 0, 0))
    full = lambda arr: pl.BlockSpec(arr.shape, lambda i, c: (0,) * arr.ndim)
    kern = functools.partial(_ssd_kernel, nchunks=nchunks, valid_last=valid_last)
    return pl.pallas_call(
        kern,
        grid=(b, nchunks),
        in_specs=[rows(512), rows(1024), rows(128), perb(8, 1024), perb(512, 128),
                  full(cw8), full(cb), full(dtb), full(a), full(dsk), full(ng)],
        out_specs=[rows(512), perb(8, 1024), perb(512, 128)],
        out_shape=[jax.ShapeDtypeStruct((b, l, 512), F32),
                   jax.ShapeDtypeStruct((b, 8, 1024), F32),
                   jax.ShapeDtypeStruct((b, 512, 128), F32)],
        scratch_shapes=[pltpu.VMEM((SSD_CHUNK + 8, CONV_DIM), F32), pltpu.VMEM((SSD_STATE, D_SSD), F32)],
        compiler_params=_cparams(("arbitrary", "arbitrary")),
    )(z, xbc, dtg, hist8, h0, cw8, cb, dtb, a, dsk, ng)


CMP_GROUP_PAGES = 16


def _compress_kernel(pt_ref, pgk_ref, pgv_ref, w1_ref, pe_ref, b1_ref, w2_ref, out_ref, xs, ps, *, npages):
    p = pl.program_id(1)
    sub = p % CMP_GROUP_PAGES
    hb = PAGE // CMP_STRIDE
    r0 = pl.multiple_of(sub * hb, hb)
    for kv, pg_ref in enumerate((pgk_ref, pgv_ref)):
        for pos in range(CMP_STRIDE):
            xs[kv, pl.ds(r0, hb), pos * LANES:(pos + 1) * LANES] = pg_ref[0, pl.ds(pos, hb, stride=CMP_STRIDE), :]

    @pl.when(sub == CMP_GROUP_PAGES - 1)
    def _():
        rows = CMP_GROUP_PAGES * hb
        g0 = pl.multiple_of((p // CMP_GROUP_PAGES) * rows, rows)
        for kv in range(2):
            x = xs[kv]
            for s in range(2):
                ps[kv, s, pl.ds(g0, rows), :] = _dot((x + pe_ref[kv, s:s + 1, :]).astype(BF16), w1_ref[kv, s])

    @pl.when(p == npages - 1)
    def _():
        nh = npages * hb
        for kv in range(2):
            hid = b1_ref[kv:kv + 1, :] + ps[kv, 0] + pltpu.roll(ps[kv, 1], nh - 1, 0)
            out_ref[0, :, kv * LANES:(kv + 1) * LANES] = _dot(jax.nn.gelu(hid).astype(BF16), w2_ref[kv])


def _compress(pages, table, w1e, pee, b1e, w2e):
    b, npages = table.shape
    nh = npages * (PAGE // CMP_STRIDE)
    kern = functools.partial(_compress_kernel, npages=npages)
    full = lambda arr: pl.BlockSpec(arr.shape, lambda i, p, pt: (0,) * arr.ndim)
    return pl.pallas_call(
        kern,
        grid_spec=pltpu.PrefetchScalarGridSpec(
            num_scalar_prefetch=1,
            grid=(b, npages),
            in_specs=[pl.BlockSpec((1, PAGE, LANES), lambda i, p, pt: (pt[i, p], 0, 0)),
                      pl.BlockSpec((1, PAGE, LANES), lambda i, p, pt: (pt[i, p], 0, 1)),
                      full(w1e), full(pee), full(b1e), full(w2e)],
            out_specs=pl.BlockSpec((1, nh, 256), lambda i, p, pt: (i, 0, 0)),
            scratch_shapes=[pltpu.VMEM((2, CMP_GROUP_PAGES * 8, CMP_STRIDE * LANES), F32),
                            pltpu.VMEM((2, 2, nh, 256), F32)]),
        out_shape=jax.ShapeDtypeStruct((b, nh, 256), F32),
        compiler_params=_cparams(("arbitrary", "arbitrary")),
    )(table, pages, pages, w1e, pee, b1e, w2e)


def _kvfmt_kernel(pt_ref, pg_ref, tail_ref, k_ref, v_ref, *, npages):
    p = pl.program_id(1)
    src = jnp.where(p < npages, pg_ref[0], jnp.where(p == npages, tail_ref[0], 0.0))
    r = lax.broadcasted_iota(jnp.int32, (PAGE, LANES), 0)
    lane = lax.broadcasted_iota(jnp.int32, (PAGE, LANES), 1)
    blk = (2 * p + r // SLC_BLOCK) % SLAB_BLOCKS
    k_ref[0, :, 0:128] = src[:, 0:128].astype(BF16)
    k_ref[0, :, 128:256] = jnp.where(lane == blk, 1.0, 0.0).astype(BF16)
    v_ref[0] = src[:, 128:256].astype(BF16)


def _kvfmt(pages, table, tail, npages_out):
    b, npages = table.shape
    kern = functools.partial(_kvfmt_kernel, npages=npages)
    return pl.pallas_call(
        kern,
        grid_spec=pltpu.PrefetchScalarGridSpec(
            num_scalar_prefetch=1,
            grid=(b, npages_out),
            in_specs=[pl.BlockSpec((1, PAGE, 256), lambda i, p, pt: (pt[i, jnp.minimum(p, npages - 1)], 0, 1)),
                      pl.BlockSpec((1, PAGE, 256), lambda i, p, pt: (i, 0, 1))],
            out_specs=[pl.BlockSpec((1, PAGE, 256), lambda i, p, pt: (i, p, 0)),
                       pl.BlockSpec((1, PAGE, 128), lambda i, p, pt: (i, p, 0))]),
        out_shape=[jax.ShapeDtypeStruct((b, npages_out * PAGE, 256), BF16),
                   jax.ShapeDtypeStruct((b, npages_out * PAGE, 128), BF16)],
        compiler_params=_cparams(("arbitrary", "arbitrary")),
    )(table, pages, tail)


def _group_rows(qfull, g, tq):
    lane = lax.broadcasted_iota(jnp.int32, (tq, LANES), 1)
    keep = (lane >= HEAD_DIM) if g else (lane < HEAD_DIM)
    pieces = []
    for r in range(4):
        h = 4 * g + r
        slab = qfull[:, (h // 2) * LANES:(h // 2 + 1) * LANES]
        if h % 2 != g:
            slab = pltpu.roll(slab, HEAD_DIM, 1)
        pieces.append(jnp.where(keep, slab, 0.0))
    return jnp.concatenate(pieces, axis=0)


def _masked_softmax(s, mask):
    s = jnp.where(mask, s, NEG)
    m = jnp.max(s, axis=-1, keepdims=True)
    e = jnp.where(mask, jnp.exp(s - m), 0.0)
    return e / jnp.maximum(jnp.sum(e, axis=-1, keepdims=True), 1e-30)


def _importance(p4, ov):
    h1, h2, h3 = _split3(p4)
    return _dot(h1, ov) + (_dot(h2, ov) + _dot(h3, ov))


def _topk_bias(imp, qpos, ns):
    tq, nsp = imp.shape
    jf = lax.broadcasted_iota(jnp.int32, (tq, nsp), 1)
    jff = jf.astype(F32)
    cur = qpos >> 6
    valid = jf <= cur
    forced = valid & ((jf == 0) | (jf == cur) | (jf == cur - 1))
    score = jnp.where(forced, FORCE_BONUS, jnp.where(valid, imp, -1.0))
    score = jnp.where(jf < ns, score, -2.0)

    def pick(_, carry):
        sc, sb = carry
        mx = jnp.max(sc, axis=-1, keepdims=True)
        jm = jnp.min(jnp.where(sc == mx, jff, float(nsp)), axis=-1, keepdims=True)
        hit = jff == jm
        return jnp.where(hit, -3.0, sc), jnp.where(hit, 0.0, sb)

    return lax.fori_loop(0, TOP_N, pick, (score, jnp.full((tq, nsp), NEG, F32)))[1]


def _gate_cols(sg, g, k, tq):
    return jnp.concatenate([sg[:, 8 + 3 * (4 * g + r) + k:8 + 3 * (4 * g + r) + k + 1] for r in range(4)], axis=0)


def _emit_heads(outs, tq, y_ref):
    lane = lax.broadcasted_iota(jnp.int32, (tq, LANES), 1)
    for k in range(4):
        g = k // 2
        r0 = (2 * k) % 4
        a = outs[g][r0 * tq:(r0 + 1) * tq]
        b = outs[g][(r0 + 1) * tq:(r0 + 2) * tq]
        if g == 0:
            slab = jnp.where(lane < HEAD_DIM, a, pltpu.roll(b, HEAD_DIM, 1))
        else:
            slab = jnp.where(lane < HEAD_DIM, pltpu.roll(a, HEAD_DIM, 1), b)
        y_ref[0, :, k * LANES:(k + 1) * LANES] = slab


def _nsa_kernel(qn_ref, qr_ref, dtg_ref, kcvc_ref, ov_ref, kaug_ref, vsl_ref, kwin_ref, vwin_ref,
                y_ref, maskb, m_sc, l_sc, acc_sc, *, tq, ns, nsp, tk, wl):
    t = pl.program_id(1)
    scale = HEAD_DIM ** -0.5
    qpos = t * tq + lax.broadcasted_iota(jnp.int32, (tq, 1), 0)
    pos4 = jnp.concatenate([qpos] * 4, axis=0)
    nslab = nsp // LANES

    qn = qn_ref[0] * scale
    qr = qr_ref[0] * scale
    kc = kcvc_ref[0, :, 0:128]
    vc_b = kcvc_ref[0, :, 128:256].astype(BF16)
    ncb = kc.shape[0]
    c_end = lax.broadcasted_iota(jnp.int32, (1, ncb), 1) * CMP_STRIDE + (CMP_BLOCK - 1)
    mask_c = c_end <= pos4

    qr_g = []
    qs_g = []
    o_cmp = []
    for g in range(KV_HEADS):
        qn_g = _group_rows(qn, g, tq)
        qr_f = _group_rows(qr, g, tq)
        qr_g.append(qr_f.astype(BF16))
        qs_g.append((qr_f * LOG2E).astype(BF16))
        p_c = _masked_softmax(_dot_nt_hi(qn_g, kc), mask_c)
        o_cmp.append(_dot(p_c.astype(BF16), vc_b))
        p4 = p_c[0:tq] + p_c[tq:2 * tq] + p_c[2 * tq:3 * tq] + p_c[3 * tq:4 * tq]
        selb = _topk_bias(_importance(p4, ov_ref[...]), qpos, ns)
        for sl in range(nslab):
            piece = selb[:, sl * LANES:(sl + 1) * LANES]
            maskb[sl, g] = jnp.concatenate([piece] * 4, axis=0).astype(BF16)

    m_sc[...] = jnp.full(m_sc.shape, 2.0 * NEG, F32)
    l_sc[...] = jnp.zeros(l_sc.shape, F32)
    acc_sc[...] = jnp.zeros(acc_sc.shape, F32)
    ndiag = (t * tq + tq - 1) // tk

    def tile(jt, causal):
        k0 = pl.multiple_of(jt * tk, tk)
        kt = kaug_ref[0, pl.ds(k0, tk), :]
        vt = vsl_ref[0, pl.ds(k0, tk), :]
        sb = k0 // (SLAB_BLOCKS * SLC_BLOCK)
        for g in range(KV_HEADS):
            qa = jnp.concatenate([qs_g[g], maskb[sb, g]], axis=1)
            s = _dot_nt(qa, kt)
            if causal:
                kpos = k0 + lax.broadcasted_iota(jnp.int32, (1, tk), 1)
                s = jnp.where(kpos <= pos4, s, NEG)
            m_old = m_sc[g]
            m_new = jnp.maximum(m_old, jnp.max(s, axis=-1, keepdims=True))
            alpha = jnp.exp2(m_old - m_new)
            pe = jnp.exp2(s - m_new)
            l_sc[g] = alpha * l_sc[g] + jnp.sum(pe, axis=-1, keepdims=True)
            acc_sc[g] = alpha * acc_sc[g] + _dot(pe.astype(BF16), vt)
            m_sc[g] = m_new

    def full_tile(jt, carry):
        tile(jt, False)
        return carry

    lax.fori_loop(0, ndiag, full_tile, 0)
    tile(ndiag, True)

    ws = pl.multiple_of(jnp.maximum(t * tq - WINDOW, 0), LANES)
    kw = kwin_ref[0, pl.ds(ws, wl), :]
    vw = vwin_ref[0, pl.ds(ws, wl), :]
    kposw = ws + lax.broadcasted_iota(jnp.int32, (1, wl), 1)
    dpos = pos4 - kposw
    mask_w = (dpos >= 0) & (dpos < WINDOW)

    sg = jax.nn.sigmoid(dtg_ref[0])
    outs = []
    for g in range(KV_HEADS):
        p_w = _masked_softmax(_dot_nt(qr_g[g], kw), mask_w)
        o_w = _dot(p_w.astype(BF16), vw)
        o_s = acc_sc[g] / l_sc[g]
        outs.append(_gate_cols(sg, g, 0, tq) * o_cmp[g] + _gate_cols(sg, g, 1, tq) * o_s
                    + _gate_cols(sg, g, 2, tq) * o_w)
    _emit_heads(outs, tq, y_ref)


def _nsa(qn, qr, dtg, kcvc, ov, kaug, vsl, kwin, vwin, *, tq, ns, tk, wl):
    b, lq = qn.shape[:2]
    nsp = ov.shape[1]
    nslab = nsp // LANES
    kern = functools.partial(_nsa_kernel, tq=tq, ns=ns, nsp=nsp, tk=tk, wl=wl)
    rows = lambda n: pl.BlockSpec((1, tq, n), lambda i, t: (i, t, 0))
    perb = lambda arr: pl.BlockSpec((1,) + arr.shape[1:], lambda i, t: (i, 0, 0), pipeline_mode=pl.Buffered(1))
    return pl.pallas_call(
        kern,
        grid=(b, lq // tq),
        in_specs=[rows(512), rows(512), rows(128), perb(kcvc),
                  pl.BlockSpec(ov.shape, lambda i, t: (0, 0)),
                  perb(kaug), perb(vsl), perb(kwin), perb(vwin)],
        out_specs=rows(512),
        out_shape=jax.ShapeDtypeStruct((b, lq, 512), F32),
        scratch_shapes=[pltpu.VMEM((nslab, KV_HEADS, 4 * tq, LANES), BF16),
                        pltpu.VMEM((KV_HEADS, 4 * tq, 1), F32),
                        pltpu.VMEM((KV_HEADS, 4 * tq, 1), F32),
                        pltpu.VMEM((KV_HEADS, 4 * tq, LANES), F32)],
        compiler_params=_cparams(("arbitrary", "arbitrary")),
    )(qn, qr, dtg, kcvc, ov, kaug, vsl, kwin, vwin)


SAMPLE_PAGES_PER_STEP = 16
SAMPLE_TQ = 8


def _nsa_sample_kernel(pt_ref, *refs, npages, past, ns, nsp):
    pps = SAMPLE_PAGES_PER_STEP
    tq = SAMPLE_TQ
    pages = refs[:pps]
    (tail_ref, qn_ref, qr_ref, dtg_ref, w1_ref, pe_ref, b1_ref, w2_ref, ov_ref, ex_ref, cwin_ref, wtail_ref,
     y_ref, tsc, xs, ps, s_all, vt_all, bias_sc) = refs[pps:]
    j = pl.program_id(1)
    ngroups = npages // pps
    hb = PAGE // CMP_STRIDE
    scale = HEAD_DIM ** -0.5
    qr_aug = jnp.concatenate([_group_rows(qr_ref[0] * scale, g, tq) for g in range(KV_HEADS)], axis=0).astype(BF16)

    for i, pg in enumerate(pages):
        for kv in range(2):
            tsc[i, kv] = pg[0, 0, kv * LANES:(kv + 1) * LANES, :].T
            for pos in range(CMP_STRIDE):
                xs[kv, i * hb:(i + 1) * hb, pos * LANES:(pos + 1) * LANES] = tsc[i, kv, pl.ds(pos, hb, stride=CMP_STRIDE), :]
        page_idx = j * pps + i
        s_all[page_idx] = _dot(qr_aug, pg[0, 0, 256:384, :].astype(BF16))
        vt_all[page_idx] = pg[0, 0, 384:512, :].astype(BF16)
    rows = pps * hb
    g0 = pl.multiple_of(j * rows, rows)
    for kv in range(2):
        x = xs[kv]
        for s in range(2):
            ps[kv, s, pl.ds(g0, rows), :] = _dot((x + pe_ref[kv, s:s + 1, :]).astype(BF16), w1_ref[kv, s])

    @pl.when(j == ngroups - 1)
    def _():
        nh = npages * hb
        kcvc = []
        for kv in range(2):
            hid = b1_ref[kv:kv + 1, :] + ps[kv, 0] + pltpu.roll(ps[kv, 1], nh - 1, 0)
            kcvc.append(_dot(jax.nn.gelu(hid).astype(BF16), w2_ref[kv]))
        kc, vc = kcvc
        qpos = past + lax.broadcasted_iota(jnp.int32, (tq, 1), 0)
        pos_r = jnp.concatenate([qpos] * (4 * KV_HEADS), axis=0)
        nrow = 4 * KV_HEADS * tq
        s_all[npages] = _dot(qr_aug, tail_ref[0, 256:384, :].astype(BF16))
        vt_all[npages] = tail_ref[0, 384:512, :].astype(BF16)

        qn_aug = jnp.concatenate([_group_rows(qn_ref[0] * scale, g, tq) for g in range(KV_HEADS)], axis=0)
        c_end = lax.broadcasted_iota(jnp.int32, (1, nh), 1) * CMP_STRIDE + (CMP_BLOCK - 1)
        p_c = _masked_softmax(_dot_nt_hi(qn_aug, kc), c_end <= pos_r)
        o_c = _dot(p_c.astype(BF16), vc.astype(BF16))
        slab_pages = SLAB_BLOCKS * SLC_BLOCK // PAGE
        for g in range(KV_HEADS):
            base = 4 * tq * g
            p4 = (p_c[base:base + tq] + p_c[base + tq:base + 2 * tq]
                  + p_c[base + 2 * tq:base + 3 * tq] + p_c[base + 3 * tq:base + 4 * tq])
            selb = _topk_bias(_importance(p4, ov_ref[...]), qpos, ns)
            for sl in range(nsp // LANES):
                npg = min(slab_pages, npages + 1 - slab_pages * sl)
                bias = _dot(selb[:, sl * LANES:(sl + 1) * LANES].astype(BF16), ex_ref[:, 0:npg * PAGE])
                for pp in range(npg):
                    bias_sc[slab_pages * sl + pp, g * tq:(g + 1) * tq, :] = bias[:, pp * PAGE:(pp + 1) * PAGE]

        def biased(pidx):
            b = bias_sc[pidx]
            b_r = jnp.concatenate([b[0:tq]] * 4 + [b[tq:2 * tq]] * 4, axis=0)
            return s_all[pidx] + b_r

        def p1(pidx, m):
            s = biased(pidx)
            s_all[pidx] = s
            return jnp.maximum(m, s)

        m = lax.fori_loop(0, npages, p1, jnp.full((nrow, LANES), 2.0 * NEG, F32))
        kpos_t = past + lax.broadcasted_iota(jnp.int32, (1, PAGE), 1)
        s_t = jnp.where(kpos_t <= pos_r, biased(npages), NEG)
        s_all[npages] = s_t
        mrow = jnp.max(jnp.maximum(m, s_t), axis=-1, keepdims=True)

        def p2(pidx, carry):
            l, acc = carry
            p = jnp.exp(s_all[pidx] - mrow)
            return l + p, acc + _dot_nt(p.astype(BF16), vt_all[pidx])

        l, acc = lax.fori_loop(0, npages + 1, p2, (jnp.zeros((nrow, LANES), F32), jnp.zeros((nrow, LANES), F32)),
                               unroll=3)
        o_s = acc / jnp.sum(l, axis=-1, keepdims=True)

        wbuf = cwin_ref.shape[3]
        kw_b = cwin_ref[0, 0, 0:128, :].astype(BF16)
        vw_b = cwin_ref[0, 0, 128:256, :].astype(BF16)
        kt_b = wtail_ref[0, 0:128, :].astype(BF16)
        vt_b = wtail_ref[0, 128:256, :].astype(BF16)
        s_w = jnp.concatenate([_dot(qr_aug, kw_b), _dot(qr_aug, kt_b)], axis=1)
        kposw = jnp.concatenate([past - wbuf + lax.broadcasted_iota(jnp.int32, (1, wbuf), 1), kpos_t], axis=1)
        dpos = pos_r - kposw
        p_w = _masked_softmax(s_w, (dpos >= 0) & (dpos < WINDOW) & (kposw >= 0))
        o_w = _dot_nt(p_w[:, 0:wbuf].astype(BF16), vw_b) + _dot_nt(p_w[:, wbuf:].astype(BF16), vt_b)

        sg = jax.nn.sigmoid(dtg_ref[0])
        outs = []
        for g in range(KV_HEADS):
            sl = slice(4 * tq * g, 4 * tq * (g + 1))
            outs.append(_gate_cols(sg, g, 0, tq) * o_c[sl] + _gate_cols(sg, g, 1, tq) * o_s[sl]
                        + _gate_cols(sg, g, 2, tq) * o_w[sl])
        _emit_heads(outs, tq, y_ref)


def _nsa_sample(layer, cache_t, table, tail_t, qn, qr, dtg, w1e, pee, b1e, w2e, ov, ex, cwin_t, wtail_t, *, past, ns):
    b, npages = table.shape
    pps = SAMPLE_PAGES_PER_STEP
    tq = SAMPLE_TQ
    nsp = ov.shape[1]
    nh = npages * (PAGE // CMP_STRIDE)
    nrow = 4 * KV_HEADS * tq
    kern = functools.partial(_nsa_sample_kernel, npages=npages, past=past, ns=ns, nsp=nsp)
    const = lambda arr: pl.BlockSpec(arr.shape, lambda i, j, pt: (0,) * arr.ndim, pipeline_mode=pl.Buffered(1))
    perb = lambda arr: pl.BlockSpec((1,) + arr.shape[1:], lambda i, j, pt: (i,) + (0,) * (arr.ndim - 1))

    def page_spec(k):
        return pl.BlockSpec((1, 1, 512, PAGE), lambda i, j, pt: (layer, pt[i, j * pps + k], 0, 0))

    return pl.pallas_call(
        kern,
        grid_spec=pltpu.PrefetchScalarGridSpec(
            num_scalar_prefetch=1,
            grid=(b, npages // pps),
            in_specs=[page_spec(k) for k in range(pps)] + [
                perb(tail_t), perb(qn), perb(qr), perb(dtg), const(w1e), const(pee), const(b1e), const(w2e),
                const(ov), const(ex),
                pl.BlockSpec((1, 1) + cwin_t.shape[2:], lambda i, j, pt: (layer, i, 0, 0)), perb(wtail_t)],
            out_specs=pl.BlockSpec((1, tq, 512), lambda i, j, pt: (i, 0, 0)),
            scratch_shapes=[pltpu.VMEM((pps, 2, PAGE, LANES), F32),
                            pltpu.VMEM((2, pps * 8, CMP_STRIDE * LANES), F32),
                            pltpu.VMEM((2, 2, nh, 256), F32),
                            pltpu.VMEM((npages + 1, nrow, LANES), F32),
                            pltpu.VMEM((npages + 1, LANES, PAGE), BF16),
                            pltpu.VMEM((npages + 1, KV_HEADS * tq, LANES), F32)]),
        out_shape=jax.ShapeDtypeStruct((b, tq, 512), F32),
        compiler_params=_cparams(("arbitrary", "arbitrary")),
    )(table, *([cache_t] * pps), tail_t, qn, qr, dtg, w1e, pee, b1e, w2e, ov, ex, cwin_t, wtail_t)


def _oproj_kernel(x_ref, ys_ref, yn_ref, w_ref, g_ref, r_ref, x1_ref, h_ref, lg_ref):
    a = jnp.concatenate([ys_ref[...], yn_ref[...]], axis=1).astype(BF16)
    x1 = x_ref[...] + _dot(a, w_ref[...])
    x1_ref[...] = x1
    h = _rms(x1, g_ref[...])
    h_ref[...] = h.astype(BF16)
    lg_ref[...] = _dot_hi(h, r_ref[...])


def _oproj(x, ys, yn, w, g, router, tm):
    t = x.shape[0]
    row = lambda n: pl.BlockSpec((tm, n), lambda i: (i, 0))
    full = lambda a: pl.BlockSpec(a.shape, lambda i: (0,) * a.ndim)
    return pl.pallas_call(
        _oproj_kernel,
        grid=(t // tm,),
        in_specs=[row(D_MODEL), row(512), row(512), full(w), full(g), full(router)],
        out_specs=[row(D_MODEL), row(D_MODEL), row(LANES)],
        out_shape=[jax.ShapeDtypeStruct((t, D_MODEL), F32), jax.ShapeDtypeStruct((t, D_MODEL), BF16),
                   jax.ShapeDtypeStruct((t, LANES), F32)],
        compiler_params=_cparams(("arbitrary",)),
    )(x, ys, yn, w, g, router)


def _ffn_kernel(h_ref, x1_ref, wg_ref, wu_ref, wd_ref, o_ref):
    f = pl.program_id(1)

    @pl.when(f == 0)
    def _():
        o_ref[...] = x1_ref[...]

    hb = h_ref[...]
    act = _silu(_dot(hb, wg_ref[...])) * _dot(hb, wu_ref[...])
    o_ref[...] += _dot(act.astype(BF16), wd_ref[...])


def _ffn(h, x1, wg, wu, wd, tm, tf):
    t = h.shape[0]
    dff = wg.shape[1]
    return pl.pallas_call(
        _ffn_kernel,
        grid=(t // tm, dff // tf),
        in_specs=[pl.BlockSpec((tm, D_MODEL), lambda i, f: (i, 0)),
                  pl.BlockSpec((tm, D_MODEL), lambda i, f: (i, 0)),
                  pl.BlockSpec((D_MODEL, tf), lambda i, f: (0, f)),
                  pl.BlockSpec((D_MODEL, tf), lambda i, f: (0, f)),
                  pl.BlockSpec((tf, D_MODEL), lambda i, f: (f, 0))],
        out_specs=pl.BlockSpec((tm, D_MODEL), lambda i, f: (i, 0)),
        out_shape=jax.ShapeDtypeStruct((t, D_MODEL), F32),
        compiler_params=_cparams(("arbitrary", "arbitrary")),
    )(h, x1, wg, wu, wd)


def _moe_kernel(be_ref, nb_ref, xb_ref, wg_ref, wu_ref, wd_ref, o_ref):
    b = pl.program_id(0)
    f = pl.program_id(1)

    @pl.when(f == 0)
    def _():
        o_ref[...] = jnp.zeros(o_ref.shape, F32)

    @pl.when(b < nb_ref[0])
    def _():
        hb = xb_ref[...]
        act = _silu(_dot(hb, wg_ref[0])) * _dot(hb, wu_ref[0])
        o_ref[...] += _dot(act.astype(BF16), wd_ref[0])


def _moe(blk_e, nblk_used, xb, wg, wu, wd, tmb, tf):
    n_blk = xb.shape[0] // tmb
    dex = wg.shape[2]
    nf = dex // tf

    def feff(b, f, nb):
        return jnp.where(b < nb[0], f, nf - 1)

    return pl.pallas_call(
        _moe_kernel,
        grid_spec=pltpu.PrefetchScalarGridSpec(
            num_scalar_prefetch=2,
            grid=(n_blk, nf),
            in_specs=[pl.BlockSpec((tmb, D_MODEL), lambda b, f, be, nb: (b, 0)),
                      pl.BlockSpec((1, D_MODEL, tf), lambda b, f, be, nb: (be[b], 0, feff(b, f, nb))),
                      pl.BlockSpec((1, D_MODEL, tf), lambda b, f, be, nb: (be[b], 0, feff(b, f, nb))),
                      pl.BlockSpec((1, tf, D_MODEL), lambda b, f, be, nb: (be[b], feff(b, f, nb), 0))],
            out_specs=pl.BlockSpec((tmb, D_MODEL), lambda b, f, be, nb: (b, 0))),
        out_shape=jax.ShapeDtypeStruct((n_blk * tmb, D_MODEL), F32),
        compiler_params=_cparams(("arbitrary", "arbitrary")),
    )(blk_e, nblk_used, xb, wg, wu, wd)


def _moe_layer(h, x1, logits, wg, wu, wd, tmb, tf):
    t = h.shape[0]
    top_v, top_e = lax.top_k(logits[:, :N_EXPERTS], TOP_K)
    gate = jax.nn.softmax(top_v, axis=-1)
    e_flat = top_e.reshape(-1)
    tok = jnp.repeat(jnp.arange(t, dtype=jnp.int32), TOP_K)
    order = jnp.argsort(e_flat)
    e_s = e_flat[order]
    tok_s = tok[order]
    counts = jnp.bincount(e_flat, length=N_EXPERTS)
    starts = jnp.cumsum(counts) - counts
    padded = (counts + tmb - 1) // tmb * tmb
    pends = jnp.cumsum(padded)
    pstarts = pends - padded
    dest_s = (pstarts[e_s] + jnp.arange(t * TOP_K, dtype=jnp.int32) - starts[e_s]).astype(jnp.int32)
    n_blk = -(-(t * TOP_K) // tmb) + N_EXPERTS
    buf_tok = jnp.full((n_blk * tmb,), t, dtype=jnp.int32).at[dest_s].set(tok_s)
    blk_e = jnp.minimum(jnp.searchsorted(pends, jnp.arange(n_blk, dtype=jnp.int32) * tmb, side='right'),
                        N_EXPERTS - 1).astype(jnp.int32)
    nblk_used = (pends[-1] // tmb).astype(jnp.int32).reshape(1)
    h_pad = jnp.concatenate([h, jnp.zeros((1, D_MODEL), h.dtype)], axis=0)
    xb = h_pad[buf_tok]
    out = _moe(blk_e, nblk_used, xb, wg, wu, wd, tmb, tf)
    dest = jnp.zeros((t * TOP_K,), jnp.int32).at[order].set(dest_s).reshape(t, TOP_K)
    return x1 + out[dest[:, 0]] * gate[:, 0:1] + out[dest[:, 1]] * gate[:, 1:2]


def _final_kernel(x_ref, g_ref, o_ref):
    o_ref[...] = _rms(x_ref[...], g_ref[...])


def _final_norm(x, g, tm):
    t = x.shape[0]
    return pl.pallas_call(
        _final_kernel,
        grid=(t // tm,),
        in_specs=[pl.BlockSpec((tm, D_MODEL), lambda i: (i, 0)), pl.BlockSpec(g.shape, lambda i: (0, 0))],
        out_specs=pl.BlockSpec((tm, D_MODEL), lambda i: (i, 0)),
        out_shape=jax.ShapeDtypeStruct((t, D_MODEL), F32),
        compiler_params=_cparams(("arbitrary",)),
    )(x, g)


def _rope_tables(pos):
    inv = 1.0 / (ROPE_THETA ** (jnp.arange(0, HEAD_DIM, 2, dtype=F32) / HEAD_DIM))
    ang = pos.astype(F32)[:, None] * inv[None, :]
    cos = jnp.cos(ang)
    sin = jnp.sin(ang)
    return jnp.concatenate([cos, cos, cos, cos], axis=1), jnp.concatenate([-sin, sin, -sin, sin], axis=1)


def _overlap(ncb, nsp, ns):
    c = jnp.arange(ncb, dtype=jnp.int32)[:, None] * CMP_STRIDE
    s = jnp.arange(nsp, dtype=jnp.int32)[None, :] * SLC_BLOCK
    ov = (c <= s + SLC_BLOCK - 1) & (c + CMP_BLOCK - 1 >= s) & (jnp.arange(nsp)[None, :] < ns)
    return ov.astype(BF16)


def _pad_lanes(v, n):
    return jnp.zeros((1, n), F32).at[0, :v.shape[0]].set(v.astype(F32))


def kernel(x_prompt, x_sample, cache_kv, cache_win, state_ssm, state_conv, page_table, ln_mix, w_in, conv_w,
           conv_b, dt_bias, a_log, d_skip, ssd_norm, cmp_pos, cmp_w1, cmp_b1, cmp_w2, w_out, ln_ffn,
           ffn_w_gate, ffn_w_up, ffn_w_down, moe_router, moe_w_gate, moe_w_up, moe_w_down, ln_final):
    depth = w_in.shape[0]
    bsz, seq = x_prompt.shape[:2]
    dbsz, dseq = x_sample.shape[:2]
    n_pages = page_table.shape[1]
    past = n_pages * cache_kv.shape[2]
    win_buf = cache_win.shape[2]
    npool = cache_kv.shape[1]
    assert bsz == 1 and cache_kv.shape[2] == PAGE and seq % PAGE == 0 and seq >= WINDOW + PAGE
    assert win_buf == WINDOW and past % PAGE == 0
    assert not any(past <= CMP_STRIDE * c + CMP_BLOCK - 1 <= past + dseq - 1
                   for c in range(past // CMP_STRIDE - 2, past // CMP_STRIDE + 2))

    ts = dbsz * dseq
    ns_p = seq // SLC_BLOCK
    ns_s = -(-(past + dseq) // SLC_BLOCK)
    nsp_p = -(-ns_p // LANES) * LANES
    nsp_s = -(-ns_s // LANES) * LANES
    assert dseq <= SAMPLE_TQ and n_pages % SAMPLE_PAGES_PER_STEP == 0
    slab_pages = SLAB_BLOCKS * SLC_BLOCK // PAGE
    assert (nsp_s // LANES - 1) * slab_pages < n_pages + 1 <= (nsp_s // LANES) * slab_pages
    cache_t = jnp.transpose(cache_kv, (0, 1, 3, 4, 5, 2)).reshape(depth, npool, 512, PAGE)
    cwin_t = jnp.transpose(cache_win, (0, 1, 3, 4, 5, 2)).reshape(depth, dbsz, 256, win_buf)
    ex = (jnp.arange(SLAB_BLOCKS * SLC_BLOCK, dtype=jnp.int32)[None, :] // SLC_BLOCK
          == jnp.arange(SLAB_BLOCKS, dtype=jnp.int32)[:, None]).astype(BF16)

    cs_p, sn_p = _rope_tables(jnp.arange(seq, dtype=jnp.int32))
    cs_s, sn_s = _rope_tables(past + jnp.arange(ts, dtype=jnp.int32) % dseq)
    ov_p = _overlap(seq // CMP_STRIDE, nsp_p, ns_p)
    ov_s = _overlap(past // CMP_STRIDE, nsp_s, ns_s)
    ident = jnp.arange(seq // PAGE, dtype=jnp.int32).reshape(1, -1)

    xp = x_prompt.reshape(seq, D_MODEL)
    xs = x_sample.reshape(ts, D_MODEL)
    outs = {k: [] for k in ('kv_p', 'kv_s', 'win_p', 'win_s', 'ssm_p', 'ssm_s', 'conv_p', 'conv_s')}

    for l in range(depth):
        w = w_in[l]
        wr = jnp.concatenate([w[:, :1536], w[:, 1544:2824], w[:, 1536:1544], w[:, 2824:2848],
                              jnp.zeros((D_MODEL, C_END - 2848), F32)], axis=1).astype(BF16)
        g_mix = ln_mix[l].reshape(1, -1)
        cw8 = jnp.zeros((8, CONV_DIM), F32).at[:CONV_W].set(conv_w[l])
        cb = conv_b[l].reshape(1, -1)
        dtb = _pad_lanes(dt_bias[l], LANES)
        a_neg = _pad_lanes(-jnp.exp(a_log[l].astype(F32)), LANES)
        dsk = jnp.repeat(d_skip[l].astype(F32), SSD_HEAD_DIM).reshape(1, -1)
        ng = ssd_norm[l].reshape(1, -1)
        w1 = cmp_w1[l].reshape(2, 2, CMP_STRIDE, HEAD_DIM, CMP_HIDDEN)
        eye = jnp.eye(KV_HEADS, dtype=F32)
        w1e = jnp.einsum('vspdh,gk->vspgdkh', w1, eye).reshape(2, 2, CMP_STRIDE * LANES, 256).astype(BF16)
        pe = cmp_pos[l].reshape(2, 2, CMP_STRIDE, 1, HEAD_DIM)
        pee = jnp.broadcast_to(pe, (2, 2, CMP_STRIDE, KV_HEADS, HEAD_DIM)).reshape(2, 2, CMP_STRIDE * LANES)
        b1e = jnp.concatenate([cmp_b1[l], cmp_b1[l]], axis=1)
        w2e = jnp.einsum('vhd,gk->vghkd', cmp_w2[l], eye).reshape(2, 256, LANES).astype(BF16)
        wo = w_out[l].astype(BF16)
        g_ffn = ln_ffn[l].reshape(1, -1)
        if l % 2 == 0:
            router = jnp.zeros((D_MODEL, LANES), F32)
        else:
            router = jnp.zeros((D_MODEL, LANES), F32).at[:, :N_EXPERTS].set(moe_router[l // 2])

        z, xbc, dtg, q, qr, kv, win = _front(xp, g_mix, wr, cs_p, sn_p, 256)
        y_ssd, hist, hfin = _ssd(z[None], xbc[None], dtg[None], jnp.zeros((1, 8, CONV_DIM), F32),
                                 jnp.zeros((1, D_SSD, SSD_STATE), F32), cw8, cb, dtb, a_neg, dsk, ng, SSD_CHUNK)
        kv_pages = kv.reshape(seq // PAGE, PAGE, 512)
        kcvc = _compress(kv_pages, ident, w1e, pee, b1e, w2e)
        kaug, vsl = _kvfmt(kv_pages, ident, jnp.zeros((1, PAGE, 512), F32), seq // PAGE)
        y_nsa = _nsa(q[None], qr[None], dtg[None], kcvc, ov_p, kaug, vsl,
                     win[None, :, 0:128].astype(BF16), win[None, :, 128:256].astype(BF16),
                     tq=128, ns=ns_p, tk=512, wl=WINDOW + 128)
        x1, h, logits = _oproj(xp, y_ssd[0], y_nsa[0], wo, g_ffn, router, 512)
        outs['kv_p'].append(kv.reshape(1, seq, 4, KV_HEADS, HEAD_DIM))
        outs['win_p'].append(win[seq - WINDOW:].reshape(1, WINDOW, 2, KV_HEADS, HEAD_DIM))
        outs['ssm_p'].append(hfin.reshape(1, SSD_HEADS, SSD_HEAD_DIM, SSD_STATE))
        outs['conv_p'].append(hist[:, 5:8])

        zs, xbcs, dtgs, qs, qrs, kvs, wins = _front(xs, g_mix, wr, cs_s, sn_s, ts)
        padl = lambda v: jnp.pad(v.reshape(dbsz, dseq, -1), ((0, 0), (0, SSD_CHUNK - dseq), (0, 0)))
        hist8 = jnp.pad(state_conv[l], ((0, 0), (8 - (CONV_W - 1), 0), (0, 0)))
        y_ssd_s, hist_s, hfin_s = _ssd(padl(zs), padl(xbcs), padl(dtgs), hist8,
                                       state_ssm[l].reshape(dbsz, D_SSD, SSD_STATE),
                                       cw8, cb, dtb, a_neg, dsk, ng, dseq)
        tail_t = jnp.pad(jnp.transpose(kvs.reshape(dbsz, dseq, 512), (0, 2, 1)), ((0, 0), (0, 0), (0, PAGE - dseq)))
        wtail_t = jnp.pad(jnp.transpose(wins.reshape(dbsz, dseq, 256), (0, 2, 1)), ((0, 0), (0, 0), (0, PAGE - dseq)))
        padq = lambda v: jnp.pad(v.reshape(dbsz, dseq, -1), ((0, 0), (0, SAMPLE_TQ - dseq), (0, 0)))
        y_nsa_s = _nsa_sample(l, cache_t, page_table, tail_t, padq(qs), padq(qrs), padq(dtgs), w1e, pee, b1e, w2e,
                              ov_s, ex, cwin_t, wtail_t, past=past, ns=ns_s)
        y_nsa_s = y_nsa_s[:, :dseq].reshape(ts, D_NSA)
        x1s, hs, logits_s = _oproj(xs, y_ssd_s[:, :dseq].reshape(ts, D_SSD), y_nsa_s, wo, g_ffn, router, ts)
        outs['kv_s'].append(kvs.reshape(dbsz, dseq, 4, KV_HEADS, HEAD_DIM))
        win_all = jnp.concatenate([cache_win[l], wins.reshape(dbsz, dseq, 2, KV_HEADS, HEAD_DIM)], axis=1)
        outs['win_s'].append(win_all[:, dseq:])
        outs['ssm_s'].append(hfin_s.reshape(dbsz, SSD_HEADS, SSD_HEAD_DIM, SSD_STATE))
        outs['conv_s'].append(hist_s[:, 5:8])

        if l % 2 == 0:
            wg = ffn_w_gate[l // 2].astype(BF16)
            wu = ffn_w_up[l // 2].astype(BF16)
            wd = ffn_w_down[l // 2].astype(BF16)
            xp = _ffn(h, x1, wg, wu, wd, 1024, 256)
            xs = _ffn(hs, x1s, wg, wu, wd, ts, 256)
        else:
            wg = moe_w_gate[l // 2].astype(BF16)
            wu = moe_w_up[l // 2].astype(BF16)
            wd = moe_w_down[l // 2].astype(BF16)
            xp = _moe_layer(h, x1, logits, wg, wu, wd, 1024, 512)
            xs = _moe_layer(hs, x1s, logits_s, wg, wu, wd, 128, 512)

    g_fin = ln_final.reshape(1, -1)
    y_prompt = _final_norm(xp, g_fin, 1024).reshape(bsz, seq, D_MODEL)
    y_sample = _final_norm(xs, g_fin, ts).reshape(dbsz, dseq, D_MODEL)
    st = lambda k: jnp.stack(outs[k])
    return (y_prompt, y_sample, st('kv_p'), st('kv_s'), st('win_p'), st('win_s'),
            st('ssm_p'), st('ssm_s'), st('conv_p'), st('conv_s'))
```

```python
import functools
import math

import jax
import jax.numpy as jnp
from jax import lax
from jax.experimental import pallas as pl
from jax.experimental.pallas import tpu as pltpu

F32 = jnp.float32
BF16 = jnp.bfloat16

D_MODEL = 1024
SSD_HEADS = 8
SSD_HEAD_DIM = 64
D_SSD = 512
SSD_GROUPS = 2
SSD_STATE = 128
CONV_W = 4
CONV_DIM = 1024
SSD_CHUNK = 128
NSA_HEADS = 8
HEAD_DIM = 64
KV_HEADS = 2
D_NSA = 512
CMP_BLOCK = 32
CMP_STRIDE = 16
CMP_HIDDEN = 128
SLC_BLOCK = 64
TOP_N = 16
WINDOW = 512
ROPE_THETA = 10000.0
FORCE_BONUS = 1.0e4
N_EXPERTS = 8
TOP_K = 2
EPS = 1e-6

PAGE = 128
LANES = 128
LOG2E = 1.4426950408889634
NEG = -1.0e30
SLAB_BLOCKS = 128
VMEM_LIMIT = 56 * 1024 * 1024

C_Z, C_XBC, C_Q, C_KV, C_DT, C_END = 0, 512, 1536, 2048, 2816, 2944


def _cparams(sem):
    return pltpu.CompilerParams(dimension_semantics=sem, vmem_limit_bytes=VMEM_LIMIT)


def _dot(a, b):
    return jnp.dot(a, b, preferred_element_type=F32)


def _dot_nt(a, b):
    return lax.dot_general(a, b, (((1,), (1,)), ((), ())), preferred_element_type=F32)


def _split3(x):
    h1 = x.astype(BF16)
    r1 = x - h1.astype(F32)
    h2 = r1.astype(BF16)
    h3 = (r1 - h2.astype(F32)).astype(BF16)
    return h1, h2, h3


def _dot_nt_hi(a, b):
    a1, a2, _ = _split3(a)
    b1, b2, _ = _split3(b)
    return _dot_nt(a1, b1) + (_dot_nt(a1, b2) + _dot_nt(a2, b1))


def _dot_hi(a, b):
    a1, a2, _ = _split3(a)
    b1, b2, _ = _split3(b)
    return _dot(a1, b1) + (_dot(a1, b2) + _dot(a2, b1))


def _silu(x):
    return x * jax.nn.sigmoid(x)


def _rms(x, g):
    return x * lax.rsqrt(jnp.mean(x * x, axis=-1, keepdims=True) + EPS) * g


def _front_kernel(x_ref, g_ref, w_ref, cs_ref, sn_ref,
                  z_ref, xbc_ref, dtg_ref, q_ref, qr_ref, kv_ref, win_ref):
    xn = _rms(x_ref[...], g_ref[...])
    p = _dot(xn.astype(BF16), w_ref[...])
    cs = cs_ref[...]
    sn = sn_ref[...]
    lane = lax.broadcasted_iota(jnp.int32, cs.shape, 1)
    first = (lane % HEAD_DIM) < (HEAD_DIM // 2)

    def rope(v):
        sw = jnp.where(first, pltpu.roll(v, LANES - HEAD_DIM // 2, 1), pltpu.roll(v, HEAD_DIM // 2, 1))
        return v * cs + sw * sn

    z_ref[...] = p[:, C_Z:C_XBC]
    xbc_ref[...] = p[:, C_XBC:C_Q]
    dtg_ref[...] = p[:, C_DT:C_END]
    q_ref[...] = p[:, C_Q:C_KV]
    for s in range(4):
        qr_ref[:, s * LANES:(s + 1) * LANES] = rope(p[:, C_Q + s * LANES:C_Q + (s + 1) * LANES])
    kv_ref[:, 0:256] = p[:, C_KV:C_KV + 256]
    kv_ref[:, 256:384] = rope(p[:, C_KV + 256:C_KV + 384])
    kv_ref[:, 384:512] = p[:, C_KV + 384:C_KV + 512]
    win_ref[:, 0:128] = rope(p[:, C_KV + 512:C_KV + 640])
    win_ref[:, 128:256] = p[:, C_KV + 640:C_KV + 768]


def _front(x, g, w, cs, sn, tm):
    t = x.shape[0]
    row = lambda n: pl.BlockSpec((tm, n), lambda i: (i, 0))
    full = lambda a: pl.BlockSpec(a.shape, lambda i: (0,) * a.ndim)
    outs = [512, 1024, 128, 512, 512, 512, 256]
    return pl.pallas_call(
        _front_kernel,
        grid=(t // tm,),
        in_specs=[row(D_MODEL), full(g), full(w), row(LANES), row(LANES)],
        out_specs=[row(n) for n in outs],
        out_shape=[jax.ShapeDtypeStruct((t, n), F32) for n in outs],
        compiler_params=_cparams(("arbitrary",)),
    )(x, g, w, cs, sn)


def _expand_heads(v):
    r = v.shape[0]
    lane = lax.broadcasted_iota(jnp.int32, (r, LANES), 1)
    outs = []
    for k in range(4):
        a = jnp.broadcast_to(v[:, 2 * k:2 * k + 1], (r, LANES))
        b = jnp.broadcast_to(v[:, 2 * k + 1:2 * k + 2], (r, LANES))
        outs.append(jnp.where(lane < SSD_HEAD_DIM, a, b))
    return jnp.concatenate(outs, axis=1)


def _ssd_kernel(z_ref, xbc_ref, dtg_ref, hist_ref, h0_ref, cw_ref, cb_ref, dtb_ref, a_ref, dsk_ref, ng_ref,
                y_ref, hist_o_ref, hfin_ref, xpad, ht, *, nchunks, valid_last):
    c = pl.program_id(1)
    q = SSD_CHUNK

    @pl.when(c == 0)
    def _():
        xpad[0:8, :] = hist_ref[0]
        ht[...] = h0_ref[0].T

    xpad[8:8 + q, :] = xbc_ref[0]
    conv = cb_ref[...]
    for k in range(CONV_W):
        conv = conv + xpad[5 + k:5 + k + q, :] * cw_ref[k:k + 1, :]
    xc = _silu(conv)
    xs = xc[:, 0:D_SSD]
    bm = xc[:, D_SSD:D_SSD + 256]
    cm = xc[:, D_SSD + 256:D_SSD + 512]

    row = lax.broadcasted_iota(jnp.int32, (q, q), 0)
    col = lax.broadcasted_iota(jnp.int32, (q, q), 1)
    nvalid = jnp.where(c == nchunks - 1, valid_last, q)
    xdt_in = dtg_ref[0] + dtb_ref[...]
    dt = jnp.maximum(xdt_in, 0.0) + jnp.log1p(jnp.exp(-jnp.abs(xdt_in)))
    dt = jnp.where(row < nvalid, dt, 0.0)
    dta = dt * a_ref[...]
    causal = row >= col
    cum = jnp.dot(causal.astype(F32), dta, preferred_element_type=F32, precision=lax.Precision.HIGHEST)
    cum_t = cum.T
    ecum = jnp.exp(cum)
    toend = jnp.exp(cum[q - 1:q, :] - cum)
    dt_e = _expand_heads(dt)
    ecum_e = _expand_heads(ecum)
    toend_e = _expand_heads(toend)
    xdt = xs * dt_e
    xdt_b = xdt.astype(BF16)
    xw_b = (xdt * toend_e).astype(BF16)
    lane = lax.broadcasted_iota(jnp.int32, (q, LANES), 1)

    y_intra = []
    y_inter = []
    for g in range(SSD_GROUPS):
        bg = bm[:, g * 128:(g + 1) * 128]
        cg_b = cm[:, g * 128:(g + 1) * 128].astype(BF16)
        bg_b = bg.astype(BF16)
        cb = _dot_nt(cg_b, bg_b)
        htg = ht[:, g * 256:(g + 1) * 256]
        y_inter.append(_dot(cg_b, htg.astype(BF16)))
        for kk in range(2):
            slab = xdt_b[:, (2 * g + kk) * LANES:(2 * g + kk + 1) * LANES]
            res = []
            for hh in range(2):
                h = 4 * g + 2 * kk + hh
                seg = cum[:, h:h + 1] - cum_t[h:h + 1, :]
                decay = jnp.exp(jnp.where(causal, seg, NEG))
                res.append(_dot((decay * cb).astype(BF16), slab))
            y_intra.append(jnp.where(lane < SSD_HEAD_DIM, res[0], res[1]))
        s_new = _dot(bg.T.astype(BF16), xw_b[:, g * 256:(g + 1) * 256])
        ht[:, g * 256:(g + 1) * 256] = htg * ecum_e[q - 1:q, g * 256:(g + 1) * 256] + s_new

    y = jnp.concatenate(y_intra, axis=1) + jnp.concatenate(y_inter, axis=1) * ecum_e + dsk_ref[...] * xs
    y = y * _silu(z_ref[0])
    y_ref[0] = _rms(y, ng_ref[...])

    xpad[0:8, :] = xpad[q:q + 8, :]

    @pl.when(c == nchunks - 1)
    def _():
        hist_o_ref[0] = xpad[valid_last:valid_last + 8, :]
        hfin_ref[0] = ht[...].T


def _ssd(z, xbc, dtg, hist8, h0, cw8, cb, dtb, a, dsk, ng, valid_last):
    b, l = z.shape[:2]
    nchunks = l // SSD_CHUNK
    rows = lambda n: pl.BlockSpec((1, SSD_CHUNK, n), lambda i, c: (i, c, 0))
    perb = lambda r, n: pl.BlockSpec((1, r, n), lambda i, c: (i, 0, 0))
    full = lambda arr: pl.BlockSpec(arr.shape, lambda i, c: (0,) * arr.ndim)
    kern = functools.partial(_ssd_kernel, nchunks=nchunks, valid_last=valid_last)
    return pl.pallas_call(
        kern,
        grid=(b, nchunks),
        in_specs=[rows(512), rows(1024), rows(128), perb(8, 1024), perb(512, 128),
                  full(cw8), full(cb), full(dtb), full(a), full(dsk), full(ng)],
        out_specs=[rows(512), perb(8, 1024), perb(512, 128)],
        out_shape=[jax.ShapeDtypeStruct((b, l, 512), F32),
                   jax.ShapeDtypeStruct((b, 8, 1024), F32),
                   jax.ShapeDtypeStruct((b, 512, 128), F32)],
        scratch_shapes=[pltpu.VMEM((SSD_CHUNK + 8, CONV_DIM), F32), pltpu.VMEM((SSD_STATE, D_SSD), F32)],
        compiler_params=_cparams(("arbitrary", "arbitrary")),
    )(z, xbc, dtg, hist8, h0, cw8, cb, dtb, a, dsk, ng)


NSA_TK = 512
CMP_GROUP_PAGES = 16


def _compress_kernel(pt_ref, pgk_ref, pgv_ref, w1_ref, pe_ref, b1_ref, w2_ref, out_ref, xs, ps, *, npages):
    p = pl.program_id(1)
    sub = p % CMP_GROUP_PAGES
    hb = PAGE // CMP_STRIDE
    r0 = pl.multiple_of(sub * hb, hb)
    for kv, pg_ref in enumerate((pgk_ref, pgv_ref)):
        for pos in range(CMP_STRIDE):
            xs[kv, pl.ds(r0, hb), pos * LANES:(pos + 1) * LANES] = pg_ref[0, pl.ds(pos, hb, stride=CMP_STRIDE), :]

    @pl.when(sub == CMP_GROUP_PAGES - 1)
    def _():
        rows = CMP_GROUP_PAGES * hb
        g0 = pl.multiple_of((p // CMP_GROUP_PAGES) * rows, rows)
        for kv in range(2):
            x = xs[kv]
            for s in range(2):
                ps[kv, s, pl.ds(g0, rows), :] = _dot((x + pe_ref[kv, s:s + 1, :]).astype(BF16), w1_ref[kv, s])

    @pl.when(p == npages - 1)
    def _():
        nh = npages * hb
        for kv in range(2):
            hid = b1_ref[kv:kv + 1, :] + ps[kv, 0] + pltpu.roll(ps[kv, 1], nh - 1, 0)
            out_ref[0, :, kv * LANES:(kv + 1) * LANES] = _dot(jax.nn.gelu(hid).astype(BF16), w2_ref[kv])


def _compress(pages, table, w1e, pee, b1e, w2e):
    b, npages = table.shape
    nh = npages * (PAGE // CMP_STRIDE)
    kern = functools.partial(_compress_kernel, npages=npages)
    full = lambda arr: pl.BlockSpec(arr.shape, lambda i, p, pt: (0,) * arr.ndim)
    return pl.pallas_call(
        kern,
        grid_spec=pltpu.PrefetchScalarGridSpec(
            num_scalar_prefetch=1,
            grid=(b, npages),
            in_specs=[pl.BlockSpec((1, PAGE, LANES), lambda i, p, pt: (pt[i, p], 0, 0)),
                      pl.BlockSpec((1, PAGE, LANES), lambda i, p, pt: (pt[i, p], 0, 1)),
                      full(w1e), full(pee), full(b1e), full(w2e)],
            out_specs=pl.BlockSpec((1, nh, 256), lambda i, p, pt: (i, 0, 0)),
            scratch_shapes=[pltpu.VMEM((2, CMP_GROUP_PAGES * 8, CMP_STRIDE * LANES), F32),
                            pltpu.VMEM((2, 2, nh, 256), F32)]),
        out_shape=jax.ShapeDtypeStruct((b, nh, 256), F32),
        compiler_params=_cparams(("arbitrary", "arbitrary")),
    )(table, pages, pages, w1e, pee, b1e, w2e)


def _kvfmt_kernel(kv_ref, k_ref, vt_ref):
    p = pl.program_id(0)
    src = kv_ref[...]
    r = lax.broadcasted_iota(jnp.int32, (PAGE, LANES), 0)
    lane = lax.broadcasted_iota(jnp.int32, (PAGE, LANES), 1)
    blk = (2 * p + r // SLC_BLOCK) % SLAB_BLOCKS
    k_ref[:, 0:128] = src[:, 0:128].astype(BF16)
    k_ref[:, 128:256] = jnp.where(lane == blk, 1.0, 0.0).astype(BF16)
    vt_ref[0] = src[:, 128:256].T.astype(BF16)


def _kvfmt(kv, tk):
    l = kv.shape[0]
    per = tk // PAGE
    return pl.pallas_call(
        _kvfmt_kernel,
        grid=(l // PAGE,),
        in_specs=[pl.BlockSpec((PAGE, 256), lambda p: (p, 1))],
        out_specs=[pl.BlockSpec((PAGE, 256), lambda p: (p, 0)),
                   pl.BlockSpec((1, LANES, PAGE), lambda p: (p // per, 0, p % per))],
        out_shape=[jax.ShapeDtypeStruct((l, 256), BF16),
                   jax.ShapeDtypeStruct((l // tk, LANES, tk), BF16)],
        compiler_params=_cparams(("arbitrary",)),
    )(kv)


def _group_rows(qfull, g, tq):
    lane = lax.broadcasted_iota(jnp.int32, (tq, LANES), 1)
    keep = (lane >= HEAD_DIM) if g else (lane < HEAD_DIM)
    pieces = []
    for r in range(4):
        h = 4 * g + r
        slab = qfull[:, (h // 2) * LANES:(h // 2 + 1) * LANES]
        if h % 2 != g:
            slab = pltpu.roll(slab, HEAD_DIM, 1)
        pieces.append(jnp.where(keep, slab, 0.0))
    return jnp.concatenate(pieces, axis=0)


def _masked_softmax(s, mask):
    s = jnp.where(mask, s, NEG)
    m = jnp.max(s, axis=-1, keepdims=True)
    e = jnp.where(mask, jnp.exp(s - m), 0.0)
    return e / jnp.maximum(jnp.sum(e, axis=-1, keepdims=True), 1e-30)


def _importance(p4, ov):
    h1, h2, h3 = _split3(p4)
    return _dot(h1, ov) + (_dot(h2, ov) + _dot(h3, ov))


def _topk_bias(imp, qpos, ns):
    tq, nsp = imp.shape
    jf = lax.broadcasted_iota(jnp.int32, (tq, nsp), 1)
    jff = jf.astype(F32)
    cur = qpos >> 6
    valid = jf <= cur
    forced = valid & ((jf == 0) | (jf == cur) | (jf == cur - 1))
    score = jnp.where(forced, FORCE_BONUS, jnp.where(valid, imp, -1.0))
    score = jnp.where(jf < ns, score, -2.0)

    def pick(_, carry):
        sc, sb = carry
        mx = jnp.max(sc, axis=-1, keepdims=True)
        jm = jnp.min(jnp.where(sc == mx, jff, float(nsp)), axis=-1, keepdims=True)
        hit = jff == jm
        return jnp.where(hit, -3.0, sc), jnp.where(hit, 0.0, sb)

    return lax.fori_loop(0, TOP_N, pick, (score, jnp.full((tq, nsp), NEG, F32)))[1]


def _gate_cols(sg, g, k, tq):
    return jnp.concatenate([sg[:, 8 + 3 * (4 * g + r) + k:8 + 3 * (4 * g + r) + k + 1] for r in range(4)], axis=0)


def _emit_heads(outs, tq, y_ref):
    lane = lax.broadcasted_iota(jnp.int32, (tq, LANES), 1)
    for k in range(4):
        g = k // 2
        r0 = (2 * k) % 4
        a = outs[g][r0 * tq:(r0 + 1) * tq]
        b = outs[g][(r0 + 1) * tq:(r0 + 2) * tq]
        if g == 0:
            slab = jnp.where(lane < HEAD_DIM, a, pltpu.roll(b, HEAD_DIM, 1))
        else:
            slab = jnp.where(lane < HEAD_DIM, pltpu.roll(a, HEAD_DIM, 1), b)
        y_ref[0, :, k * LANES:(k + 1) * LANES] = slab


def _nsa_kernel(qn_ref, qr_ref, dtg_ref, kcvc_ref, ov_ref, kaug_ref, vt_ref, kwin_ref, vwin_ref,
                y_ref, maskb, m_sc, l_sc, acc_sc, *, tq, ns, nsp, tk, wl):
    t = pl.program_id(1)
    scale = HEAD_DIM ** -0.5
    qpos = t * tq + lax.broadcasted_iota(jnp.int32, (tq, 1), 0)
    pos4 = jnp.concatenate([qpos] * 4, axis=0)
    nslab = nsp // LANES

    qn = qn_ref[0] * scale
    qr = qr_ref[0] * scale
    kc = kcvc_ref[0, :, 0:128]
    vc_b = kcvc_ref[0, :, 128:256].astype(BF16)
    ncb = kc.shape[0]
    c_end = lax.broadcasted_iota(jnp.int32, (1, ncb), 1) * CMP_STRIDE + (CMP_BLOCK - 1)
    mask_c = c_end <= pos4

    qr_g = []
    qs_g = []
    o_cmp = []
    for g in range(KV_HEADS):
        qn_g = _group_rows(qn, g, tq)
        qr_f = _group_rows(qr, g, tq)
        qr_g.append(qr_f.astype(BF16))
        qs_g.append((qr_f * LOG2E).astype(BF16))
        p_c = _masked_softmax(_dot_nt_hi(qn_g, kc), mask_c)
        o_cmp.append(_dot(p_c.astype(BF16), vc_b))
        p4 = p_c[0:tq] + p_c[tq:2 * tq] + p_c[2 * tq:3 * tq] + p_c[3 * tq:4 * tq]
        selb = _topk_bias(_importance(p4, ov_ref[...]), qpos, ns)
        for sl in range(nslab):
            piece = selb[:, sl * LANES:(sl + 1) * LANES]
            maskb[sl, g] = jnp.concatenate([piece] * 4, axis=0).astype(BF16)

    m_sc[...] = jnp.full(m_sc.shape, 2.0 * NEG, F32)
    l_sc[...] = jnp.zeros(l_sc.shape, F32)
    acc_sc[...] = jnp.zeros(acc_sc.shape, F32)
    ndiag = (t * tq + tq - 1) // tk
    posr = t * tq + (lax.broadcasted_iota(jnp.int32, (1, 4 * tq), 1) & (tq - 1))

    def tile(jt, causal):
        k0 = pl.multiple_of(jt * tk, tk)
        kt = kaug_ref[0, pl.ds(k0, tk), :]
        vt = vt_ref[jt]
        sb = k0 // (SLAB_BLOCKS * SLC_BLOCK)
        for g in range(KV_HEADS):
            qa = jnp.concatenate([qs_g[g], maskb[sb, g]], axis=1)
            s = _dot_nt(kt, qa)
            if causal:
                kpos = k0 + lax.broadcasted_iota(jnp.int32, (tk, 1), 0)
                s = jnp.where(kpos <= posr, s, NEG)
            m_old = m_sc[g]
            m_new = jnp.maximum(m_old, jnp.max(s, axis=0, keepdims=True))
            alpha = jnp.exp2(m_old - m_new)
            pe = jnp.exp2(s - m_new)
            l_sc[g] = alpha * l_sc[g] + jnp.sum(pe, axis=0, keepdims=True)
            acc_sc[g] = alpha * acc_sc[g] + _dot(vt, pe.astype(BF16))
            m_sc[g] = m_new

    def full_tile(jt, carry):
        tile(jt, False)
        return carry

    lax.fori_loop(0, ndiag, full_tile, 0)
    tile(ndiag, True)

    ws = pl.multiple_of(jnp.maximum(t * tq - WINDOW, 0), LANES)
    kw = kwin_ref[0, pl.ds(ws, wl), :]
    vw = vwin_ref[0, pl.ds(ws, wl), :]
    kposw = ws + lax.broadcasted_iota(jnp.int32, (1, wl), 1)
    dpos = pos4 - kposw
    mask_w = (dpos >= 0) & (dpos < WINDOW)

    sg = jax.nn.sigmoid(dtg_ref[0])
    outs = []
    for g in range(KV_HEADS):
        p_w = _masked_softmax(_dot_nt(qr_g[g], kw), mask_w)
        o_w = _dot(p_w.astype(BF16), vw)
        o_s = (acc_sc[g] / l_sc[g]).T
        outs.append(_gate_cols(sg, g, 0, tq) * o_cmp[g] + _gate_cols(sg, g, 1, tq) * o_s
                    + _gate_cols(sg, g, 2, tq) * o_w)
    _emit_heads(outs, tq, y_ref)


def _nsa(qn, qr, dtg, kcvc, ov, kaug, vt, kwin, vwin, *, tq, ns, tk, wl):
    b, lq = qn.shape[:2]
    nsp = ov.shape[1]
    nslab = nsp // LANES
    kern = functools.partial(_nsa_kernel, tq=tq, ns=ns, nsp=nsp, tk=tk, wl=wl)
    rows = lambda n: pl.BlockSpec((1, tq, n), lambda i, t: (i, t, 0))
    perb = lambda arr: pl.BlockSpec((1,) + arr.shape[1:], lambda i, t: (i, 0, 0), pipeline_mode=pl.Buffered(1))
    return pl.pallas_call(
        kern,
        grid=(b, lq // tq),
        in_specs=[rows(512), rows(512), rows(128), perb(kcvc),
                  pl.BlockSpec(ov.shape, lambda i, t: (0, 0)),
                  perb(kaug), pl.BlockSpec(vt.shape, lambda i, t: (0, 0, 0), pipeline_mode=pl.Buffered(1)),
                  perb(kwin), perb(vwin)],
        out_specs=rows(512),
        out_shape=jax.ShapeDtypeStruct((b, lq, 512), F32),
        scratch_shapes=[pltpu.VMEM((nslab, KV_HEADS, 4 * tq, LANES), BF16),
                        pltpu.VMEM((KV_HEADS, 1, 4 * tq), F32),
                        pltpu.VMEM((KV_HEADS, 1, 4 * tq), F32),
                        pltpu.VMEM((KV_HEADS, LANES, 4 * tq), F32)],
        compiler_params=_cparams(("arbitrary", "arbitrary")),
    )(qn, qr, dtg, kcvc, ov, kaug, vt, kwin, vwin)


SAMPLE_PAGES_PER_STEP = 16
SAMPLE_TQ = 8


def _nsa_sample_kernel(pt_ref, *refs, npages, past, ns, nsp):
    pps = SAMPLE_PAGES_PER_STEP
    tq = SAMPLE_TQ
    pages = refs[:pps]
    (tail_ref, qn_ref, qr_ref, dtg_ref, w1_ref, pe_ref, b1_ref, w2_ref, ov_ref, ex_ref, cwin_ref, wtail_ref,
     y_ref, tsc, xs, ps, s_all, vt_all, bias_sc) = refs[pps:]
    j = pl.program_id(1)
    ngroups = npages // pps
    hb = PAGE // CMP_STRIDE
    scale = HEAD_DIM ** -0.5
    qr_aug = jnp.concatenate([_group_rows(qr_ref[0] * scale, g, tq) for g in range(KV_HEADS)], axis=0).astype(BF16)

    for i, pg in enumerate(pages):
        for kv in range(2):
            tsc[i, kv] = pg[0, 0, kv * LANES:(kv + 1) * LANES, :].T
            for pos in range(CMP_STRIDE):
                xs[kv, i * hb:(i + 1) * hb, pos * LANES:(pos + 1) * LANES] = tsc[i, kv, pl.ds(pos, hb, stride=CMP_STRIDE), :]
        page_idx = j * pps + i
        s_all[page_idx] = _dot(qr_aug, pg[0, 0, 256:384, :].astype(BF16))
        vt_all[page_idx] = pg[0, 0, 384:512, :].astype(BF16)
    rows = pps * hb
    g0 = pl.multiple_of(j * rows, rows)
    for kv in range(2):
        x = xs[kv]
        for s in range(2):
            ps[kv, s, pl.ds(g0, rows), :] = _dot((x + pe_ref[kv, s:s + 1, :]).astype(BF16), w1_ref[kv, s])

    @pl.when(j == ngroups - 1)
    def _():
        nh = npages * hb
        kcvc = []
        for kv in range(2):
            hid = b1_ref[kv:kv + 1, :] + ps[kv, 0] + pltpu.roll(ps[kv, 1], nh - 1, 0)
            kcvc.append(_dot(jax.nn.gelu(hid).astype(BF16), w2_ref[kv]))
        kc, vc = kcvc
        qpos = past + lax.broadcasted_iota(jnp.int32, (tq, 1), 0)
        pos_r = jnp.concatenate([qpos] * (4 * KV_HEADS), axis=0)
        nrow = 4 * KV_HEADS * tq
        s_all[npages] = _dot(qr_aug, tail_ref[0, 256:384, :].astype(BF16))
        vt_all[npages] = tail_ref[0, 384:512, :].astype(BF16)

        qn_aug = jnp.concatenate([_group_rows(qn_ref[0] * scale, g, tq) for g in range(KV_HEADS)], axis=0)
        c_end = lax.broadcasted_iota(jnp.int32, (1, nh), 1) * CMP_STRIDE + (CMP_BLOCK - 1)
        p_c = _masked_softmax(_dot_nt_hi(qn_aug, kc), c_end <= pos_r)
        o_c = _dot(p_c.astype(BF16), vc.astype(BF16))
        slab_pages = SLAB_BLOCKS * SLC_BLOCK // PAGE
        for g in range(KV_HEADS):
            base = 4 * tq * g
            p4 = (p_c[base:base + tq] + p_c[base + tq:base + 2 * tq]
                  + p_c[base + 2 * tq:base + 3 * tq] + p_c[base + 3 * tq:base + 4 * tq])
            selb = _topk_bias(_importance(p4, ov_ref[...]), qpos, ns)
            for sl in range(nsp // LANES):
                npg = min(slab_pages, npages + 1 - slab_pages * sl)
                bias = _dot(selb[:, sl * LANES:(sl + 1) * LANES].astype(BF16), ex_ref[:, 0:npg * PAGE])
                for pp in range(npg):
                    bias_sc[slab_pages * sl + pp, g * tq:(g + 1) * tq, :] = bias[:, pp * PAGE:(pp + 1) * PAGE]

        def biased(pidx):
            b = bias_sc[pidx]
            b_r = jnp.concatenate([b[0:tq]] * 4 + [b[tq:2 * tq]] * 4, axis=0)
            return s_all[pidx] + b_r

        def p1(pidx, m):
            s = biased(pidx)
            s_all[pidx] = s
            return jnp.maximum(m, s)

        m = lax.fori_loop(0, npages, p1, jnp.full((nrow, LANES), 2.0 * NEG, F32))
        kpos_t = past + lax.broadcasted_iota(jnp.int32, (1, PAGE), 1)
        s_t = jnp.where(kpos_t <= pos_r, biased(npages), NEG)
        s_all[npages] = s_t
        mrow = jnp.max(jnp.maximum(m, s_t), axis=-1, keepdims=True)

        def p2(pidx, carry):
            l, acc = carry
            p = jnp.exp(s_all[pidx] - mrow)
            return l + p, acc + _dot_nt(p.astype(BF16), vt_all[pidx])

        l, acc = lax.fori_loop(0, npages + 1, p2, (jnp.zeros((nrow, LANES), F32), jnp.zeros((nrow, LANES), F32)),
                               unroll=3)
        o_s = acc / jnp.sum(l, axis=-1, keepdims=True)

        wbuf = cwin_ref.shape[3]
        kw_b = cwin_ref[0, 0, 0:128, :].astype(BF16)
        vw_b = cwin_ref[0, 0, 128:256, :].astype(BF16)
        kt_b = wtail_ref[0, 0:128, :].astype(BF16)
        vt_b = wtail_ref[0, 128:256, :].astype(BF16)
        s_w = jnp.concatenate([_dot(qr_aug, kw_b), _dot(qr_aug, kt_b)], axis=1)
        kposw = jnp.concatenate([past - wbuf + lax.broadcasted_iota(jnp.int32, (1, wbuf), 1), kpos_t], axis=1)
        dpos = pos_r - kposw
        p_w = _masked_softmax(s_w, (dpos >= 0) & (dpos < WINDOW) & (kposw >= 0))
        o_w = _dot_nt(p_w[:, 0:wbuf].astype(BF16), vw_b) + _dot_nt(p_w[:, wbuf:].astype(BF16), vt_b)

        sg = jax.nn.sigmoid(dtg_ref[0])
        outs = []
        for g in range(KV_HEADS):
            sl = slice(4 * tq * g, 4 * tq * (g + 1))
            outs.append(_gate_cols(sg, g, 0, tq) * o_c[sl] + _gate_cols(sg, g, 1, tq) * o_s[sl]
                        + _gate_cols(sg, g, 2, tq) * o_w[sl])
        _emit_heads(outs, tq, y_ref)


def _nsa_sample(layer, cache_t, table, tail_t, qn, qr, dtg, w1e, pee, b1e, w2e, ov, ex, cwin_t, wtail_t, *, past, ns):
    b, npages = table.shape
    pps = SAMPLE_PAGES_PER_STEP
    tq = SAMPLE_TQ
    nsp = ov.shape[1]
    nh = npages * (PAGE // CMP_STRIDE)
    nrow = 4 * KV_HEADS * tq
    kern = functools.partial(_nsa_sample_kernel, npages=npages, past=past, ns=ns, nsp=nsp)
    const = lambda arr: pl.BlockSpec(arr.shape, lambda i, j, pt: (0,) * arr.ndim, pipeline_mode=pl.Buffered(1))
    perb = lambda arr: pl.BlockSpec((1,) + arr.shape[1:], lambda i, j, pt: (i,) + (0,) * (arr.ndim - 1))

    def page_spec(k):
        return pl.BlockSpec((1, 1, 512, PAGE), lambda i, j, pt: (layer, pt[i, j * pps + k], 0, 0))

    return pl.pallas_call(
        kern,
        grid_spec=pltpu.PrefetchScalarGridSpec(
            num_scalar_prefetch=1,
            grid=(b, npages // pps),
            in_specs=[page_spec(k) for k in range(pps)] + [
                perb(tail_t), perb(qn), perb(qr), perb(dtg), const(w1e), const(pee), const(b1e), const(w2e),
                const(ov), const(ex),
                pl.BlockSpec((1, 1) + cwin_t.shape[2:], lambda i, j, pt: (layer, i, 0, 0)), perb(wtail_t)],
            out_specs=pl.BlockSpec((1, tq, 512), lambda i, j, pt: (i, 0, 0)),
            scratch_shapes=[pltpu.VMEM((pps, 2, PAGE, LANES), F32),
                            pltpu.VMEM((2, pps * 8, CMP_STRIDE * LANES), F32),
                            pltpu.VMEM((2, 2, nh, 256), F32),
                            pltpu.VMEM((npages + 1, nrow, LANES), F32),
                            pltpu.VMEM((npages + 1, LANES, PAGE), BF16),
                            pltpu.VMEM((npages + 1, KV_HEADS * tq, LANES), F32)]),
        out_shape=jax.ShapeDtypeStruct((b, tq, 512), F32),
        compiler_params=_cparams(("arbitrary", "arbitrary")),
    )(table, *([cache_t] * pps), tail_t, qn, qr, dtg, w1e, pee, b1e, w2e, ov, ex, cwin_t, wtail_t)


def _oproj_kernel(x_ref, ys_ref, yn_ref, w_ref, g_ref, r_ref, x1_ref, h_ref, lg_ref):
    a = jnp.concatenate([ys_ref[...], yn_ref[...]], axis=1).astype(BF16)
    x1 = x_ref[...] + _dot(a, w_ref[...])
    x1_ref[...] = x1
    h = _rms(x1, g_ref[...])
    h_ref[...] = h.astype(BF16)
    lg_ref[...] = _dot_hi(h, r_ref[...])


def _oproj(x, ys, yn, w, g, router, tm):
    t = x.shape[0]
    row = lambda n: pl.BlockSpec((tm, n), lambda i: (i, 0))
    full = lambda a: pl.BlockSpec(a.shape, lambda i: (0,) * a.ndim)
    return pl.pallas_call(
        _oproj_kernel,
        grid=(t // tm,),
        in_specs=[row(D_MODEL), row(512), row(512), full(w), full(g), full(router)],
        out_specs=[row(D_MODEL), row(D_MODEL), row(LANES)],
        out_shape=[jax.ShapeDtypeStruct((t, D_MODEL), F32), jax.ShapeDtypeStruct((t, D_MODEL), BF16),
                   jax.ShapeDtypeStruct((t, LANES), F32)],
        compiler_params=_cparams(("arbitrary",)),
    )(x, ys, yn, w, g, router)


def _ffn_kernel(h_ref, x1_ref, wg_ref, wu_ref, wd_ref, o_ref):
    f = pl.program_id(1)

    @pl.when(f == 0)
    def _():
        o_ref[...] = x1_ref[...]

    hb = h_ref[...]
    act = _silu(_dot(hb, wg_ref[...])) * _dot(hb, wu_ref[...])
    o_ref[...] += _dot(act.astype(BF16), wd_ref[...])


def _ffn(h, x1, wg, wu, wd, tm, tf):
    t = h.shape[0]
    dff = wg.shape[1]
    return pl.pallas_call(
        _ffn_kernel,
        grid=(t // tm, dff // tf),
        in_specs=[pl.BlockSpec((tm, D_MODEL), lambda i, f: (i, 0)),
                  pl.BlockSpec((tm, D_MODEL), lambda i, f: (i, 0)),
                  pl.BlockSpec((D_MODEL, tf), lambda i, f: (0, f)),
                  pl.BlockSpec((D_MODEL, tf), lambda i, f: (0, f)),
                  pl.BlockSpec((tf, D_MODEL), lambda i, f: (f, 0))],
        out_specs=pl.BlockSpec((tm, D_MODEL), lambda i, f: (i, 0)),
        out_shape=jax.ShapeDtypeStruct((t, D_MODEL), F32),
        compiler_params=_cparams(("arbitrary", "arbitrary")),
    )(h, x1, wg, wu, wd)


def _moe_kernel(be_ref, nb_ref, xb_ref, wg_ref, wu_ref, wd_ref, o_ref):
    b = pl.program_id(0)
    f = pl.program_id(1)

    @pl.when(f == 0)
    def _():
        o_ref[...] = jnp.zeros(o_ref.shape, F32)

    @pl.when(b < nb_ref[0])
    def _():
        hb = xb_ref[...]
        act = _silu(_dot(hb, wg_ref[0])) * _dot(hb, wu_ref[0])
        o_ref[...] += _dot(act.astype(BF16), wd_ref[0])


def _moe(blk_e, nblk_used, xb, wg, wu, wd, tmb, tf):
    n_blk = xb.shape[0] // tmb
    dex = wg.shape[2]
    nf = dex // tf

    def feff(b, f, nb):
        return jnp.where(b < nb[0], f, nf - 1)

    return pl.pallas_call(
        _moe_kernel,
        grid_spec=pltpu.PrefetchScalarGridSpec(
            num_scalar_prefetch=2,
            grid=(n_blk, nf),
            in_specs=[pl.BlockSpec((tmb, D_MODEL), lambda b, f, be, nb: (b, 0)),
                      pl.BlockSpec((1, D_MODEL, tf), lambda b, f, be, nb: (be[b], 0, feff(b, f, nb))),
                      pl.BlockSpec((1, D_MODEL, tf), lambda b, f, be, nb: (be[b], 0, feff(b, f, nb))),
                      pl.BlockSpec((1, tf, D_MODEL), lambda b, f, be, nb: (be[b], feff(b, f, nb), 0))],
            out_specs=pl.BlockSpec((tmb, D_MODEL), lambda b, f, be, nb: (b, 0))),
        out_shape=jax.ShapeDtypeStruct((n_blk * tmb, D_MODEL), F32),
        compiler_params=_cparams(("arbitrary", "arbitrary")),
    )(blk_e, nblk_used, xb, wg, wu, wd)


def _moe_layer(h, x1, logits, wg, wu, wd, tmb, tf):
    t = h.shape[0]
    top_v, top_e = lax.top_k(logits[:, :N_EXPERTS], TOP_K)
    gate = jax.nn.softmax(top_v, axis=-1)
    e_flat = top_e.reshape(-1)
    tok = jnp.repeat(jnp.arange(t, dtype=jnp.int32), TOP_K)
    order = jnp.argsort(e_flat)
    e_s = e_flat[order]
    tok_s = tok[order]
    counts = jnp.bincount(e_flat, length=N_EXPERTS)
    starts = jnp.cumsum(counts) - counts
    padded = (counts + tmb - 1) // tmb * tmb
    pends = jnp.cumsum(padded)
    pstarts = pends - padded
    dest_s = (pstarts[e_s] + jnp.arange(t * TOP_K, dtype=jnp.int32) - starts[e_s]).astype(jnp.int32)
    n_blk = -(-(t * TOP_K) // tmb) + N_EXPERTS
    buf_tok = jnp.full((n_blk * tmb,), t, dtype=jnp.int32).at[dest_s].set(tok_s)
    blk_e = jnp.minimum(jnp.searchsorted(pends, jnp.arange(n_blk, dtype=jnp.int32) * tmb, side='right'),
                        N_EXPERTS - 1).astype(jnp.int32)
    nblk_used = (pends[-1] // tmb).astype(jnp.int32).reshape(1)
    h_pad = jnp.concatenate([h, jnp.zeros((1, D_MODEL), h.dtype)], axis=0)
    xb = h_pad[buf_tok]
    out = _moe(blk_e, nblk_used, xb, wg, wu, wd, tmb, tf)
    dest = jnp.zeros((t * TOP_K,), jnp.int32).at[order].set(dest_s).reshape(t, TOP_K)
    return x1 + out[dest[:, 0]] * gate[:, 0:1] + out[dest[:, 1]] * gate[:, 1:2]


def _final_kernel(x_ref, g_ref, o_ref):
    o_ref[...] = _rms(x_ref[...], g_ref[...])


def _final_norm(x, g, tm):
    t = x.shape[0]
    return pl.pallas_call(
        _final_kernel,
        grid=(t // tm,),
        in_specs=[pl.BlockSpec((tm, D_MODEL), lambda i: (i, 0)), pl.BlockSpec(g.shape, lambda i: (0, 0))],
        out_specs=pl.BlockSpec((tm, D_MODEL), lambda i: (i, 0)),
        out_shape=jax.ShapeDtypeStruct((t, D_MODEL), F32),
        compiler_params=_cparams(("arbitrary",)),
    )(x, g)


def _rope_tables(pos):
    inv = 1.0 / (ROPE_THETA ** (jnp.arange(0, HEAD_DIM, 2, dtype=F32) / HEAD_DIM))
    ang = pos.astype(F32)[:, None] * inv[None, :]
    cos = jnp.cos(ang)
    sin = jnp.sin(ang)
    return jnp.concatenate([cos, cos, cos, cos], axis=1), jnp.concatenate([-sin, sin, -sin, sin], axis=1)


def _overlap(ncb, nsp, ns):
    c = jnp.arange(ncb, dtype=jnp.int32)[:, None] * CMP_STRIDE
    s = jnp.arange(nsp, dtype=jnp.int32)[None, :] * SLC_BLOCK
    ov = (c <= s + SLC_BLOCK - 1) & (c + CMP_BLOCK - 1 >= s) & (jnp.arange(nsp)[None, :] < ns)
    return ov.astype(BF16)


def _pad_lanes(v, n):
    return jnp.zeros((1, n), F32).at[0, :v.shape[0]].set(v.astype(F32))


def kernel(x_prompt, x_sample, cache_kv, cache_win, state_ssm, state_conv, page_table, ln_mix, w_in, conv_w,
           conv_b, dt_bias, a_log, d_skip, ssd_norm, cmp_pos, cmp_w1, cmp_b1, cmp_w2, w_out, ln_ffn,
           ffn_w_gate, ffn_w_up, ffn_w_down, moe_router, moe_w_gate, moe_w_up, moe_w_down, ln_final):
    depth = w_in.shape[0]
    bsz, seq = x_prompt.shape[:2]
    dbsz, dseq = x_sample.shape[:2]
    n_pages = page_table.shape[1]
    past = n_pages * cache_kv.shape[2]
    win_buf = cache_win.shape[2]
    npool = cache_kv.shape[1]
    assert bsz == 1 and cache_kv.shape[2] == PAGE and seq % PAGE == 0 and seq >= WINDOW + PAGE
    assert win_buf == WINDOW and past % PAGE == 0
    assert not any(past <= CMP_STRIDE * c + CMP_BLOCK - 1 <= past + dseq - 1
                   for c in range(past // CMP_STRIDE - 2, past // CMP_STRIDE + 2))

    ts = dbsz * dseq
    ns_p = seq // SLC_BLOCK
    ns_s = -(-(past + dseq) // SLC_BLOCK)
    nsp_p = -(-ns_p // LANES) * LANES
    nsp_s = -(-ns_s // LANES) * LANES
    assert dseq <= SAMPLE_TQ and n_pages % SAMPLE_PAGES_PER_STEP == 0
    slab_pages = SLAB_BLOCKS * SLC_BLOCK // PAGE
    assert (nsp_s // LANES - 1) * slab_pages < n_pages + 1 <= (nsp_s // LANES) * slab_pages
    cache_t = jnp.transpose(cache_kv, (0, 1, 3, 4, 5, 2)).reshape(depth, npool, 512, PAGE)
    cwin_t = jnp.transpose(cache_win, (0, 1, 3, 4, 5, 2)).reshape(depth, dbsz, 256, win_buf)
    ex = (jnp.arange(SLAB_BLOCKS * SLC_BLOCK, dtype=jnp.int32)[None, :] // SLC_BLOCK
          == jnp.arange(SLAB_BLOCKS, dtype=jnp.int32)[:, None]).astype(BF16)

    cs_p, sn_p = _rope_tables(jnp.arange(seq, dtype=jnp.int32))
    cs_s, sn_s = _rope_tables(past + jnp.arange(ts, dtype=jnp.int32) % dseq)
    ov_p = _overlap(seq // CMP_STRIDE, nsp_p, ns_p)
    ov_s = _overlap(past // CMP_STRIDE, nsp_s, ns_s)
    ident = jnp.arange(seq // PAGE, dtype=jnp.int32).reshape(1, -1)

    xp = x_prompt.reshape(seq, D_MODEL)
    xs = x_sample.reshape(ts, D_MODEL)
    outs = {k: [] for k in ('kv_p', 'kv_s', 'win_p', 'win_s', 'ssm_p', 'ssm_s', 'conv_p', 'conv_s')}

    for l in range(depth):
        w = w_in[l]
        wr = jnp.concatenate([w[:, :1536], w[:, 1544:2824], w[:, 1536:1544], w[:, 2824:2848],
                              jnp.zeros((D_MODEL, C_END - 2848), F32)], axis=1).astype(BF16)
        g_mix = ln_mix[l].reshape(1, -1)
        cw8 = jnp.zeros((8, CONV_DIM), F32).at[:CONV_W].set(conv_w[l])
        cb = conv_b[l].reshape(1, -1)
        dtb = _pad_lanes(dt_bias[l], LANES)
        a_neg = _pad_lanes(-jnp.exp(a_log[l].astype(F32)), LANES)
        dsk = jnp.repeat(d_skip[l].astype(F32), SSD_HEAD_DIM).reshape(1, -1)
        ng = ssd_norm[l].reshape(1, -1)
        w1 = cmp_w1[l].reshape(2, 2, CMP_STRIDE, HEAD_DIM, CMP_HIDDEN)
        eye = jnp.eye(KV_HEADS, dtype=F32)
        w1e = jnp.einsum('vspdh,gk->vspgdkh', w1, eye).reshape(2, 2, CMP_STRIDE * LANES, 256).astype(BF16)
        pe = cmp_pos[l].reshape(2, 2, CMP_STRIDE, 1, HEAD_DIM)
        pee = jnp.broadcast_to(pe, (2, 2, CMP_STRIDE, KV_HEADS, HEAD_DIM)).reshape(2, 2, CMP_STRIDE * LANES)
        b1e = jnp.concatenate([cmp_b1[l], cmp_b1[l]], axis=1)
        w2e = jnp.einsum('vhd,gk->vghkd', cmp_w2[l], eye).reshape(2, 256, LANES).astype(BF16)
        wo = w_out[l].astype(BF16)
        g_ffn = ln_ffn[l].reshape(1, -1)
        if l % 2 == 0:
            router = jnp.zeros((D_MODEL, LANES), F32)
        else:
            router = jnp.zeros((D_MODEL, LANES), F32).at[:, :N_EXPERTS].set(moe_router[l // 2])

        z, xbc, dtg, q, qr, kv, win = _front(xp, g_mix, wr, cs_p, sn_p, 256)
        y_ssd, hist, hfin = _ssd(z[None], xbc[None], dtg[None], jnp.zeros((1, 8, CONV_DIM), F32),
                                 jnp.zeros((1, D_SSD, SSD_STATE), F32), cw8, cb, dtb, a_neg, dsk, ng, SSD_CHUNK)
        kv_pages = kv.reshape(seq // PAGE, PAGE, 512)
        kcvc = _compress(kv_pages, ident, w1e, pee, b1e, w2e)
        kaug, vt = _kvfmt(kv, NSA_TK)
        y_nsa = _nsa(q[None], qr[None], dtg[None], kcvc, ov_p, kaug[None], vt,
                     win[None, :, 0:128].astype(BF16), win[None, :, 128:256].astype(BF16),
                     tq=128, ns=ns_p, tk=NSA_TK, wl=WINDOW + 128)
        x1, h, logits = _oproj(xp, y_ssd[0], y_nsa[0], wo, g_ffn, router, 512)
        outs['kv_p'].append(kv.reshape(1, seq, 4, KV_HEADS, HEAD_DIM))
        outs['win_p'].append(win[seq - WINDOW:].reshape(1, WINDOW, 2, KV_HEADS, HEAD_DIM))
        outs['ssm_p'].append(hfin.reshape(1, SSD_HEADS, SSD_HEAD_DIM, SSD_STATE))
        outs['conv_p'].append(hist[:, 5:8])

        zs, xbcs, dtgs, qs, qrs, kvs, wins = _front(xs, g_mix, wr, cs_s, sn_s, ts)
        padl = lambda v: jnp.pad(v.reshape(dbsz, dseq, -1), ((0, 0), (0, SSD_CHUNK - dseq), (0, 0)))
        hist8 = jnp.pad(state_conv[l], ((0, 0), (8 - (CONV_W - 1), 0), (0, 0)))
        y_ssd_s, hist_s, hfin_s = _ssd(padl(zs), padl(xbcs), padl(dtgs), hist8,
                                       state_ssm[l].reshape(dbsz, D_SSD, SSD_STATE),
                                       cw8, cb, dtb, a_neg, dsk, ng, dseq)
        tail_t = jnp.pad(jnp.transpose(kvs.reshape(dbsz, dseq, 512), (0, 2, 1)), ((0, 0), (0, 0), (0, PAGE - dseq)))
        wtail_t = jnp.pad(jnp.transpose(wins.reshape(dbsz, dseq, 256), (0, 2, 1)), ((0, 0), (0, 0), (0, PAGE - dseq)))
        padq = lambda v: jnp.pad(v.reshape(dbsz, dseq, -1), ((0, 0), (0, SAMPLE_TQ - dseq), (0, 0)))
        y_nsa_s = _nsa_sample(l, cache_t, page_table, tail_t, padq(qs), padq(qrs), padq(dtgs), w1e, pee, b1e, w2e,
                              ov_s, ex, cwin_t, wtail_t, past=past, ns=ns_s)
        y_nsa_s = y_nsa_s[:, :dseq].reshape(ts, D_NSA)
        x1s, hs, logits_s = _oproj(xs, y_ssd_s[:, :dseq].reshape(ts, D_SSD), y_nsa_s, wo, g_ffn, router, ts)
        outs['kv_s'].append(kvs.reshape(dbsz, dseq, 4, KV_HEADS, HEAD_DIM))
        win_all = jnp.concatenate([cache_win[l], wins.reshape(dbsz, dseq, 2, KV_HEADS, HEAD_DIM)], axis=1)
        outs['win_s'].append(win_all[:, dseq:])
        outs['ssm_s'].append(hfin_s.reshape(dbsz, SSD_HEADS, SSD_HEAD_DIM, SSD_STATE))
        outs['conv_s'].append(hist_s[:, 5:8])

        if l % 2 == 0:
            wg = ffn_w_gate[l // 2].astype(BF16)
            wu = ffn_w_up[l // 2].astype(BF16)
            wd = ffn_w_down[l // 2].astype(BF16)
            xp = _ffn(h, x1, wg, wu, wd, 1024, 256)
            xs = _ffn(hs, x1s, wg, wu, wd, ts, 256)
        else:
            wg = moe_w_gate[l // 2].astype(BF16)
            wu = moe_w_up[l // 2].astype(BF16)
            wd = moe_w_down[l // 2].astype(BF16)
            xp = _moe_layer(h, x1, logits, wg, wu, wd, 1024, 512)
            xs = _moe_layer(hs, x1s, logits_s, wg, wu, wd, 128, 512)

    g_fin = ln_final.reshape(1, -1)
    y_prompt = _final_norm(xp, g_fin, 1024).reshape(bsz, seq, D_MODEL)
    y_sample = _final_norm(xs, g_fin, ts).reshape(dbsz, dseq, D_MODEL)
    st = lambda k: jnp.stack(outs[k])
    return (y_prompt, y_sample, st('kv_p'), st('kv_s'), st('win_p'), st('win_s'),
            st('ssm_p'), st('ssm_s'), st('conv_p'), st('conv_s'))
```

```python
import functools
import math

import jax
import jax.numpy as jnp
from jax import lax
from jax.experimental import pallas as pl
from jax.experimental.pallas import tpu as pltpu

F32 = jnp.float32
BF16 = jnp.bfloat16

D_MODEL = 1024
SSD_HEADS = 8
SSD_HEAD_DIM = 64
D_SSD = 512
SSD_GROUPS = 2
SSD_STATE = 128
CONV_W = 4
CONV_DIM = 1024
SSD_CHUNK = 128
NSA_HEADS = 8
HEAD_DIM = 64
KV_HEADS = 2
D_NSA = 512
CMP_BLOCK = 32
CMP_STRIDE = 16
CMP_HIDDEN = 128
SLC_BLOCK = 64
TOP_N = 16
WINDOW = 512
ROPE_THETA = 10000.0
FORCE_BONUS = 1.0e4
N_EXPERTS = 8
TOP_K = 2
EPS = 1e-6

PAGE = 128
LANES = 128
LOG2E = 1.4426950408889634
NEG = -1.0e30
SLAB_BLOCKS = 128
VMEM_LIMIT = 56 * 1024 * 1024

C_Z, C_XBC, C_Q, C_KV, C_DT, C_END = 0, 512, 1536, 2048, 2816, 2944


def _cparams(sem):
    return pltpu.CompilerParams(dimension_semantics=sem, vmem_limit_bytes=VMEM_LIMIT)


def _dot(a, b):
    return jnp.dot(a, b, preferred_element_type=F32)


def _dot_nt(a, b):
    return lax.dot_general(a, b, (((1,), (1,)), ((), ())), preferred_element_type=F32)


def _split3(x):
    h1 = x.astype(BF16)
    r1 = x - h1.astype(F32)
    h2 = r1.astype(BF16)
    h3 = (r1 - h2.astype(F32)).astype(BF16)
    return h1, h2, h3


def _dot_nt_hi(a, b):
    a1, a2, _ = _split3(a)
    b1, b2, _ = _split3(b)
    return _dot_nt(a1, b1) + (_dot_nt(a1, b2) + _dot_nt(a2, b1))


def _dot_hi(a, b):
    a1, a2, _ = _split3(a)
    b1, b2, _ = _split3(b)
    return _dot(a1, b1) + (_dot(a1, b2) + _dot(a2, b1))


def _silu(x):
    return x * jax.nn.sigmoid(x)


def _rms(x, g):
    return x * lax.rsqrt(jnp.mean(x * x, axis=-1, keepdims=True) + EPS) * g


def _front_kernel(x_ref, g_ref, w_ref, cs_ref, sn_ref,
                  z_ref, xbc_ref, dtg_ref, q_ref, qr_ref, kv_ref, win_ref):
    xn = _rms(x_ref[...], g_ref[...])
    p = _dot(xn.astype(BF16), w_ref[...])
    cs = cs_ref[...]
    sn = sn_ref[...]
    lane = lax.broadcasted_iota(jnp.int32, cs.shape, 1)
    first = (lane % HEAD_DIM) < (HEAD_DIM // 2)

    def rope(v):
        sw = jnp.where(first, pltpu.roll(v, LANES - HEAD_DIM // 2, 1), pltpu.roll(v, HEAD_DIM // 2, 1))
        return v * cs + sw * sn

    z_ref[...] = p[:, C_Z:C_XBC]
    xbc_ref[...] = p[:, C_XBC:C_Q]
    dtg_ref[...] = p[:, C_DT:C_END]
    q_ref[...] = p[:, C_Q:C_KV]
    for s in range(4):
        qr_ref[:, s * LANES:(s + 1) * LANES] = rope(p[:, C_Q + s * LANES:C_Q + (s + 1) * LANES])
    kv_ref[:, 0:256] = p[:, C_KV:C_KV + 256]
    kv_ref[:, 256:384] = rope(p[:, C_KV + 256:C_KV + 384])
    kv_ref[:, 384:512] = p[:, C_KV + 384:C_KV + 512]
    win_ref[:, 0:128] = rope(p[:, C_KV + 512:C_KV + 640])
    win_ref[:, 128:256] = p[:, C_KV + 640:C_KV + 768]


def _front(x, g, w, cs, sn, tm):
    t = x.shape[0]
    row = lambda n: pl.BlockSpec((tm, n), lambda i: (i, 0))
    full = lambda a: pl.BlockSpec(a.shape, lambda i: (0,) * a.ndim)
    outs = [512, 1024, 128, 512, 512, 512, 256]
    return pl.pallas_call(
        _front_kernel,
        grid=(t // tm,),
        in_specs=[row(D_MODEL), full(g), full(w), row(LANES), row(LANES)],
        out_specs=[row(n) for n in outs],
        out_shape=[jax.ShapeDtypeStruct((t, n), F32) for n in outs],
        compiler_params=_cparams(("arbitrary",)),
    )(x, g, w, cs, sn)


def _expand_heads(v):
    r = v.shape[0]
    lane = lax.broadcasted_iota(jnp.int32, (r, LANES), 1)
    outs = []
    for k in range(4):
        a = jnp.broadcast_to(v[:, 2 * k:2 * k + 1], (r, LANES))
        b = jnp.broadcast_to(v[:, 2 * k + 1:2 * k + 2], (r, LANES))
        outs.append(jnp.where(lane < SSD_HEAD_DIM, a, b))
    return jnp.concatenate(outs, axis=1)


def _ssd_kernel(z_ref, xbc_ref, dtg_ref, hist_ref, h0_ref, cw_ref, cb_ref, dtb_ref, a_ref, dsk_ref, ng_ref,
                y_ref, hist_o_ref, hfin_ref, xpad, ht, *, nchunks, valid_last):
    c = pl.program_id(1)
    q = SSD_CHUNK

    @pl.when(c == 0)
    def _():
        xpad[0:8, :] = hist_ref[0]
        ht[...] = h0_ref[0].T

    xpad[8:8 + q, :] = xbc_ref[0]
    conv = cb_ref[...]
    for k in range(CONV_W):
        conv = conv + xpad[5 + k:5 + k + q, :] * cw_ref[k:k + 1, :]
    xc = _silu(conv)
    xs = xc[:, 0:D_SSD]
    bm = xc[:, D_SSD:D_SSD + 256]
    cm = xc[:, D_SSD + 256:D_SSD + 512]

    row = lax.broadcasted_iota(jnp.int32, (q, q), 0)
    col = lax.broadcasted_iota(jnp.int32, (q, q), 1)
    nvalid = jnp.where(c == nchunks - 1, valid_last, q)
    xdt_in = dtg_ref[0] + dtb_ref[...]
    dt = jnp.maximum(xdt_in, 0.0) + jnp.log1p(jnp.exp(-jnp.abs(xdt_in)))
    dt = jnp.where(row < nvalid, dt, 0.0)
    dta = dt * a_ref[...]
    causal = row >= col
    cum = jnp.dot(causal.astype(F32), dta, preferred_element_type=F32, precision=lax.Precision.HIGHEST)
    cum_t = cum.T
    ecum = jnp.exp(cum)
    toend = jnp.exp(cum[q - 1:q, :] - cum)
    dt_e = _expand_heads(dt)
    ecum_e = _expand_heads(ecum)
    toend_e = _expand_heads(toend)
    xdt = xs * dt_e
    xdt_b = xdt.astype(BF16)
    xw_b = (xdt * toend_e).astype(BF16)
    lane = lax.broadcasted_iota(jnp.int32, (q, LANES), 1)

    y_intra = []
    y_inter = []
    for g in range(SSD_GROUPS):
        bg = bm[:, g * 128:(g + 1) * 128]
        cg_b = cm[:, g * 128:(g + 1) * 128].astype(BF16)
        bg_b = bg.astype(BF16)
        cb = _dot_nt(cg_b, bg_b)
        htg = ht[:, g * 256:(g + 1) * 256]
        y_inter.append(_dot(cg_b, htg.astype(BF16)))
        for kk in range(2):
            slab = xdt_b[:, (2 * g + kk) * LANES:(2 * g + kk + 1) * LANES]
            res = []
            for hh in range(2):
                h = 4 * g + 2 * kk + hh
                seg = cum[:, h:h + 1] - cum_t[h:h + 1, :]
                decay = jnp.exp(jnp.where(causal, seg, NEG))
                res.append(_dot((decay * cb).astype(BF16), slab))
            y_intra.append(jnp.where(lane < SSD_HEAD_DIM, res[0], res[1]))
        s_new = _dot(bg.T.astype(BF16), xw_b[:, g * 256:(g + 1) * 256])
        ht[:, g * 256:(g + 1) * 256] = htg * ecum_e[q - 1:q, g * 256:(g + 1) * 256] + s_new

    y = jnp.concatenate(y_intra, axis=1) + jnp.concatenate(y_inter, axis=1) * ecum_e + dsk_ref[...] * xs
    y = y * _silu(z_ref[0])
    y_ref[0] = _rms(y, ng_ref[...])

    xpad[0:8, :] = xpad[q:q + 8, :]

    @pl.when(c == nchunks - 1)
    def _():
        hist_o_ref[0] = xpad[valid_last:valid_last + 8, :]
        hfin_ref[0] = ht[...].T


def _ssd(z, xbc, dtg, hist8, h0, cw8, cb, dtb, a, dsk, ng, valid_last):
    b, l = z.shape[:2]
    nchunks = l // SSD_CHUNK
    rows = lambda n: pl.BlockSpec((1, SSD_CHUNK, n), lambda i, c: (i, c, 0))
    perb = lambda r, n: pl.BlockSpec((1, r, n), lambda i, c: (i, 0, 0))
    full = lambda arr: pl.BlockSpec(arr.shape, lambda i, c: (0,) * arr.ndim)
    kern = functools.partial(_ssd_kernel, nchunks=nchunks, valid_last=valid_last)
    return pl.pallas_call(
        kern,
        grid=(b, nchunks),
        in_specs=[rows(512), rows(1024), rows(128), perb(8, 1024), perb(512, 128),
                  full(cw8), full(cb), full(dtb), full(a), full(dsk), full(ng)],
        out_specs=[rows(512), perb(8, 1024), perb(512, 128)],
        out_shape=[jax.ShapeDtypeStruct((b, l, 512), F32),
                   jax.ShapeDtypeStruct((b, 8, 1024), F32),
                   jax.ShapeDtypeStruct((b, 512, 128), F32)],
        scratch_shapes=[pltpu.VMEM((SSD_CHUNK + 8, CONV_DIM), F32), pltpu.VMEM((SSD_STATE, D_SSD), F32)],
        compiler_params=_cparams(("arbitrary", "arbitrary")),
    )(z, xbc, dtg, hist8, h0, cw8, cb, dtb, a, dsk, ng)


SUM_ROWS = 16
NSA_TK = 512
CMP_GROUP_PAGES = 16


def _compress_kernel(pt_ref, pgk_ref, pgv_ref, w1_ref, pe_ref, b1_ref, w2_ref, w2t_ref, kchi_ref, kclo_ref, vct_ref,
                     xs, ps, *, npages):
    p = pl.program_id(1)
    sub = p % CMP_GROUP_PAGES
    hb = PAGE // CMP_STRIDE
    r0 = pl.multiple_of(sub * hb, hb)
    for kv, pg_ref in enumerate((pgk_ref, pgv_ref)):
        for pos in range(CMP_STRIDE):
            xs[kv, pl.ds(r0, hb), pos * LANES:(pos + 1) * LANES] = pg_ref[0, pl.ds(pos, hb, stride=CMP_STRIDE), :]

    @pl.when(sub == CMP_GROUP_PAGES - 1)
    def _():
        rows = CMP_GROUP_PAGES * hb
        g0 = pl.multiple_of((p // CMP_GROUP_PAGES) * rows, rows)
        for kv in range(2):
            x = xs[kv]
            for s in range(2):
                ps[kv, s, pl.ds(g0, rows), :] = _dot((x + pe_ref[kv, s:s + 1, :]).astype(BF16), w1_ref[kv, s])

    @pl.when(p == npages - 1)
    def _():
        nh = npages * hb
        act = [jax.nn.gelu(b1_ref[kv:kv + 1, :] + ps[kv, 0] + pltpu.roll(ps[kv, 1], nh - 1, 0)).astype(BF16)
               for kv in range(2)]
        kc = _dot(act[0], w2_ref[0])
        hi = kc.astype(BF16)
        kchi_ref[0] = hi
        kclo_ref[0] = (kc - hi.astype(F32)).astype(BF16)
        vct_ref[0] = _dot_nt(w2t_ref[...], act[1]).astype(BF16)


def _compress(pages, table, w1e, pee, b1e, w2e):
    b, npages = table.shape
    w2t = w2e[1].T
    nh = npages * (PAGE // CMP_STRIDE)
    kern = functools.partial(_compress_kernel, npages=npages)
    full = lambda arr: pl.BlockSpec(arr.shape, lambda i, p, pt: (0,) * arr.ndim)
    return pl.pallas_call(
        kern,
        grid_spec=pltpu.PrefetchScalarGridSpec(
            num_scalar_prefetch=1,
            grid=(b, npages),
            in_specs=[pl.BlockSpec((1, PAGE, LANES), lambda i, p, pt: (pt[i, p], 0, 0)),
                      pl.BlockSpec((1, PAGE, LANES), lambda i, p, pt: (pt[i, p], 0, 1)),
                      full(w1e), full(pee), full(b1e), full(w2e), full(w2t)],
            out_specs=[pl.BlockSpec((1, nh, LANES), lambda i, p, pt: (i, 0, 0)),
                       pl.BlockSpec((1, nh, LANES), lambda i, p, pt: (i, 0, 0)),
                       pl.BlockSpec((1, LANES, nh), lambda i, p, pt: (i, 0, 0))],
            scratch_shapes=[pltpu.VMEM((2, CMP_GROUP_PAGES * 8, CMP_STRIDE * LANES), F32),
                            pltpu.VMEM((2, 2, nh, 256), F32)]),
        out_shape=[jax.ShapeDtypeStruct((b, nh, LANES), BF16), jax.ShapeDtypeStruct((b, nh, LANES), BF16),
                   jax.ShapeDtypeStruct((b, LANES, nh), BF16)],
        compiler_params=_cparams(("arbitrary", "arbitrary")),
    )(table, pages, pages, w1e, pee, b1e, w2e, w2t)


def _kvfmt_kernel(kv_ref, win_ref, k_ref, vt_ref, kw_ref, vwt_ref):
    p = pl.program_id(0)
    src = kv_ref[...]
    r = lax.broadcasted_iota(jnp.int32, (PAGE, LANES), 0)
    lane = lax.broadcasted_iota(jnp.int32, (PAGE, LANES), 1)
    blk = (2 * p + r // SLC_BLOCK) % SLAB_BLOCKS
    k_ref[:, 0:128] = src[:, 0:128].astype(BF16)
    k_ref[:, 128:256] = jnp.where(lane == blk, 1.0, 0.0).astype(BF16)
    vt_ref[0, 0:LANES, :] = src[:, 128:256].T.astype(BF16)
    vt_ref[0, LANES:LANES + SUM_ROWS, :] = jnp.ones((SUM_ROWS, PAGE), BF16)
    kw_ref[...] = win_ref[:, 0:128].astype(BF16)
    vwt_ref[0] = win_ref[:, 128:256].T.astype(BF16)


def _kvfmt(kv, win, tk):
    l = kv.shape[0]
    per = tk // PAGE
    return pl.pallas_call(
        _kvfmt_kernel,
        grid=(l // PAGE,),
        in_specs=[pl.BlockSpec((PAGE, 256), lambda p: (p, 1)), pl.BlockSpec((PAGE, 256), lambda p: (p, 0))],
        out_specs=[pl.BlockSpec((PAGE, 256), lambda p: (p, 0)),
                   pl.BlockSpec((1, LANES + SUM_ROWS, PAGE), lambda p: (p // per, 0, p % per)),
                   pl.BlockSpec((PAGE, LANES), lambda p: (p, 0)),
                   pl.BlockSpec((1, LANES, PAGE), lambda p: (p, 0, 0))],
        out_shape=[jax.ShapeDtypeStruct((l, 256), BF16),
                   jax.ShapeDtypeStruct((l // tk, LANES + SUM_ROWS, tk), BF16),
                   jax.ShapeDtypeStruct((l, LANES), BF16),
                   jax.ShapeDtypeStruct((l // PAGE, LANES, PAGE), BF16)],
        compiler_params=_cparams(("arbitrary",)),
    )(kv, win)


def _group_rows(qfull, g, tq):
    lane = lax.broadcasted_iota(jnp.int32, (tq, LANES), 1)
    keep = (lane >= HEAD_DIM) if g else (lane < HEAD_DIM)
    pieces = []
    for r in range(4):
        h = 4 * g + r
        slab = qfull[:, (h // 2) * LANES:(h // 2 + 1) * LANES]
        if h % 2 != g:
            slab = pltpu.roll(slab, HEAD_DIM, 1)
        pieces.append(jnp.where(keep, slab, 0.0))
    return jnp.concatenate(pieces, axis=0)


def _masked_softmax(s, mask):
    s = jnp.where(mask, s, NEG)
    m = jnp.max(s, axis=-1, keepdims=True)
    e = jnp.where(mask, jnp.exp(s - m), 0.0)
    return e / jnp.maximum(jnp.sum(e, axis=-1, keepdims=True), 1e-30)


def _importance(p4, ov):
    h1, h2, h3 = _split3(p4)
    return _dot(h1, ov) + (_dot(h2, ov) + _dot(h3, ov))


def _topk_bias(imp, qpos, ns):
    tq, nsp = imp.shape
    jf = lax.broadcasted_iota(jnp.int32, (tq, nsp), 1)
    jff = jf.astype(F32)
    cur = qpos >> 6
    valid = jf <= cur
    forced = valid & ((jf == 0) | (jf == cur) | (jf == cur - 1))
    score = jnp.where(forced, FORCE_BONUS, jnp.where(valid, imp, -1.0))
    score = jnp.where(jf < ns, score, -2.0)

    def pick(_, carry):
        sc, sb = carry
        mx = jnp.max(sc, axis=-1, keepdims=True)
        jm = jnp.min(jnp.where(sc == mx, jff, float(nsp)), axis=-1, keepdims=True)
        hit = jff == jm
        return jnp.where(hit, -3.0, sc), jnp.where(hit, 0.0, sb)

    return lax.fori_loop(0, TOP_N, pick, (score, jnp.full((tq, nsp), NEG, F32)))[1]


def _gate_cols(sg, g, k, tq):
    return jnp.concatenate([sg[:, 8 + 3 * (4 * g + r) + k:8 + 3 * (4 * g + r) + k + 1] for r in range(4)], axis=0)


def _emit_heads(outs, tq, y_ref):
    lane = lax.broadcasted_iota(jnp.int32, (tq, LANES), 1)
    for k in range(4):
        g = k // 2
        r0 = (2 * k) % 4
        a = outs[g][r0 * tq:(r0 + 1) * tq]
        b = outs[g][(r0 + 1) * tq:(r0 + 2) * tq]
        if g == 0:
            slab = jnp.where(lane < HEAD_DIM, a, pltpu.roll(b, HEAD_DIM, 1))
        else:
            slab = jnp.where(lane < HEAD_DIM, pltpu.roll(a, HEAD_DIM, 1), b)
        y_ref[0, :, k * LANES:(k + 1) * LANES] = slab


def _masked_softmax_t(s, mask):
    s = jnp.where(mask, s, NEG)
    m = jnp.max(s, axis=0, keepdims=True)
    e = jnp.where(mask, jnp.exp(s - m), 0.0)
    return e / jnp.maximum(jnp.sum(e, axis=0, keepdims=True), 1e-30)


def _topk_bias_t(imp, qpos, ns):
    nsp, tq = imp.shape
    jf = lax.broadcasted_iota(jnp.int32, (nsp, tq), 0)
    jff = jf.astype(F32)
    cur = qpos >> 6
    valid = jf <= cur
    forced = valid & ((jf == 0) | (jf == cur) | (jf == cur - 1))
    score = jnp.where(forced, FORCE_BONUS, jnp.where(valid, imp, -1.0))
    score = jnp.where(jf < ns, score, -2.0)

    def pick(_, carry):
        sc, sb = carry
        mx = jnp.max(sc, axis=0, keepdims=True)
        jm = jnp.min(jnp.where(sc == mx, jff, float(nsp)), axis=0, keepdims=True)
        hit = jff == jm
        return jnp.where(hit, -3.0, sc), jnp.where(hit, 0.0, sb)

    return lax.fori_loop(0, TOP_N, pick, (score, jnp.full((nsp, tq), NEG, F32)))[1]


def _nsa_kernel(qn_ref, qr_ref, dtg_ref, kchi_ref, kclo_ref, vct_ref, kaug_ref, vt_ref, kwin_ref, vwt_ref,
                y_ref, qat, p4_sc, sa_sc, sb_sc, m_sc, acc_sc, *, tq, ns, nsp, tk, wl):
    t = pl.program_id(1)
    scale = HEAD_DIM ** -0.5
    r4 = 4 * tq
    nslab = nsp // LANES
    lane4 = lax.broadcasted_iota(jnp.int32, (1, r4), 1)
    posr = t * tq + (lane4 & (tq - 1))
    qpos = t * tq + lax.broadcasted_iota(jnp.int32, (1, tq), 1)

    qn = qn_ref[0] * scale
    qr = qr_ref[0] * scale
    kc_hi = kchi_ref[0]
    kc_lo = kclo_ref[0]
    ncb = kc_hi.shape[0]
    c_end = lax.broadcasted_iota(jnp.int32, (ncb, 1), 0) * CMP_STRIDE + (CMP_BLOCK - 1)
    mask_c = c_end <= posr

    ws = pl.multiple_of(jnp.maximum(t * tq - WINDOW, 0), LANES)
    kw = kwin_ref[pl.ds(ws, wl), :]
    kposw = ws + lax.broadcasted_iota(jnp.int32, (wl, 1), 0)
    dpos = posr - kposw
    mask_w = (dpos >= 0) & (dpos < WINDOW)
    wpage = ws // PAGE

    p4_sc[0:8, :] = jnp.zeros((8, tq), F32)
    p4_sc[8 + ncb:, :] = jnp.zeros((p4_sc.shape[0] - 8 - ncb, tq), F32)
    o_cmp = []
    o_win = []
    for g in range(KV_HEADS):
        qn_t = _group_rows(qn, g, tq).T
        qr_t = _group_rows(qr, g, tq).T
        qh = qn_t.astype(BF16)
        ql = (qn_t - qh.astype(F32)).astype(BF16)
        s_c = _dot(kc_hi, qh) + (_dot(kc_hi, ql) + _dot(kc_lo, qh))
        p_c = _masked_softmax_t(s_c, mask_c)
        o_cmp.append(_dot(vct_ref[0], p_c.astype(BF16)))
        p4 = p_c[:, 0:tq] + p_c[:, tq:2 * tq] + p_c[:, 2 * tq:3 * tq] + p_c[:, 3 * tq:4 * tq]
        p4_sc[8:8 + ncb, :] = p4
        per = SLC_BLOCK // CMP_STRIDE
        imp = p4_sc[pl.ds(8 - 1, nsp, stride=per), :]
        for k in range(per):
            imp = imp + p4_sc[pl.ds(8 + k, nsp, stride=per), :]
        selb = _topk_bias_t(imp, qpos, ns)
        qs_b = (qr_t * LOG2E).astype(BF16)
        for sl in range(nslab):
            piece = selb[sl * LANES:(sl + 1) * LANES, :]
            qat[sl, g, 0:LANES, :] = qs_b
            qat[sl, g, LANES:2 * LANES, :] = jnp.concatenate([piece] * 4, axis=1).astype(BF16)
        p_w = _masked_softmax_t(_dot(kw, qr_t.astype(BF16)), mask_w).astype(BF16)
        o_w = _dot(vwt_ref[wpage], p_w[0:PAGE])
        for i in range(1, wl // PAGE):
            o_w = o_w + _dot(vwt_ref[wpage + i], p_w[i * PAGE:(i + 1) * PAGE])
        o_win.append(o_w)

    m_sc[...] = jnp.full(m_sc.shape, 2.0 * NEG, F32)
    acc_sc[...] = jnp.zeros(acc_sc.shape, F32)
    ndiag = (t * tq + tq - 1) // tk

    def scores(jt, s_buf):
        k0 = pl.multiple_of(jt * tk, tk)
        kt = kaug_ref[pl.ds(k0, tk), :]
        sb = k0 // (SLAB_BLOCKS * SLC_BLOCK)
        for g in range(KV_HEADS):
            s_buf[g] = _dot(kt, qat[sb, g])

    def accumulate(jt, s_buf, causal):
        vt = vt_ref[jt]
        for g in range(KV_HEADS):
            s = s_buf[g]
            if causal:
                kpos = jt * tk + lax.broadcasted_iota(jnp.int32, (tk, 1), 0)
                s = jnp.where(kpos <= posr, s, NEG)
            m_old = m_sc[g]
            m_new = jnp.maximum(m_old, jnp.max(s, axis=0, keepdims=True))
            pe = jnp.exp2(s - m_new).astype(BF16)
            acc_sc[g] = jnp.exp2(m_old - m_new) * acc_sc[g] + _dot(vt, pe)
            m_sc[g] = m_new

    scores(0, sa_sc)

    def tile_pair(p, carry):
        jt = 2 * p
        scores(jt + 1, sb_sc)
        accumulate(jt, sa_sc, False)
        scores(jt + 2, sa_sc)
        accumulate(jt + 1, sb_sc, False)
        return carry

    npair = ndiag // 2
    lax.fori_loop(0, npair, tile_pair, 0)

    @pl.when(ndiag % 2 == 0)
    def _():
        accumulate(ndiag, sa_sc, True)

    @pl.when(ndiag % 2 == 1)
    def _():
        scores(ndiag, sb_sc)
        accumulate(ndiag - 1, sa_sc, False)
        accumulate(ndiag, sb_sc, True)

    sg_t = jax.nn.sigmoid(dtg_ref[0]).T
    outs = []
    for g in range(KV_HEADS):
        gate = [jnp.concatenate([sg_t[8 + 3 * (4 * g + r) + k:8 + 3 * (4 * g + r) + k + 1, :] for r in range(4)], axis=1)
                for k in range(3)]
        o_sel = acc_sc[g, 0:LANES, :] / acc_sc[g, LANES:LANES + 1, :]
        o_t = gate[0] * o_cmp[g] + gate[1] * o_sel + gate[2] * o_win[g]
        outs.append(o_t.T)
    _emit_heads(outs, tq, y_ref)


def _nsa(qn, qr, dtg, kchi, kclo, vct, kaug, vt, kwin, vwt, *, tq, ns, nsp, tk, wl):
    b, lq = qn.shape[:2]
    ncb = kchi.shape[1]
    assert nsp * (SLC_BLOCK // CMP_STRIDE) >= ncb
    nslab = nsp // LANES
    kern = functools.partial(_nsa_kernel, tq=tq, ns=ns, nsp=nsp, tk=tk, wl=wl)
    rows = lambda n: pl.BlockSpec((1, tq, n), lambda i, t: (i, t, 0))
    whole = lambda arr: pl.BlockSpec(arr.shape, lambda i, t: (0,) * arr.ndim, pipeline_mode=pl.Buffered(1))
    return pl.pallas_call(
        kern,
        grid=(b, lq // tq),
        in_specs=[rows(512), rows(512), rows(128), whole(kchi), whole(kclo), whole(vct),
                  whole(kaug), whole(vt), whole(kwin), whole(vwt)],
        out_specs=rows(512),
        out_shape=jax.ShapeDtypeStruct((b, lq, 512), F32),
        scratch_shapes=[pltpu.VMEM((nslab, KV_HEADS, 2 * LANES, 4 * tq), BF16),
                        pltpu.VMEM((nsp * (SLC_BLOCK // CMP_STRIDE) + 16, tq), F32),
                        pltpu.VMEM((KV_HEADS, tk, 4 * tq), F32),
                        pltpu.VMEM((KV_HEADS, tk, 4 * tq), F32),
                        pltpu.VMEM((KV_HEADS, 1, 4 * tq), F32),
                        pltpu.VMEM((KV_HEADS, LANES + SUM_ROWS, 4 * tq), F32)],
        compiler_params=_cparams(("arbitrary", "arbitrary")),
    )(qn, qr, dtg, kchi, kclo, vct, kaug, vt, kwin, vwt)


SAMPLE_PAGES_PER_STEP = 16
SAMPLE_TQ = 8


def _nsa_sample_kernel(pt_ref, *refs, npages, past, ns, nsp):
    pps = SAMPLE_PAGES_PER_STEP
    tq = SAMPLE_TQ
    pages = refs[:pps]
    (tail_ref, qn_ref, qr_ref, dtg_ref, w1_ref, pe_ref, b1_ref, w2_ref, ov_ref, ex_ref, cwin_ref, wtail_ref,
     y_ref, tsc, xs, ps, s_all, vt_all, bias_sc) = refs[pps:]
    j = pl.program_id(1)
    ngroups = npages // pps
    hb = PAGE // CMP_STRIDE
    scale = HEAD_DIM ** -0.5
    qr_aug = jnp.concatenate([_group_rows(qr_ref[0] * scale, g, tq) for g in range(KV_HEADS)], axis=0).astype(BF16)

    for i, pg in enumerate(pages):
        for kv in range(2):
            tsc[i, kv] = pg[0, 0, kv * LANES:(kv + 1) * LANES, :].T
            for pos in range(CMP_STRIDE):
                xs[kv, i * hb:(i + 1) * hb, pos * LANES:(pos + 1) * LANES] = tsc[i, kv, pl.ds(pos, hb, stride=CMP_STRIDE), :]
        page_idx = j * pps + i
        s_all[page_idx] = _dot(qr_aug, pg[0, 0, 256:384, :].astype(BF16))
        vt_all[page_idx] = pg[0, 0, 384:512, :].astype(BF16)
    rows = pps * hb
    g0 = pl.multiple_of(j * rows, rows)
    for kv in range(2):
        x = xs[kv]
        for s in range(2):
            ps[kv, s, pl.ds(g0, rows), :] = _dot((x + pe_ref[kv, s:s + 1, :]).astype(BF16), w1_ref[kv, s])

    @pl.when(j == ngroups - 1)
    def _():
        nh = npages * hb
        kcvc = []
        for kv in range(2):
            hid = b1_ref[kv:kv + 1, :] + ps[kv, 0] + pltpu.roll(ps[kv, 1], nh - 1, 0)
            kcvc.append(_dot(jax.nn.gelu(hid).astype(BF16), w2_ref[kv]))
        kc, vc = kcvc
        qpos = past + lax.broadcasted_iota(jnp.int32, (tq, 1), 0)
        pos_r = jnp.concatenate([qpos] * (4 * KV_HEADS), axis=0)
        nrow = 4 * KV_HEADS * tq
        s_all[npages] = _dot(qr_aug, tail_ref[0, 256:384, :].astype(BF16))
        vt_all[npages] = tail_ref[0, 384:512, :].astype(BF16)

        qn_aug = jnp.concatenate([_group_rows(qn_ref[0] * scale, g, tq) for g in range(KV_HEADS)], axis=0)
        c_end = lax.broadcasted_iota(jnp.int32, (1, nh), 1) * CMP_STRIDE + (CMP_BLOCK - 1)
        p_c = _masked_softmax(_dot_nt_hi(qn_aug, kc), c_end <= pos_r)
        o_c = _dot(p_c.astype(BF16), vc.astype(BF16))
        slab_pages = SLAB_BLOCKS * SLC_BLOCK // PAGE
        for g in range(KV_HEADS):
            base = 4 * tq * g
            p4 = (p_c[base:base + tq] + p_c[base + tq:base + 2 * tq]
                  + p_c[base + 2 * tq:base + 3 * tq] + p_c[base + 3 * tq:base + 4 * tq])
            selb = _topk_bias(_importance(p4, ov_ref[...]), qpos, ns)
            for sl in range(nsp // LANES):
                npg = min(slab_pages, npages + 1 - slab_pages * sl)
                bias = _dot(selb[:, sl * LANES:(sl + 1) * LANES].astype(BF16), ex_ref[:, 0:npg * PAGE])
                for pp in range(npg):
                    bias_sc[slab_pages * sl + pp, g * tq:(g + 1) * tq, :] = bias[:, pp * PAGE:(pp + 1) * PAGE]

        def biased(pidx):
            b = bias_sc[pidx]
            b_r = jnp.concatenate([b[0:tq]] * 4 + [b[tq:2 * tq]] * 4, axis=0)
            return s_all[pidx] + b_r

        def p1(pidx, m):
            s = biased(pidx)
            s_all[pidx] = s
            return jnp.maximum(m, s)

        m = lax.fori_loop(0, npages, p1, jnp.full((nrow, LANES), 2.0 * NEG, F32))
        kpos_t = past + lax.broadcasted_iota(jnp.int32, (1, PAGE), 1)
        s_t = jnp.where(kpos_t <= pos_r, biased(npages), NEG)
        s_all[npages] = s_t
        mrow = jnp.max(jnp.maximum(m, s_t), axis=-1, keepdims=True)

        def p2(pidx, carry):
            l, acc = carry
            p = jnp.exp(s_all[pidx] - mrow)
            return l + p, acc + _dot_nt(p.astype(BF16), vt_all[pidx])

        l, acc = lax.fori_loop(0, npages + 1, p2, (jnp.zeros((nrow, LANES), F32), jnp.zeros((nrow, LANES), F32)),
                               unroll=3)
        o_s = acc / jnp.sum(l, axis=-1, keepdims=True)

        wbuf = cwin_ref.shape[3]
        kw_b = cwin_ref[0, 0, 0:128, :].astype(BF16)
        vw_b = cwin_ref[0, 0, 128:256, :].astype(BF16)
        kt_b = wtail_ref[0, 0:128, :].astype(BF16)
        vt_b = wtail_ref[0, 128:256, :].astype(BF16)
        s_w = jnp.concatenate([_dot(qr_aug, kw_b), _dot(qr_aug, kt_b)], axis=1)
        kposw = jnp.concatenate([past - wbuf + lax.broadcasted_iota(jnp.int32, (1, wbuf), 1), kpos_t], axis=1)
        dpos = pos_r - kposw
        p_w = _masked_softmax(s_w, (dpos >= 0) & (dpos < WINDOW) & (kposw >= 0))
        o_w = _dot_nt(p_w[:, 0:wbuf].astype(BF16), vw_b) + _dot_nt(p_w[:, wbuf:].astype(BF16), vt_b)

        sg = jax.nn.sigmoid(dtg_ref[0])
        outs = []
        for g in range(KV_HEADS):
            sl = slice(4 * tq * g, 4 * tq * (g + 1))
            outs.append(_gate_cols(sg, g, 0, tq) * o_c[sl] + _gate_cols(sg, g, 1, tq) * o_s[sl]
                        + _gate_cols(sg, g, 2, tq) * o_w[sl])
        _emit_heads(outs, tq, y_ref)


def _nsa_sample(layer, cache_t, table, tail_t, qn, qr, dtg, w1e, pee, b1e, w2e, ov, ex, cwin_t, wtail_t, *, past, ns):
    b, npages = table.shape
    pps = SAMPLE_PAGES_PER_STEP
    tq = SAMPLE_TQ
    nsp = ov.shape[1]
    nh = npages * (PAGE // CMP_STRIDE)
    nrow = 4 * KV_HEADS * tq
    kern = functools.partial(_nsa_sample_kernel, npages=npages, past=past, ns=ns, nsp=nsp)
    const = lambda arr: pl.BlockSpec(arr.shape, lambda i, j, pt: (0,) * arr.ndim, pipeline_mode=pl.Buffered(1))
    perb = lambda arr: pl.BlockSpec((1,) + arr.shape[1:], lambda i, j, pt: (i,) + (0,) * (arr.ndim - 1))

    def page_spec(k):
        return pl.BlockSpec((1, 1, 512, PAGE), lambda i, j, pt: (layer, pt[i, j * pps + k], 0, 0))

    return pl.pallas_call(
        kern,
        grid_spec=pltpu.PrefetchScalarGridSpec(
            num_scalar_prefetch=1,
            grid=(b, npages // pps),
            in_specs=[page_spec(k) for k in range(pps)] + [
                perb(tail_t), perb(qn), perb(qr), perb(dtg), const(w1e), const(pee), const(b1e), const(w2e),
                const(ov), const(ex),
                pl.BlockSpec((1, 1) + cwin_t.shape[2:], lambda i, j, pt: (layer, i, 0, 0)), perb(wtail_t)],
            out_specs=pl.BlockSpec((1, tq, 512), lambda i, j, pt: (i, 0, 0)),
            scratch_shapes=[pltpu.VMEM((pps, 2, PAGE, LANES), F32),
                            pltpu.VMEM((2, pps * 8, CMP_STRIDE * LANES), F32),
                            pltpu.VMEM((2, 2, nh, 256), F32),
                            pltpu.VMEM((npages + 1, nrow, LANES), F32),
                            pltpu.VMEM((npages + 1, LANES, PAGE), BF16),
                            pltpu.VMEM((npages + 1, KV_HEADS * tq, LANES), F32)]),
        out_shape=jax.ShapeDtypeStruct((b, tq, 512), F32),
        compiler_params=_cparams(("arbitrary", "arbitrary")),
    )(table, *([cache_t] * pps), tail_t, qn, qr, dtg, w1e, pee, b1e, w2e, ov, ex, cwin_t, wtail_t)


def _oproj_kernel(x_ref, ys_ref, yn_ref, w_ref, g_ref, r_ref, x1_ref, h_ref, lg_ref):
    a = jnp.concatenate([ys_ref[...], yn_ref[...]], axis=1).astype(BF16)
    x1 = x_ref[...] + _dot(a, w_ref[...])
    x1_ref[...] = x1
    h = _rms(x1, g_ref[...])
    h_ref[...] = h.astype(BF16)
    lg_ref[...] = _dot_hi(h, r_ref[...])


def _oproj(x, ys, yn, w, g, router, tm):
    t = x.shape[0]
    row = lambda n: pl.BlockSpec((tm, n), lambda i: (i, 0))
    full = lambda a: pl.BlockSpec(a.shape, lambda i: (0,) * a.ndim)
    return pl.pallas_call(
        _oproj_kernel,
        grid=(t // tm,),
        in_specs=[row(D_MODEL), row(512), row(512), full(w), full(g), full(router)],
        out_specs=[row(D_MODEL), row(D_MODEL), row(LANES)],
        out_shape=[jax.ShapeDtypeStruct((t, D_MODEL), F32), jax.ShapeDtypeStruct((t, D_MODEL), BF16),
                   jax.ShapeDtypeStruct((t, LANES), F32)],
        compiler_params=_cparams(("arbitrary",)),
    )(x, ys, yn, w, g, router)


def _ffn_kernel(h_ref, x1_ref, wg_ref, wu_ref, wd_ref, o_ref):
    f = pl.program_id(1)

    @pl.when(f == 0)
    def _():
        o_ref[...] = x1_ref[...]

    hb = h_ref[...]
    act = _silu(_dot(hb, wg_ref[...])) * _dot(hb, wu_ref[...])
    o_ref[...] += _dot(act.astype(BF16), wd_ref[...])


def _ffn(h, x1, wg, wu, wd, tm, tf):
    t = h.shape[0]
    dff = wg.shape[1]
    return pl.pallas_call(
        _ffn_kernel,
        grid=(t // tm, dff // tf),
        in_specs=[pl.BlockSpec((tm, D_MODEL), lambda i, f: (i, 0)),
                  pl.BlockSpec((tm, D_MODEL), lambda i, f: (i, 0)),
                  pl.BlockSpec((D_MODEL, tf), lambda i, f: (0, f)),
                  pl.BlockSpec((D_MODEL, tf), lambda i, f: (0, f)),
                  pl.BlockSpec((tf, D_MODEL), lambda i, f: (f, 0))],
        out_specs=pl.BlockSpec((tm, D_MODEL), lambda i, f: (i, 0)),
        out_shape=jax.ShapeDtypeStruct((t, D_MODEL), F32),
        compiler_params=_cparams(("arbitrary", "arbitrary")),
    )(h, x1, wg, wu, wd)


def _moe_kernel(be_ref, nb_ref, xb_ref, wg_ref, wu_ref, wd_ref, o_ref):
    b = pl.program_id(0)
    f = pl.program_id(1)

    @pl.when(f == 0)
    def _():
        o_ref[...] = jnp.zeros(o_ref.shape, F32)

    @pl.when(b < nb_ref[0])
    def _():
        hb = xb_ref[...]
        act = _silu(_dot(hb, wg_ref[0])) * _dot(hb, wu_ref[0])
        o_ref[...] += _dot(act.astype(BF16), wd_ref[0])


def _moe(blk_e, nblk_used, xb, wg, wu, wd, tmb, tf):
    n_blk = xb.shape[0] // tmb
    dex = wg.shape[2]
    nf = dex // tf

    def feff(b, f, nb):
        return jnp.where(b < nb[0], f, nf - 1)

    return pl.pallas_call(
        _moe_kernel,
        grid_spec=pltpu.PrefetchScalarGridSpec(
            num_scalar_prefetch=2,
            grid=(n_blk, nf),
            in_specs=[pl.BlockSpec((tmb, D_MODEL), lambda b, f, be, nb: (b, 0)),
                      pl.BlockSpec((1, D_MODEL, tf), lambda b, f, be, nb: (be[b], 0, feff(b, f, nb))),
                      pl.BlockSpec((1, D_MODEL, tf), lambda b, f, be, nb: (be[b], 0, feff(b, f, nb))),
                      pl.BlockSpec((1, tf, D_MODEL), lambda b, f, be, nb: (be[b], feff(b, f, nb), 0))],
            out_specs=pl.BlockSpec((tmb, D_MODEL), lambda b, f, be, nb: (b, 0))),
        out_shape=jax.ShapeDtypeStruct((n_blk * tmb, D_MODEL), F32),
        compiler_params=_cparams(("arbitrary", "arbitrary")),
    )(blk_e, nblk_used, xb, wg, wu, wd)


def _moe_layer(h, x1, logits, wg, wu, wd, tmb, tf):
    t = h.shape[0]
    top_v, top_e = lax.top_k(logits[:, :N_EXPERTS], TOP_K)
    gate = jax.nn.softmax(top_v, axis=-1)
    e_flat = top_e.reshape(-1)
    tok = jnp.repeat(jnp.arange(t, dtype=jnp.int32), TOP_K)
    order = jnp.argsort(e_flat)
    e_s = e_flat[order]
    tok_s = tok[order]
    counts = jnp.bincount(e_flat, length=N_EXPERTS)
    starts = jnp.cumsum(counts) - counts
    padded = (counts + tmb - 1) // tmb * tmb
    pends = jnp.cumsum(padded)
    pstarts = pends - padded
    dest_s = (pstarts[e_s] + jnp.arange(t * TOP_K, dtype=jnp.int32) - starts[e_s]).astype(jnp.int32)
    n_blk = -(-(t * TOP_K) // tmb) + N_EXPERTS
    buf_tok = jnp.full((n_blk * tmb,), t, dtype=jnp.int32).at[dest_s].set(tok_s)
    blk_e = jnp.minimum(jnp.searchsorted(pends, jnp.arange(n_blk, dtype=jnp.int32) * tmb, side='right'),
                        N_EXPERTS - 1).astype(jnp.int32)
    nblk_used = (pends[-1] // tmb).astype(jnp.int32).reshape(1)
    h_pad = jnp.concatenate([h, jnp.zeros((1, D_MODEL), h.dtype)], axis=0)
    xb = h_pad[buf_tok]
    out = _moe(blk_e, nblk_used, xb, wg, wu, wd, tmb, tf)
    dest = jnp.zeros((t * TOP_K,), jnp.int32).at[order].set(dest_s).reshape(t, TOP_K)
    return x1 + out[dest[:, 0]] * gate[:, 0:1] + out[dest[:, 1]] * gate[:, 1:2]


def _final_kernel(x_ref, g_ref, o_ref):
    o_ref[...] = _rms(x_ref[...], g_ref[...])


def _final_norm(x, g, tm):
    t = x.shape[0]
    return pl.pallas_call(
        _final_kernel,
        grid=(t // tm,),
        in_specs=[pl.BlockSpec((tm, D_MODEL), lambda i: (i, 0)), pl.BlockSpec(g.shape, lambda i: (0, 0))],
        out_specs=pl.BlockSpec((tm, D_MODEL), lambda i: (i, 0)),
        out_shape=jax.ShapeDtypeStruct((t, D_MODEL), F32),
        compiler_params=_cparams(("arbitrary",)),
    )(x, g)


def _rope_tables(pos):
    inv = 1.0 / (ROPE_THETA ** (jnp.arange(0, HEAD_DIM, 2, dtype=F32) / HEAD_DIM))
    ang = pos.astype(F32)[:, None] * inv[None, :]
    cos = jnp.cos(ang)
    sin = jnp.sin(ang)
    return jnp.concatenate([cos, cos, cos, cos], axis=1), jnp.concatenate([-sin, sin, -sin, sin], axis=1)


def _overlap(ncb, nsp, ns):
    c = jnp.arange(ncb, dtype=jnp.int32)[:, None] * CMP_STRIDE
    s = jnp.arange(nsp, dtype=jnp.int32)[None, :] * SLC_BLOCK
    ov = (c <= s + SLC_BLOCK - 1) & (c + CMP_BLOCK - 1 >= s) & (jnp.arange(nsp)[None, :] < ns)
    return ov.astype(BF16)


def _pad_lanes(v, n):
    return jnp.zeros((1, n), F32).at[0, :v.shape[0]].set(v.astype(F32))


def kernel(x_prompt, x_sample, cache_kv, cache_win, state_ssm, state_conv, page_table, ln_mix, w_in, conv_w,
           conv_b, dt_bias, a_log, d_skip, ssd_norm, cmp_pos, cmp_w1, cmp_b1, cmp_w2, w_out, ln_ffn,
           ffn_w_gate, ffn_w_up, ffn_w_down, moe_router, moe_w_gate, moe_w_up, moe_w_down, ln_final):
    depth = w_in.shape[0]
    bsz, seq = x_prompt.shape[:2]
    dbsz, dseq = x_sample.shape[:2]
    n_pages = page_table.shape[1]
    past = n_pages * cache_kv.shape[2]
    win_buf = cache_win.shape[2]
    npool = cache_kv.shape[1]
    assert bsz == 1 and cache_kv.shape[2] == PAGE and seq % PAGE == 0 and seq >= WINDOW + PAGE
    assert win_buf == WINDOW and past % PAGE == 0
    assert not any(past <= CMP_STRIDE * c + CMP_BLOCK - 1 <= past + dseq - 1
                   for c in range(past // CMP_STRIDE - 2, past // CMP_STRIDE + 2))

    ts = dbsz * dseq
    ns_p = seq // SLC_BLOCK
    ns_s = -(-(past + dseq) // SLC_BLOCK)
    nsp_p = -(-ns_p // LANES) * LANES
    nsp_s = -(-ns_s // LANES) * LANES
    assert dseq <= SAMPLE_TQ and n_pages % SAMPLE_PAGES_PER_STEP == 0
    slab_pages = SLAB_BLOCKS * SLC_BLOCK // PAGE
    assert (nsp_s // LANES - 1) * slab_pages < n_pages + 1 <= (nsp_s // LANES) * slab_pages
    cache_t = jnp.transpose(cache_kv, (0, 1, 3, 4, 5, 2)).reshape(depth, npool, 512, PAGE)
    cwin_t = jnp.transpose(cache_win, (0, 1, 3, 4, 5, 2)).reshape(depth, dbsz, 256, win_buf)
    ex = (jnp.arange(SLAB_BLOCKS * SLC_BLOCK, dtype=jnp.int32)[None, :] // SLC_BLOCK
          == jnp.arange(SLAB_BLOCKS, dtype=jnp.int32)[:, None]).astype(BF16)

    cs_p, sn_p = _rope_tables(jnp.arange(seq, dtype=jnp.int32))
    cs_s, sn_s = _rope_tables(past + jnp.arange(ts, dtype=jnp.int32) % dseq)
    ov_s = _overlap(past // CMP_STRIDE, nsp_s, ns_s)
    ident = jnp.arange(seq // PAGE, dtype=jnp.int32).reshape(1, -1)

    xp = x_prompt.reshape(seq, D_MODEL)
    xs = x_sample.reshape(ts, D_MODEL)
    outs = {k: [] for k in ('kv_p', 'kv_s', 'win_p', 'win_s', 'ssm_p', 'ssm_s', 'conv_p', 'conv_s')}

    for l in range(depth):
        w = w_in[l]
        wr = jnp.concatenate([w[:, :1536], w[:, 1544:2824], w[:, 1536:1544], w[:, 2824:2848],
                              jnp.zeros((D_MODEL, C_END - 2848), F32)], axis=1).astype(BF16)
        g_mix = ln_mix[l].reshape(1, -1)
        cw8 = jnp.zeros((8, CONV_DIM), F32).at[:CONV_W].set(conv_w[l])
        cb = conv_b[l].reshape(1, -1)
        dtb = _pad_lanes(dt_bias[l], LANES)
        a_neg = _pad_lanes(-jnp.exp(a_log[l].astype(F32)), LANES)
        dsk = jnp.repeat(d_skip[l].astype(F32), SSD_HEAD_DIM).reshape(1, -1)
        ng = ssd_norm[l].reshape(1, -1)
        w1 = cmp_w1[l].reshape(2, 2, CMP_STRIDE, HEAD_DIM, CMP_HIDDEN)
        eye = jnp.eye(KV_HEADS, dtype=F32)
        w1e = jnp.einsum('vspdh,gk->vspgdkh', w1, eye).reshape(2, 2, CMP_STRIDE * LANES, 256).astype(BF16)
        pe = cmp_pos[l].reshape(2, 2, CMP_STRIDE, 1, HEAD_DIM)
        pee = jnp.broadcast_to(pe, (2, 2, CMP_STRIDE, KV_HEADS, HEAD_DIM)).reshape(2, 2, CMP_STRIDE * LANES)
        b1e = jnp.concatenate([cmp_b1[l], cmp_b1[l]], axis=1)
        w2e = jnp.einsum('vhd,gk->vghkd', cmp_w2[l], eye).reshape(2, 256, LANES).astype(BF16)
        wo = w_out[l].astype(BF16)
        g_ffn = ln_ffn[l].reshape(1, -1)
        if l % 2 == 0:
            router = jnp.zeros((D_MODEL, LANES), F32)
        else:
            router = jnp.zeros((D_MODEL, LANES), F32).at[:, :N_EXPERTS].set(moe_router[l // 2])

        z, xbc, dtg, q, qr, kv, win = _front(xp, g_mix, wr, cs_p, sn_p, 256)
        y_ssd, hist, hfin = _ssd(z[None], xbc[None], dtg[None], jnp.zeros((1, 8, CONV_DIM), F32),
                                 jnp.zeros((1, D_SSD, SSD_STATE), F32), cw8, cb, dtb, a_neg, dsk, ng, SSD_CHUNK)
        kv_pages = kv.reshape(seq // PAGE, PAGE, 512)
        kchi, kclo, vct = _compress(kv_pages, ident, w1e, pee, b1e, w2e)
        kaug, vt, kwin, vwt = _kvfmt(kv, win, NSA_TK)
        y_nsa = _nsa(q[None], qr[None], dtg[None], kchi, kclo, vct, kaug, vt, kwin, vwt,
                     tq=128, ns=ns_p, nsp=nsp_p, tk=NSA_TK, wl=WINDOW + 128)
        x1, h, logits = _oproj(xp, y_ssd[0], y_nsa[0], wo, g_ffn, router, 512)
        outs['kv_p'].append(kv.reshape(1, seq, 4, KV_HEADS, HEAD_DIM))
        outs['win_p'].append(win[seq - WINDOW:].reshape(1, WINDOW, 2, KV_HEADS, HEAD_DIM))
        outs['ssm_p'].append(hfin.reshape(1, SSD_HEADS, SSD_HEAD_DIM, SSD_STATE))
        outs['conv_p'].append(hist[:, 5:8])

        zs, xbcs, dtgs, qs, qrs, kvs, wins = _front(xs, g_mix, wr, cs_s, sn_s, ts)
        padl = lambda v: jnp.pad(v.reshape(dbsz, dseq, -1), ((0, 0), (0, SSD_CHUNK - dseq), (0, 0)))
        hist8 = jnp.pad(state_conv[l], ((0, 0), (8 - (CONV_W - 1), 0), (0, 0)))
        y_ssd_s, hist_s, hfin_s = _ssd(padl(zs), padl(xbcs), padl(dtgs), hist8,
                                       state_ssm[l].reshape(dbsz, D_SSD, SSD_STATE),
                                       cw8, cb, dtb, a_neg, dsk, ng, dseq)
        tail_t = jnp.pad(jnp.transpose(kvs.reshape(dbsz, dseq, 512), (0, 2, 1)), ((0, 0), (0, 0), (0, PAGE - dseq)))
        wtail_t = jnp.pad(jnp.transpose(wins.reshape(dbsz, dseq, 256), (0, 2, 1)), ((0, 0), (0, 0), (0, PAGE - dseq)))
        padq = lambda v: jnp.pad(v.reshape(dbsz, dseq, -1), ((0, 0), (0, SAMPLE_TQ - dseq), (0, 0)))
        y_nsa_s = _nsa_sample(l, cache_t, page_table, tail_t, padq(qs), padq(qrs), padq(dtgs), w1e, pee, b1e, w2e,
                              ov_s, ex, cwin_t, wtail_t, past=past, ns=ns_s)
        y_nsa_s = y_nsa_s[:, :dseq].reshape(ts, D_NSA)
        x1s, hs, logits_s = _oproj(xs, y_ssd_s[:, :dseq].reshape(ts, D_SSD), y_nsa_s, wo, g_ffn, router, ts)
        outs['kv_s'].append(kvs.reshape(dbsz, dseq, 4, KV_HEADS, HEAD_DIM))
        win_all = jnp.concatenate([cache_win[l], wins.reshape(dbsz, dseq, 2, KV_HEADS, HEAD_DIM)], axis=1)
        outs['win_s'].append(win_all[:, dseq:])
        outs['ssm_s'].append(hfin_s.reshape(dbsz, SSD_HEADS, SSD_HEAD_DIM, SSD_STATE))
        outs['conv_s'].append(hist_s[:, 5:8])

        if l % 2 == 0:
            wg = ffn_w_gate[l // 2].astype(BF16)
            wu = ffn_w_up[l // 2].astype(BF16)
            wd = ffn_w_down[l // 2].astype(BF16)
            xp = _ffn(h, x1, wg, wu, wd, 1024, 256)
            xs = _ffn(hs, x1s, wg, wu, wd, ts, 256)
        else:
            wg = moe_w_gate[l // 2].astype(BF16)
            wu = moe_w_up[l // 2].astype(BF16)
            wd = moe_w_down[l // 2].astype(BF16)
            xp = _moe_layer(h, x1, logits, wg, wu, wd, 1024, 512)
            xs = _moe_layer(hs, x1s, logits_s, wg, wu, wd, 128, 512)

    g_fin = ln_final.reshape(1, -1)
    y_prompt = _final_norm(xp, g_fin, 1024).reshape(bsz, seq, D_MODEL)
    y_sample = _final_norm(xs, g_fin, ts).reshape(dbsz, dseq, D_MODEL)
    st = lambda k: jnp.stack(outs[k])
    return (y_prompt, y_sample, st('kv_p'), st('kv_s'), st('win_p'), st('win_s'),
            st('ssm_p'), st('ssm_s'), st('conv_p'), st('conv_s'))
```

```python
import functools
import math

import jax
import jax.numpy as jnp
from jax import lax
from jax.experimental import pallas as pl
from jax.experimental.pallas import tpu as pltpu

F32 = jnp.float32
BF16 = jnp.bfloat16

D_MODEL = 1024
SSD_HEADS = 8
SSD_HEAD_DIM = 64
D_SSD = 512
SSD_GROUPS = 2
SSD_STATE = 128
CONV_W = 4
CONV_DIM = 1024
SSD_CHUNK = 128
NSA_HEADS = 8
HEAD_DIM = 64
KV_HEADS = 2
D_NSA = 512
CMP_BLOCK = 32
CMP_STRIDE = 16
CMP_HIDDEN = 128
SLC_BLOCK = 64
TOP_N = 16
WINDOW = 512
ROPE_THETA = 10000.0
FORCE_BONUS = 1.0e4
N_EXPERTS = 8
TOP_K = 2
EPS = 1e-6

PAGE = 128
LANES = 128
LOG2E = 1.4426950408889634
NEG = -1.0e30
SLAB_BLOCKS = 128
VMEM_LIMIT = 56 * 1024 * 1024

C_Z, C_XBC, C_Q, C_KV, C_DT, C_END = 0, 512, 1536, 2048, 2816, 2944


def _cparams(sem):
    return pltpu.CompilerParams(dimension_semantics=sem, vmem_limit_bytes=VMEM_LIMIT)


def _dot(a, b):
    return jnp.dot(a, b, preferred_element_type=F32)


def _dot_nt(a, b):
    return lax.dot_general(a, b, (((1,), (1,)), ((), ())), preferred_element_type=F32)


def _split3(x):
    h1 = x.astype(BF16)
    r1 = x - h1.astype(F32)
    h2 = r1.astype(BF16)
    h3 = (r1 - h2.astype(F32)).astype(BF16)
    return h1, h2, h3


def _dot_nt_hi(a, b):
    a1, a2, _ = _split3(a)
    b1, b2, _ = _split3(b)
    return _dot_nt(a1, b1) + (_dot_nt(a1, b2) + _dot_nt(a2, b1))


def _dot_hi(a, b):
    a1, a2, _ = _split3(a)
    b1, b2, _ = _split3(b)
    return _dot(a1, b1) + (_dot(a1, b2) + _dot(a2, b1))


def _silu(x):
    return x * jax.nn.sigmoid(x)


def _rms(x, g):
    return x * lax.rsqrt(jnp.mean(x * x, axis=-1, keepdims=True) + EPS) * g


def _front_kernel(x_ref, g_ref, w_ref, cs_ref, sn_ref,
                  z_ref, xbc_ref, dtg_ref, q_ref, qr_ref, kv_ref, win_ref):
    xn = _rms(x_ref[...], g_ref[...])
    p = _dot(xn.astype(BF16), w_ref[...])
    cs = cs_ref[...]
    sn = sn_ref[...]
    lane = lax.broadcasted_iota(jnp.int32, cs.shape, 1)
    first = (lane % HEAD_DIM) < (HEAD_DIM // 2)

    def rope(v):
        sw = jnp.where(first, pltpu.roll(v, LANES - HEAD_DIM // 2, 1), pltpu.roll(v, HEAD_DIM // 2, 1))
        return v * cs + sw * sn

    z_ref[...] = p[:, C_Z:C_XBC]
    xbc_ref[...] = p[:, C_XBC:C_Q]
    dtg_ref[...] = p[:, C_DT:C_END]
    q_ref[...] = p[:, C_Q:C_KV]
    for s in range(4):
        qr_ref[:, s * LANES:(s + 1) * LANES] = rope(p[:, C_Q + s * LANES:C_Q + (s + 1) * LANES])
    kv_ref[:, 0:256] = p[:, C_KV:C_KV + 256]
    kv_ref[:, 256:384] = rope(p[:, C_KV + 256:C_KV + 384])
    kv_ref[:, 384:512] = p[:, C_KV + 384:C_KV + 512]
    win_ref[:, 0:128] = rope(p[:, C_KV + 512:C_KV + 640])
    win_ref[:, 128:256] = p[:, C_KV + 640:C_KV + 768]


def _front(x, g, w, cs, sn, tm):
    t = x.shape[0]
    row = lambda n: pl.BlockSpec((tm, n), lambda i: (i, 0))
    full = lambda a: pl.BlockSpec(a.shape, lambda i: (0,) * a.ndim)
    outs = [512, 1024, 128, 512, 512, 512, 256]
    return pl.pallas_call(
        _front_kernel,
        grid=(t // tm,),
        in_specs=[row(D_MODEL), full(g), full(w), row(LANES), row(LANES)],
        out_specs=[row(n) for n in outs],
        out_shape=[jax.ShapeDtypeStruct((t, n), F32) for n in outs],
        compiler_params=_cparams(("arbitrary",)),
    )(x, g, w, cs, sn)


def _expand_heads(v):
    r = v.shape[0]
    lane = lax.broadcasted_iota(jnp.int32, (r, LANES), 1)
    outs = []
    for k in range(4):
        a = jnp.broadcast_to(v[:, 2 * k:2 * k + 1], (r, LANES))
        b = jnp.broadcast_to(v[:, 2 * k + 1:2 * k + 2], (r, LANES))
        outs.append(jnp.where(lane < SSD_HEAD_DIM, a, b))
    return jnp.concatenate(outs, axis=1)


def _ssd_kernel(z_ref, xbc_ref, dtg_ref, hist_ref, h0_ref, cw_ref, cb_ref, dtb_ref, a_ref, dsk_ref, ng_ref,
                y_ref, hist_o_ref, hfin_ref, xpad, ht, *, nchunks, valid_last):
    c = pl.program_id(1)
    q = SSD_CHUNK

    @pl.when(c == 0)
    def _():
        xpad[0:8, :] = hist_ref[0]
        ht[...] = h0_ref[0].T

    xpad[8:8 + q, :] = xbc_ref[0]
    conv = cb_ref[...]
    for k in range(CONV_W):
        conv = conv + xpad[5 + k:5 + k + q, :] * cw_ref[k:k + 1, :]
    xc = _silu(conv)
    xs = xc[:, 0:D_SSD]
    bm = xc[:, D_SSD:D_SSD + 256]
    cm = xc[:, D_SSD + 256:D_SSD + 512]

    row = lax.broadcasted_iota(jnp.int32, (q, q), 0)
    col = lax.broadcasted_iota(jnp.int32, (q, q), 1)
    nvalid = jnp.where(c == nchunks - 1, valid_last, q)
    xdt_in = dtg_ref[0] + dtb_ref[...]
    dt = jnp.maximum(xdt_in, 0.0) + jnp.log1p(jnp.exp(-jnp.abs(xdt_in)))
    dt = jnp.where(row < nvalid, dt, 0.0)
    dta = dt * a_ref[...]
    causal = row >= col
    cum = jnp.dot(causal.astype(F32), dta, preferred_element_type=F32, precision=lax.Precision.HIGHEST)
    cum_t = cum.T
    ecum = jnp.exp(cum)
    toend = jnp.exp(cum[q - 1:q, :] - cum)
    dt_e = _expand_heads(dt)
    ecum_e = _expand_heads(ecum)
    toend_e = _expand_heads(toend)
    xdt = xs * dt_e
    xdt_b = xdt.astype(BF16)
    xw_b = (xdt * toend_e).astype(BF16)
    lane = lax.broadcasted_iota(jnp.int32, (q, LANES), 1)

    y_intra = []
    y_inter = []
    for g in range(SSD_GROUPS):
        bg = bm[:, g * 128:(g + 1) * 128]
        cg_b = cm[:, g * 128:(g + 1) * 128].astype(BF16)
        bg_b = bg.astype(BF16)
        cb = _dot_nt(cg_b, bg_b)
        htg = ht[:, g * 256:(g + 1) * 256]
        y_inter.append(_dot(cg_b, htg.astype(BF16)))
        for kk in range(2):
            slab = xdt_b[:, (2 * g + kk) * LANES:(2 * g + kk + 1) * LANES]
            res = []
            for hh in range(2):
                h = 4 * g + 2 * kk + hh
                seg = cum[:, h:h + 1] - cum_t[h:h + 1, :]
                decay = jnp.exp(jnp.where(causal, seg, NEG))
                res.append(_dot((decay * cb).astype(BF16), slab))
            y_intra.append(jnp.where(lane < SSD_HEAD_DIM, res[0], res[1]))
        s_new = _dot(bg.T.astype(BF16), xw_b[:, g * 256:(g + 1) * 256])
        ht[:, g * 256:(g + 1) * 256] = htg * ecum_e[q - 1:q, g * 256:(g + 1) * 256] + s_new

    y = jnp.concatenate(y_intra, axis=1) + jnp.concatenate(y_inter, axis=1) * ecum_e + dsk_ref[...] * xs
    y = y * _silu(z_ref[0])
    y_ref[0] = _rms(y, ng_ref[...])

    xpad[0:8, :] = xpad[q:q + 8, :]

    @pl.when(c == nchunks - 1)
    def _():
        hist_o_ref[0] = xpad[valid_last:valid_last + 8, :]
        hfin_ref[0] = ht[...].T


def _ssd(z, xbc, dtg, hist8, h0, cw8, cb, dtb, a, dsk, ng, valid_last):
    b, l = z.shape[:2]
    nchunks = l // SSD_CHUNK
    rows = lambda n: pl.BlockSpec((1, SSD_CHUNK, n), lambda i, c: (i, c, 0))
    perb = lambda r, n: pl.BlockSpec((1, r, n), lambda i, c: (i, 0, 0))
    full = lambda arr: pl.BlockSpec(arr.shape, lambda i, c: (0,) * arr.ndim)
    kern = functools.partial(_ssd_kernel, nchunks=nchunks, valid_last=valid_last)
    return pl.pallas_call(
        kern,
        grid=(b, nchunks),
        in_specs=[rows(512), rows(1024), rows(128), perb(8, 1024), perb(512, 128),
                  full(cw8), full(cb), full(dtb), full(a), full(dsk), full(ng)],
        out_specs=[rows(512), perb(8, 1024), perb(512, 128)],
        out_shape=[jax.ShapeDtypeStruct((b, l, 512), F32),
                   jax.ShapeDtypeStruct((b, 8, 1024), F32),
                   jax.ShapeDtypeStruct((b, 512, 128), F32)],
        scratch_shapes=[pltpu.VMEM((SSD_CHUNK + 8, CONV_DIM), F32), pltpu.VMEM((SSD_STATE, D_SSD), F32)],
        compiler_params=_cparams(("arbitrary", "arbitrary")),
    )(z, xbc, dtg, hist8, h0, cw8, cb, dtb, a, dsk, ng)


SUM_ROWS = 16
NSA_TK = 512
CMP_GROUP_PAGES = 16


def _compress_kernel(pt_ref, pgk_ref, pgv_ref, w1_ref, pe_ref, b1_ref, w2_ref, w2t_ref, kchi_ref, kclo_ref, vct_ref,
                     xs, ps, *, npages):
    p = pl.program_id(1)
    sub = p % CMP_GROUP_PAGES
    hb = PAGE // CMP_STRIDE
    r0 = pl.multiple_of(sub * hb, hb)
    for kv, pg_ref in enumerate((pgk_ref, pgv_ref)):
        for pos in range(CMP_STRIDE):
            xs[kv, pl.ds(r0, hb), pos * LANES:(pos + 1) * LANES] = pg_ref[0, pl.ds(pos, hb, stride=CMP_STRIDE), :]

    @pl.when(sub == CMP_GROUP_PAGES - 1)
    def _():
        rows = CMP_GROUP_PAGES * hb
        g0 = pl.multiple_of((p // CMP_GROUP_PAGES) * rows, rows)
        for kv in range(2):
            x = xs[kv]
            for s in range(2):
                ps[kv, s, pl.ds(g0, rows), :] = _dot((x + pe_ref[kv, s:s + 1, :]).astype(BF16), w1_ref[kv, s])

    @pl.when(p == npages - 1)
    def _():
        nh = npages * hb
        act = [jax.nn.gelu(b1_ref[kv:kv + 1, :] + ps[kv, 0] + pltpu.roll(ps[kv, 1], nh - 1, 0)).astype(BF16)
               for kv in range(2)]
        kc = _dot(act[0], w2_ref[0])
        hi = kc.astype(BF16)
        kchi_ref[0] = hi
        kclo_ref[0] = (kc - hi.astype(F32)).astype(BF16)
        vct_ref[0] = _dot_nt(w2t_ref[...], act[1]).astype(BF16)


def _compress(pages, table, w1e, pee, b1e, w2e):
    b, npages = table.shape
    w2t = w2e[1].T
    nh = npages * (PAGE // CMP_STRIDE)
    kern = functools.partial(_compress_kernel, npages=npages)
    full = lambda arr: pl.BlockSpec(arr.shape, lambda i, p, pt: (0,) * arr.ndim)
    return pl.pallas_call(
        kern,
        grid_spec=pltpu.PrefetchScalarGridSpec(
            num_scalar_prefetch=1,
            grid=(b, npages),
            in_specs=[pl.BlockSpec((1, PAGE, LANES), lambda i, p, pt: (pt[i, p], 0, 0)),
                      pl.BlockSpec((1, PAGE, LANES), lambda i, p, pt: (pt[i, p], 0, 1)),
                      full(w1e), full(pee), full(b1e), full(w2e), full(w2t)],
            out_specs=[pl.BlockSpec((1, nh, LANES), lambda i, p, pt: (i, 0, 0)),
                       pl.BlockSpec((1, nh, LANES), lambda i, p, pt: (i, 0, 0)),
                       pl.BlockSpec((1, LANES, nh), lambda i, p, pt: (i, 0, 0))],
            scratch_shapes=[pltpu.VMEM((2, CMP_GROUP_PAGES * 8, CMP_STRIDE * LANES), F32),
                            pltpu.VMEM((2, 2, nh, 256), F32)]),
        out_shape=[jax.ShapeDtypeStruct((b, nh, LANES), BF16), jax.ShapeDtypeStruct((b, nh, LANES), BF16),
                   jax.ShapeDtypeStruct((b, LANES, nh), BF16)],
        compiler_params=_cparams(("arbitrary", "arbitrary")),
    )(table, pages, pages, w1e, pee, b1e, w2e, w2t)


def _kvfmt_kernel(kv_ref, win_ref, k_ref, vt_ref, kw_ref, vwt_ref):
    p = pl.program_id(0)
    src = kv_ref[...]
    r = lax.broadcasted_iota(jnp.int32, (PAGE, LANES), 0)
    lane = lax.broadcasted_iota(jnp.int32, (PAGE, LANES), 1)
    blk = (2 * p + r // SLC_BLOCK) % SLAB_BLOCKS
    k_ref[:, 0:128] = src[:, 0:128].astype(BF16)
    k_ref[:, 128:256] = jnp.where(lane == blk, 1.0, 0.0).astype(BF16)
    vt_ref[0, 0:LANES, :] = src[:, 128:256].T.astype(BF16)
    vt_ref[0, LANES:LANES + SUM_ROWS, :] = jnp.ones((SUM_ROWS, PAGE), BF16)
    kw_ref[...] = win_ref[:, 0:128].astype(BF16)
    vwt_ref[0] = win_ref[:, 128:256].T.astype(BF16)


def _kvfmt(kv, win, tk):
    l = kv.shape[0]
    per = tk // PAGE
    return pl.pallas_call(
        _kvfmt_kernel,
        grid=(l // PAGE,),
        in_specs=[pl.BlockSpec((PAGE, 256), lambda p: (p, 1)), pl.BlockSpec((PAGE, 256), lambda p: (p, 0))],
        out_specs=[pl.BlockSpec((PAGE, 256), lambda p: (p, 0)),
                   pl.BlockSpec((1, LANES + SUM_ROWS, PAGE), lambda p: (p // per, 0, p % per)),
                   pl.BlockSpec((PAGE, LANES), lambda p: (p, 0)),
                   pl.BlockSpec((1, LANES, PAGE), lambda p: (p, 0, 0))],
        out_shape=[jax.ShapeDtypeStruct((l, 256), BF16),
                   jax.ShapeDtypeStruct((l // tk, LANES + SUM_ROWS, tk), BF16),
                   jax.ShapeDtypeStruct((l, LANES), BF16),
                   jax.ShapeDtypeStruct((l // PAGE, LANES, PAGE), BF16)],
        compiler_params=_cparams(("arbitrary",)),
    )(kv, win)


def _group_rows(qfull, g, tq):
    lane = lax.broadcasted_iota(jnp.int32, (tq, LANES), 1)
    keep = (lane >= HEAD_DIM) if g else (lane < HEAD_DIM)
    pieces = []
    for r in range(4):
        h = 4 * g + r
        slab = qfull[:, (h // 2) * LANES:(h // 2 + 1) * LANES]
        if h % 2 != g:
            slab = pltpu.roll(slab, HEAD_DIM, 1)
        pieces.append(jnp.where(keep, slab, 0.0))
    return jnp.concatenate(pieces, axis=0)


def _masked_softmax(s, mask):
    s = jnp.where(mask, s, NEG)
    m = jnp.max(s, axis=-1, keepdims=True)
    e = jnp.where(mask, jnp.exp(s - m), 0.0)
    return e / jnp.maximum(jnp.sum(e, axis=-1, keepdims=True), 1e-30)


def _importance(p4, ov):
    h1, h2, h3 = _split3(p4)
    return _dot(h1, ov) + (_dot(h2, ov) + _dot(h3, ov))


def _topk_bias(imp, qpos, ns):
    tq, nsp = imp.shape
    jf = lax.broadcasted_iota(jnp.int32, (tq, nsp), 1)
    jff = jf.astype(F32)
    cur = qpos >> 6
    valid = jf <= cur
    forced = valid & ((jf == 0) | (jf == cur) | (jf == cur - 1))
    score = jnp.where(forced, FORCE_BONUS, jnp.where(valid, imp, -1.0))
    score = jnp.where(jf < ns, score, -2.0)

    def pick(_, carry):
        sc, sb = carry
        mx = jnp.max(sc, axis=-1, keepdims=True)
        jm = jnp.min(jnp.where(sc == mx, jff, float(nsp)), axis=-1, keepdims=True)
        hit = jff == jm
        return jnp.where(hit, -3.0, sc), jnp.where(hit, 0.0, sb)

    return lax.fori_loop(0, TOP_N, pick, (score, jnp.full((tq, nsp), NEG, F32)))[1]


def _gate_cols(sg, g, k, tq):
    return jnp.concatenate([sg[:, 8 + 3 * (4 * g + r) + k:8 + 3 * (4 * g + r) + k + 1] for r in range(4)], axis=0)


def _emit_heads(outs, tq, y_ref):
    lane = lax.broadcasted_iota(jnp.int32, (tq, LANES), 1)
    for k in range(4):
        g = k // 2
        r0 = (2 * k) % 4
        a = outs[g][r0 * tq:(r0 + 1) * tq]
        b = outs[g][(r0 + 1) * tq:(r0 + 2) * tq]
        if g == 0:
            slab = jnp.where(lane < HEAD_DIM, a, pltpu.roll(b, HEAD_DIM, 1))
        else:
            slab = jnp.where(lane < HEAD_DIM, pltpu.roll(a, HEAD_DIM, 1), b)
        y_ref[0, :, k * LANES:(k + 1) * LANES] = slab


def _masked_softmax_t(s, mask):
    s = jnp.where(mask, s, NEG)
    m = jnp.max(s, axis=0, keepdims=True)
    e = jnp.exp(s - m)
    inv = jnp.where(m > 0.5 * NEG, 1.0 / jnp.maximum(jnp.sum(e, axis=0, keepdims=True), 1e-30), 0.0)
    return e * inv


def _topk_bias_t(imp, qpos, ns):
    nsp, tq = imp.shape
    jf = lax.broadcasted_iota(jnp.int32, (nsp, tq), 0)
    jff = jf.astype(F32)
    cur = qpos >> 6
    valid = jf <= cur
    forced = valid & ((jf == 0) | (jf == cur) | (jf == cur - 1))
    score = jnp.where(forced, FORCE_BONUS, jnp.where(valid, imp, -1.0))
    score = jnp.where(jf < ns, score, -2.0)

    def pick(_, carry):
        sc, sb = carry
        mx = jnp.max(sc, axis=0, keepdims=True)
        jm = jnp.min(jnp.where(sc == mx, jff, float(nsp)), axis=0, keepdims=True)
        hit = jff == jm
        return jnp.where(hit, -3.0, sc), jnp.where(hit, 0.0, sb)

    return lax.fori_loop(0, TOP_N, pick, (score, jnp.full((nsp, tq), NEG, F32)))[1]


def _nsa_kernel(qn_ref, qr_ref, dtg_ref, kchi_ref, kclo_ref, vct_ref, kaug_ref, vt_ref, kwin_ref, vwt_ref,
                y_ref, qat, p4_sc, sa_sc, sb_sc, m_sc, acc_sc, *, tq, ns, nsp, tk, wl):
    t = pl.program_id(1)
    scale = HEAD_DIM ** -0.5
    r4 = 4 * tq
    nslab = nsp // LANES
    lane4 = lax.broadcasted_iota(jnp.int32, (1, r4), 1)
    posr = t * tq + (lane4 & (tq - 1))
    qpos = t * tq + lax.broadcasted_iota(jnp.int32, (1, tq), 1)

    qn = qn_ref[0] * scale
    qr = qr_ref[0] * scale
    kc_hi = kchi_ref[0]
    kc_lo = kclo_ref[0]
    ncb = kc_hi.shape[0]
    c_end = lax.broadcasted_iota(jnp.int32, (ncb, 1), 0) * CMP_STRIDE + (CMP_BLOCK - 1)
    mask_c = c_end <= posr

    ws = pl.multiple_of(jnp.maximum(t * tq - WINDOW, 0), LANES)
    kw = kwin_ref[pl.ds(ws, wl), :]
    kposw = ws + lax.broadcasted_iota(jnp.int32, (wl, 1), 0)
    dpos = posr - kposw
    mask_w = (dpos >= 0) & (dpos < WINDOW)
    wpage = ws // PAGE

    p4_sc[0:8, :] = jnp.zeros((8, tq), F32)
    p4_sc[8 + ncb:, :] = jnp.zeros((p4_sc.shape[0] - 8 - ncb, tq), F32)
    o_cmp = []
    o_win = []
    for g in range(KV_HEADS):
        qn_t = _group_rows(qn, g, tq).T
        qr_t = _group_rows(qr, g, tq).T
        qh = qn_t.astype(BF16)
        ql = (qn_t - qh.astype(F32)).astype(BF16)
        s_c = _dot(kc_hi, qh) + (_dot(kc_hi, ql) + _dot(kc_lo, qh))
        p_c = _masked_softmax_t(s_c, mask_c)
        o_cmp.append(_dot(vct_ref[0], p_c.astype(BF16)))
        p4 = p_c[:, 0:tq] + p_c[:, tq:2 * tq] + p_c[:, 2 * tq:3 * tq] + p_c[:, 3 * tq:4 * tq]
        p4_sc[8:8 + ncb, :] = p4
        per = SLC_BLOCK // CMP_STRIDE
        imp = p4_sc[pl.ds(8 - 1, nsp, stride=per), :]
        for k in range(per):
            imp = imp + p4_sc[pl.ds(8 + k, nsp, stride=per), :]
        selb = _topk_bias_t(imp, qpos, ns)
        qs_b = (qr_t * LOG2E).astype(BF16)
        for sl in range(nslab):
            piece = selb[sl * LANES:(sl + 1) * LANES, :]
            qat[sl, g, 0:LANES, :] = qs_b
            qat[sl, g, LANES:2 * LANES, :] = jnp.concatenate([piece] * 4, axis=1).astype(BF16)
        p_w = _masked_softmax_t(_dot(kw, qr_t.astype(BF16)), mask_w).astype(BF16)
        o_w = _dot(vwt_ref[wpage], p_w[0:PAGE])
        for i in range(1, wl // PAGE):
            o_w = o_w + _dot(vwt_ref[wpage + i], p_w[i * PAGE:(i + 1) * PAGE])
        o_win.append(o_w)

    m_sc[...] = jnp.full(m_sc.shape, 2.0 * NEG, F32)
    acc_sc[...] = jnp.zeros(acc_sc.shape, F32)
    ndiag = (t * tq + tq - 1) // tk

    def scores(jt, s_buf):
        k0 = pl.multiple_of(jt * tk, tk)
        kt = kaug_ref[pl.ds(k0, tk), :]
        sb = k0 // (SLAB_BLOCKS * SLC_BLOCK)
        for g in range(KV_HEADS):
            s_buf[g] = _dot(kt, qat[sb, g])

    def accumulate(jt, s_buf, causal):
        vt = vt_ref[jt]
        for g in range(KV_HEADS):
            s = s_buf[g]
            if causal:
                kpos = jt * tk + lax.broadcasted_iota(jnp.int32, (tk, 1), 0)
                s = jnp.where(kpos <= posr, s, NEG)
            m_old = m_sc[g]
            m_new = jnp.maximum(m_old, jnp.max(s, axis=0, keepdims=True))
            pe = jnp.exp2(s - m_new).astype(BF16)
            acc_sc[g] = jnp.exp2(m_old - m_new) * acc_sc[g] + _dot(vt, pe)
            m_sc[g] = m_new

    scores(0, sa_sc)

    def tile_pair(p, carry):
        jt = 2 * p
        scores(jt + 1, sb_sc)
        accumulate(jt, sa_sc, False)
        scores(jt + 2, sa_sc)
        accumulate(jt + 1, sb_sc, False)
        return carry

    npair = ndiag // 2
    lax.fori_loop(0, npair, tile_pair, 0)

    @pl.when(ndiag % 2 == 0)
    def _():
        accumulate(ndiag, sa_sc, True)

    @pl.when(ndiag % 2 == 1)
    def _():
        scores(ndiag, sb_sc)
        accumulate(ndiag - 1, sa_sc, False)
        accumulate(ndiag, sb_sc, True)

    sg_t = jax.nn.sigmoid(dtg_ref[0]).T
    outs = []
    for g in range(KV_HEADS):
        gate = [jnp.concatenate([sg_t[8 + 3 * (4 * g + r) + k:8 + 3 * (4 * g + r) + k + 1, :] for r in range(4)], axis=1)
                for k in range(3)]
        o_sel = acc_sc[g, 0:LANES, :] / acc_sc[g, LANES:LANES + 1, :]
        o_t = gate[0] * o_cmp[g] + gate[1] * o_sel + gate[2] * o_win[g]
        outs.append(o_t.T)
    _emit_heads(outs, tq, y_ref)


def _nsa(qn, qr, dtg, kchi, kclo, vct, kaug, vt, kwin, vwt, *, tq, ns, nsp, tk, wl):
    b, lq = qn.shape[:2]
    ncb = kchi.shape[1]
    assert nsp * (SLC_BLOCK // CMP_STRIDE) >= ncb
    nslab = nsp // LANES
    kern = functools.partial(_nsa_kernel, tq=tq, ns=ns, nsp=nsp, tk=tk, wl=wl)
    rows = lambda n: pl.BlockSpec((1, tq, n), lambda i, t: (i, t, 0))
    whole = lambda arr: pl.BlockSpec(arr.shape, lambda i, t: (0,) * arr.ndim, pipeline_mode=pl.Buffered(1))
    return pl.pallas_call(
        kern,
        grid=(b, lq // tq),
        in_specs=[rows(512), rows(512), rows(128), whole(kchi), whole(kclo), whole(vct),
                  whole(kaug), whole(vt), whole(kwin), whole(vwt)],
        out_specs=rows(512),
        out_shape=jax.ShapeDtypeStruct((b, lq, 512), F32),
        scratch_shapes=[pltpu.VMEM((nslab, KV_HEADS, 2 * LANES, 4 * tq), BF16),
                        pltpu.VMEM((nsp * (SLC_BLOCK // CMP_STRIDE) + 16, tq), F32),
                        pltpu.VMEM((KV_HEADS, tk, 4 * tq), F32),
                        pltpu.VMEM((KV_HEADS, tk, 4 * tq), F32),
                        pltpu.VMEM((KV_HEADS, 1, 4 * tq), F32),
                        pltpu.VMEM((KV_HEADS, LANES + SUM_ROWS, 4 * tq), F32)],
        compiler_params=_cparams(("arbitrary", "arbitrary")),
    )(qn, qr, dtg, kchi, kclo, vct, kaug, vt, kwin, vwt)


SAMPLE_PAGES_PER_STEP = 16
SAMPLE_TQ = 8


def _nsa_sample_kernel(pt_ref, *refs, npages, past, ns, nsp):
    pps = SAMPLE_PAGES_PER_STEP
    tq = SAMPLE_TQ
    pages = refs[:pps]
    (tail_ref, qn_ref, qr_ref, dtg_ref, w1_ref, pe_ref, b1_ref, w2_ref, ov_ref, cwin_ref, wtail_ref,
     y_ref, tsc, xs, ps, s_all, vt_all, bias_sc) = refs[pps:]
    j = pl.program_id(1)
    ngroups = npages // pps
    hb = PAGE // CMP_STRIDE
    scale = HEAD_DIM ** -0.5
    qr_aug = jnp.concatenate([_group_rows(qr_ref[0] * scale, g, tq) for g in range(KV_HEADS)], axis=0).astype(BF16)

    for i, pg in enumerate(pages):
        for kv in range(2):
            tsc[i, kv] = pg[0, 0, kv * LANES:(kv + 1) * LANES, :].T
            for pos in range(CMP_STRIDE):
                xs[kv, i * hb:(i + 1) * hb, pos * LANES:(pos + 1) * LANES] = tsc[i, kv, pl.ds(pos, hb, stride=CMP_STRIDE), :]
        page_idx = j * pps + i
        s_all[page_idx] = _dot(qr_aug, pg[0, 0, 256:384, :].astype(BF16))
        vt_all[page_idx] = pg[0, 0, 384:512, :].astype(BF16)
    rows = pps * hb
    g0 = pl.multiple_of(j * rows, rows)
    for kv in range(2):
        x = xs[kv]
        for s in range(2):
            ps[kv, s, pl.ds(g0, rows), :] = _dot((x + pe_ref[kv, s:s + 1, :]).astype(BF16), w1_ref[kv, s])

    @pl.when(j == ngroups - 1)
    def _():
        nh = npages * hb
        kcvc = []
        for kv in range(2):
            hid = b1_ref[kv:kv + 1, :] + ps[kv, 0] + pltpu.roll(ps[kv, 1], nh - 1, 0)
            kcvc.append(_dot(jax.nn.gelu(hid).astype(BF16), w2_ref[kv]))
        kc, vc = kcvc
        qpos = past + lax.broadcasted_iota(jnp.int32, (tq, 1), 0)
        pos_r = jnp.concatenate([qpos] * (4 * KV_HEADS), axis=0)
        nrow = 4 * KV_HEADS * tq
        s_all[npages] = _dot(qr_aug, tail_ref[0, 256:384, :].astype(BF16))
        vt_all[npages] = tail_ref[0, 384:512, :].astype(BF16)

        qn_aug = jnp.concatenate([_group_rows(qn_ref[0] * scale, g, tq) for g in range(KV_HEADS)], axis=0)
        c_end = lax.broadcasted_iota(jnp.int32, (1, nh), 1) * CMP_STRIDE + (CMP_BLOCK - 1)
        p_c = _masked_softmax(_dot_nt_hi(qn_aug, kc), c_end <= pos_r)
        o_c = _dot(p_c.astype(BF16), vc.astype(BF16))
        imps = []
        for g in range(KV_HEADS):
            base = 4 * tq * g
            p4 = (p_c[base:base + tq] + p_c[base + tq:base + 2 * tq]
                  + p_c[base + 2 * tq:base + 3 * tq] + p_c[base + 3 * tq:base + 4 * tq])
            imps.append(_importance(p4, ov_ref[...]))
        selb = _topk_bias(jnp.concatenate(imps, axis=0), jnp.concatenate([qpos] * KV_HEADS, axis=0), ns)
        lane_b = lax.broadcasted_iota(jnp.int32, (KV_HEADS * tq, LANES), 1)
        per_page = PAGE // SLC_BLOCK
        assert per_page == 2 and nsp >= per_page * (npages + 1)
        for pp in range(npages + 1):
            lo = jnp.broadcast_to(selb[:, 2 * pp:2 * pp + 1], (KV_HEADS * tq, LANES))
            hi = jnp.broadcast_to(selb[:, 2 * pp + 1:2 * pp + 2], (KV_HEADS * tq, LANES))
            bias_sc[pp] = jnp.where(lane_b < SLC_BLOCK, lo, hi)

        def biased(pidx):
            b = bias_sc[pidx]
            b_r = jnp.concatenate([b[0:tq]] * 4 + [b[tq:2 * tq]] * 4, axis=0)
            return s_all[pidx] + b_r

        def p1(pidx, m):
            s = biased(pidx)
            s_all[pidx] = s
            return jnp.maximum(m, s)

        m = lax.fori_loop(0, npages, p1, jnp.full((nrow, LANES), 2.0 * NEG, F32))
        kpos_t = past + lax.broadcasted_iota(jnp.int32, (1, PAGE), 1)
        s_t = jnp.where(kpos_t <= pos_r, biased(npages), NEG)
        s_all[npages] = s_t
        mrow = jnp.max(jnp.maximum(m, s_t), axis=-1, keepdims=True)

        def p2(pidx, carry):
            l, acc = carry
            p = jnp.exp(s_all[pidx] - mrow)
            return l + p, acc + _dot_nt(p.astype(BF16), vt_all[pidx])

        zero = jnp.zeros((nrow, LANES), F32)
        l, acc = p2(npages, lax.fori_loop(0, npages, p2, (zero, zero), unroll=8))
        o_s = acc / jnp.sum(l, axis=-1, keepdims=True)

        wbuf = cwin_ref.shape[3]
        kw_b = cwin_ref[0, 0, 0:128, :].astype(BF16)
        vw_b = cwin_ref[0, 0, 128:256, :].astype(BF16)
        kt_b = wtail_ref[0, 0:128, :].astype(BF16)
        vt_b = wtail_ref[0, 128:256, :].astype(BF16)
        s_w = jnp.concatenate([_dot(qr_aug, kw_b), _dot(qr_aug, kt_b)], axis=1)
        kposw = jnp.concatenate([past - wbuf + lax.broadcasted_iota(jnp.int32, (1, wbuf), 1), kpos_t], axis=1)
        dpos = pos_r - kposw
        p_w = _masked_softmax(s_w, (dpos >= 0) & (dpos < WINDOW) & (kposw >= 0))
        o_w = _dot_nt(p_w[:, 0:wbuf].astype(BF16), vw_b) + _dot_nt(p_w[:, wbuf:].astype(BF16), vt_b)

        sg = jax.nn.sigmoid(dtg_ref[0])
        outs = []
        for g in range(KV_HEADS):
            sl = slice(4 * tq * g, 4 * tq * (g + 1))
            outs.append(_gate_cols(sg, g, 0, tq) * o_c[sl] + _gate_cols(sg, g, 1, tq) * o_s[sl]
                        + _gate_cols(sg, g, 2, tq) * o_w[sl])
        _emit_heads(outs, tq, y_ref)


def _nsa_sample(layer, cache_t, table, tail_t, qn, qr, dtg, w1e, pee, b1e, w2e, ov, cwin_t, wtail_t, *, past, ns):
    b, npages = table.shape
    pps = SAMPLE_PAGES_PER_STEP
    tq = SAMPLE_TQ
    nsp = ov.shape[1]
    nh = npages * (PAGE // CMP_STRIDE)
    nrow = 4 * KV_HEADS * tq
    kern = functools.partial(_nsa_sample_kernel, npages=npages, past=past, ns=ns, nsp=nsp)
    const = lambda arr: pl.BlockSpec(arr.shape, lambda i, j, pt: (0,) * arr.ndim, pipeline_mode=pl.Buffered(1))
    perb = lambda arr: pl.BlockSpec((1,) + arr.shape[1:], lambda i, j, pt: (i,) + (0,) * (arr.ndim - 1))

    def page_spec(k):
        return pl.BlockSpec((1, 1, 512, PAGE), lambda i, j, pt: (layer, pt[i, j * pps + k], 0, 0))

    return pl.pallas_call(
        kern,
        grid_spec=pltpu.PrefetchScalarGridSpec(
            num_scalar_prefetch=1,
            grid=(b, npages // pps),
            in_specs=[page_spec(k) for k in range(pps)] + [
                perb(tail_t), perb(qn), perb(qr), perb(dtg), const(w1e), const(pee), const(b1e), const(w2e),
                const(ov),
                pl.BlockSpec((1, 1) + cwin_t.shape[2:], lambda i, j, pt: (layer, i, 0, 0)), perb(wtail_t)],
            out_specs=pl.BlockSpec((1, tq, 512), lambda i, j, pt: (i, 0, 0)),
            scratch_shapes=[pltpu.VMEM((pps, 2, PAGE, LANES), F32),
                            pltpu.VMEM((2, pps * 8, CMP_STRIDE * LANES), F32),
                            pltpu.VMEM((2, 2, nh, 256), F32),
                            pltpu.VMEM((npages + 1, nrow, LANES), F32),
                            pltpu.VMEM((npages + 1, LANES, PAGE), BF16),
                            pltpu.VMEM((npages + 1, KV_HEADS * tq, LANES), F32)]),
        out_shape=jax.ShapeDtypeStruct((b, tq, 512), F32),
        compiler_params=_cparams(("arbitrary", "arbitrary")),
    )(table, *([cache_t] * pps), tail_t, qn, qr, dtg, w1e, pee, b1e, w2e, ov, cwin_t, wtail_t)


def _oproj_kernel(x_ref, ys_ref, yn_ref, w_ref, g_ref, r_ref, x1_ref, h_ref, lg_ref):
    a = jnp.concatenate([ys_ref[...], yn_ref[...]], axis=1).astype(BF16)
    x1 = x_ref[...] + _dot(a, w_ref[...])
    x1_ref[...] = x1
    h = _rms(x1, g_ref[...])
    h_ref[...] = h.astype(h_ref.dtype)
    lg_ref[...] = _dot_hi(h, r_ref[...])


def _oproj(x, ys, yn, w, g, router, tm, h_dtype):
    t = x.shape[0]
    row = lambda n: pl.BlockSpec((tm, n), lambda i: (i, 0))
    full = lambda a: pl.BlockSpec(a.shape, lambda i: (0,) * a.ndim)
    return pl.pallas_call(
        _oproj_kernel,
        grid=(t // tm,),
        in_specs=[row(D_MODEL), row(512), row(512), full(w), full(g), full(router)],
        out_specs=[row(D_MODEL), row(D_MODEL), row(LANES)],
        out_shape=[jax.ShapeDtypeStruct((t, D_MODEL), F32), jax.ShapeDtypeStruct((t, D_MODEL), h_dtype),
                   jax.ShapeDtypeStruct((t, LANES), F32)],
        compiler_params=_cparams(("arbitrary",)),
    )(x, ys, yn, w, g, router)


def _ffn_kernel(h_ref, x1_ref, wg_ref, wu_ref, wd_ref, o_ref):
    f = pl.program_id(1)

    @pl.when(f == 0)
    def _():
        o_ref[...] = x1_ref[...]

    hb = h_ref[...]
    act = _silu(_dot(hb, wg_ref[...])) * _dot(hb, wu_ref[...])
    o_ref[...] += _dot(act.astype(BF16), wd_ref[...])


def _ffn(h, x1, wg, wu, wd, tm, tf):
    t = h.shape[0]
    dff = wg.shape[1]
    return pl.pallas_call(
        _ffn_kernel,
        grid=(t // tm, dff // tf),
        in_specs=[pl.BlockSpec((tm, D_MODEL), lambda i, f: (i, 0)),
                  pl.BlockSpec((tm, D_MODEL), lambda i, f: (i, 0)),
                  pl.BlockSpec((D_MODEL, tf), lambda i, f: (0, f)),
                  pl.BlockSpec((D_MODEL, tf), lambda i, f: (0, f)),
                  pl.BlockSpec((tf, D_MODEL), lambda i, f: (f, 0))],
        out_specs=pl.BlockSpec((tm, D_MODEL), lambda i, f: (i, 0)),
        out_shape=jax.ShapeDtypeStruct((t, D_MODEL), F32),
        compiler_params=_cparams(("arbitrary", "arbitrary")),
    )(h, x1, wg, wu, wd)


def _moe_kernel(be_ref, nb_ref, xb_ref, wg_ref, wu_ref, wd_ref, o_ref, xb16):
    b = pl.program_id(0)
    f = pl.program_id(1)

    @pl.when(f == 0)
    def _():
        o_ref[...] = jnp.zeros(o_ref.shape, F32)
        xb16[...] = xb_ref[...].astype(BF16)

    @pl.when(b < nb_ref[0])
    def _():
        hb = xb16[...]
        act = _silu(_dot(hb, wg_ref[0])) * _dot(hb, wu_ref[0])
        o_ref[...] += _dot(act.astype(BF16), wd_ref[0])


def _moe(blk_e, nblk_used, xb, wg, wu, wd, tmb, tf):
    n_blk = xb.shape[0] // tmb
    dex = wg.shape[2]
    nf = dex // tf

    def feff(b, f, nb):
        return jnp.where(b < nb[0], f, nf - 1)

    return pl.pallas_call(
        _moe_kernel,
        grid_spec=pltpu.PrefetchScalarGridSpec(
            num_scalar_prefetch=2,
            grid=(n_blk, nf),
            in_specs=[pl.BlockSpec((tmb, D_MODEL), lambda b, f, be, nb: (b, 0)),
                      pl.BlockSpec((1, D_MODEL, tf), lambda b, f, be, nb: (be[b], 0, feff(b, f, nb))),
                      pl.BlockSpec((1, D_MODEL, tf), lambda b, f, be, nb: (be[b], 0, feff(b, f, nb))),
                      pl.BlockSpec((1, tf, D_MODEL), lambda b, f, be, nb: (be[b], feff(b, f, nb), 0))],
            out_specs=pl.BlockSpec((tmb, D_MODEL), lambda b, f, be, nb: (b, 0)),
            scratch_shapes=[pltpu.VMEM((tmb, D_MODEL), BF16)]),
        out_shape=jax.ShapeDtypeStruct((n_blk * tmb, D_MODEL), F32),
        compiler_params=_cparams(("arbitrary", "arbitrary")),
    )(blk_e, nblk_used, xb, wg, wu, wd)


def _moe_layer(h, x1, logits, wg, wu, wd, tmb, tf):
    t = h.shape[0]
    top_v, top_e = lax.top_k(logits[:, :N_EXPERTS], TOP_K)
    gate = jax.nn.softmax(top_v, axis=-1)
    e_flat = top_e.reshape(-1)
    tok = jnp.repeat(jnp.arange(t, dtype=jnp.int32), TOP_K)
    order = jnp.argsort(e_flat)
    e_s = e_flat[order]
    tok_s = tok[order]
    counts = jnp.bincount(e_flat, length=N_EXPERTS)
    starts = jnp.cumsum(counts) - counts
    padded = (counts + tmb - 1) // tmb * tmb
    pends = jnp.cumsum(padded)
    pstarts = pends - padded
    dest_s = (pstarts[e_s] + jnp.arange(t * TOP_K, dtype=jnp.int32) - starts[e_s]).astype(jnp.int32)
    n_blk = -(-(t * TOP_K) // tmb) + N_EXPERTS
    buf_tok = jnp.full((n_blk * tmb,), t, dtype=jnp.int32).at[dest_s].set(tok_s)
    blk_e = jnp.minimum(jnp.searchsorted(pends, jnp.arange(n_blk, dtype=jnp.int32) * tmb, side='right'),
                        N_EXPERTS - 1).astype(jnp.int32)
    nblk_used = (pends[-1] // tmb).astype(jnp.int32).reshape(1)
    h_pad = jnp.concatenate([h, jnp.zeros((1, D_MODEL), h.dtype)], axis=0)
    out = _moe(blk_e, nblk_used, h_pad[buf_tok], wg, wu, wd, tmb, tf)
    dest = jnp.zeros((t * TOP_K,), jnp.int32).at[order].set(dest_s).reshape(t, TOP_K)
    return x1 + out[dest[:, 0]] * gate[:, 0:1] + out[dest[:, 1]] * gate[:, 1:2]


def _final_kernel(x_ref, g_ref, o_ref):
    o_ref[...] = _rms(x_ref[...], g_ref[...])


def _final_norm(x, g, tm):
    t = x.shape[0]
    return pl.pallas_call(
        _final_kernel,
        grid=(t // tm,),
        in_specs=[pl.BlockSpec((tm, D_MODEL), lambda i: (i, 0)), pl.BlockSpec(g.shape, lambda i: (0, 0))],
        out_specs=pl.BlockSpec((tm, D_MODEL), lambda i: (i, 0)),
        out_shape=jax.ShapeDtypeStruct((t, D_MODEL), F32),
        compiler_params=_cparams(("arbitrary",)),
    )(x, g)


def _rope_tables(pos):
    inv = 1.0 / (ROPE_THETA ** (jnp.arange(0, HEAD_DIM, 2, dtype=F32) / HEAD_DIM))
    ang = pos.astype(F32)[:, None] * inv[None, :]
    cos = jnp.cos(ang)
    sin = jnp.sin(ang)
    return jnp.concatenate([cos, cos, cos, cos], axis=1), jnp.concatenate([-sin, sin, -sin, sin], axis=1)


def _overlap(ncb, nsp, ns):
    c = jnp.arange(ncb, dtype=jnp.int32)[:, None] * CMP_STRIDE
    s = jnp.arange(nsp, dtype=jnp.int32)[None, :] * SLC_BLOCK
    ov = (c <= s + SLC_BLOCK - 1) & (c + CMP_BLOCK - 1 >= s) & (jnp.arange(nsp)[None, :] < ns)
    return ov.astype(BF16)


def _pad_lanes(v, n):
    return jnp.zeros((1, n), F32).at[0, :v.shape[0]].set(v.astype(F32))


def kernel(x_prompt, x_sample, cache_kv, cache_win, state_ssm, state_conv, page_table, ln_mix, w_in, conv_w,
           conv_b, dt_bias, a_log, d_skip, ssd_norm, cmp_pos, cmp_w1, cmp_b1, cmp_w2, w_out, ln_ffn,
           ffn_w_gate, ffn_w_up, ffn_w_down, moe_router, moe_w_gate, moe_w_up, moe_w_down, ln_final):
    depth = w_in.shape[0]
    bsz, seq = x_prompt.shape[:2]
    dbsz, dseq = x_sample.shape[:2]
    n_pages = page_table.shape[1]
    past = n_pages * cache_kv.shape[2]
    win_buf = cache_win.shape[2]
    npool = cache_kv.shape[1]
    assert bsz == 1 and cache_kv.shape[2] == PAGE and seq % PAGE == 0 and seq >= WINDOW + PAGE
    assert win_buf == WINDOW and past % PAGE == 0
    assert not any(past <= CMP_STRIDE * c + CMP_BLOCK - 1 <= past + dseq - 1
                   for c in range(past // CMP_STRIDE - 2, past // CMP_STRIDE + 2))

    ts = dbsz * dseq
    ns_p = seq // SLC_BLOCK
    ns_s = -(-(past + dseq) // SLC_BLOCK)
    nsp_p = -(-ns_p // LANES) * LANES
    nsp_s = -(-ns_s // LANES) * LANES
    assert dseq <= SAMPLE_TQ and n_pages % SAMPLE_PAGES_PER_STEP == 0
    cache_t = jnp.transpose(cache_kv, (0, 1, 3, 4, 5, 2)).reshape(depth, npool, 512, PAGE)
    cwin_t = jnp.transpose(cache_win, (0, 1, 3, 4, 5, 2)).reshape(depth, dbsz, 256, win_buf)

    cs_p, sn_p = _rope_tables(jnp.arange(seq, dtype=jnp.int32))
    cs_s, sn_s = _rope_tables(past + jnp.arange(ts, dtype=jnp.int32) % dseq)
    ov_s = _overlap(past // CMP_STRIDE, nsp_s, ns_s)
    ident = jnp.arange(seq // PAGE, dtype=jnp.int32).reshape(1, -1)

    xp = x_prompt.reshape(seq, D_MODEL)
    xs = x_sample.reshape(ts, D_MODEL)
    outs = {k: [] for k in ('kv_p', 'kv_s', 'win_p', 'win_s', 'ssm_p', 'ssm_s', 'conv_p', 'conv_s')}

    for l in range(depth):
        w = w_in[l]
        wr = jnp.concatenate([w[:, :1536], w[:, 1544:2824], w[:, 1536:1544], w[:, 2824:2848],
                              jnp.zeros((D_MODEL, C_END - 2848), F32)], axis=1).astype(BF16)
        g_mix = ln_mix[l].reshape(1, -1)
        cw8 = jnp.zeros((8, CONV_DIM), F32).at[:CONV_W].set(conv_w[l])
        cb = conv_b[l].reshape(1, -1)
        dtb = _pad_lanes(dt_bias[l], LANES)
        a_neg = _pad_lanes(-jnp.exp(a_log[l].astype(F32)), LANES)
        dsk = jnp.repeat(d_skip[l].astype(F32), SSD_HEAD_DIM).reshape(1, -1)
        ng = ssd_norm[l].reshape(1, -1)
        w1 = cmp_w1[l].reshape(2, 2, CMP_STRIDE, HEAD_DIM, CMP_HIDDEN)
        eye = jnp.eye(KV_HEADS, dtype=F32)
        w1e = jnp.einsum('vspdh,gk->vspgdkh', w1, eye).reshape(2, 2, CMP_STRIDE * LANES, 256).astype(BF16)
        pe = cmp_pos[l].reshape(2, 2, CMP_STRIDE, 1, HEAD_DIM)
        pee = jnp.broadcast_to(pe, (2, 2, CMP_STRIDE, KV_HEADS, HEAD_DIM)).reshape(2, 2, CMP_STRIDE * LANES)
        b1e = jnp.concatenate([cmp_b1[l], cmp_b1[l]], axis=1)
        w2e = jnp.einsum('vhd,gk->vghkd', cmp_w2[l], eye).reshape(2, 256, LANES).astype(BF16)
        wo = w_out[l].astype(BF16)
        g_ffn = ln_ffn[l].reshape(1, -1)
        if l % 2 == 0:
            router = jnp.zeros((D_MODEL, LANES), F32)
        else:
            router = jnp.zeros((D_MODEL, LANES), F32).at[:, :N_EXPERTS].set(moe_router[l // 2])

        z, xbc, dtg, q, qr, kv, win = _front(xp, g_mix, wr, cs_p, sn_p, 256)
        y_ssd, hist, hfin = _ssd(z[None], xbc[None], dtg[None], jnp.zeros((1, 8, CONV_DIM), F32),
                                 jnp.zeros((1, D_SSD, SSD_STATE), F32), cw8, cb, dtb, a_neg, dsk, ng, SSD_CHUNK)
        kv_pages = kv.reshape(seq // PAGE, PAGE, 512)
        kchi, kclo, vct = _compress(kv_pages, ident, w1e, pee, b1e, w2e)
        kaug, vt, kwin, vwt = _kvfmt(kv, win, NSA_TK)
        y_nsa = _nsa(q[None], qr[None], dtg[None], kchi, kclo, vct, kaug, vt, kwin, vwt,
                     tq=128, ns=ns_p, nsp=nsp_p, tk=NSA_TK, wl=WINDOW + 128)
        h_dtype = BF16 if l % 2 == 0 else F32
        x1, h, logits = _oproj(xp, y_ssd[0], y_nsa[0], wo, g_ffn, router, 512, h_dtype)
        outs['kv_p'].append(kv.reshape(1, seq, 4, KV_HEADS, HEAD_DIM))
        outs['win_p'].append(win[seq - WINDOW:].reshape(1, WINDOW, 2, KV_HEADS, HEAD_DIM))
        outs['ssm_p'].append(hfin.reshape(1, SSD_HEADS, SSD_HEAD_DIM, SSD_STATE))
        outs['conv_p'].append(hist[:, 5:8])

        zs, xbcs, dtgs, qs, qrs, kvs, wins = _front(xs, g_mix, wr, cs_s, sn_s, ts)
        padl = lambda v: jnp.pad(v.reshape(dbsz, dseq, -1), ((0, 0), (0, SSD_CHUNK - dseq), (0, 0)))
        hist8 = jnp.pad(state_conv[l], ((0, 0), (8 - (CONV_W - 1), 0), (0, 0)))
        y_ssd_s, hist_s, hfin_s = _ssd(padl(zs), padl(xbcs), padl(dtgs), hist8,
                                       state_ssm[l].reshape(dbsz, D_SSD, SSD_STATE),
                                       cw8, cb, dtb, a_neg, dsk, ng, dseq)
        tail_t = jnp.pad(jnp.transpose(kvs.reshape(dbsz, dseq, 512), (0, 2, 1)), ((0, 0), (0, 0), (0, PAGE - dseq)))
        wtail_t = jnp.pad(jnp.transpose(wins.reshape(dbsz, dseq, 256), (0, 2, 1)), ((0, 0), (0, 0), (0, PAGE - dseq)))
        padq = lambda v: jnp.pad(v.reshape(dbsz, dseq, -1), ((0, 0), (0, SAMPLE_TQ - dseq), (0, 0)))
        y_nsa_s = _nsa_sample(l, cache_t, page_table, tail_t, padq(qs), padq(qrs), padq(dtgs), w1e, pee, b1e, w2e,
                              ov_s, cwin_t, wtail_t, past=past, ns=ns_s)
        y_nsa_s = y_nsa_s[:, :dseq].reshape(ts, D_NSA)
        x1s, hs, logits_s = _oproj(xs, y_ssd_s[:, :dseq].reshape(ts, D_SSD), y_nsa_s, wo, g_ffn, router, ts, h_dtype)
        outs['kv_s'].append(kvs.reshape(dbsz, dseq, 4, KV_HEADS, HEAD_DIM))
        win_all = jnp.concatenate([cache_win[l], wins.reshape(dbsz, dseq, 2, KV_HEADS, HEAD_DIM)], axis=1)
        outs['win_s'].append(win_all[:, dseq:])
        outs['ssm_s'].append(hfin_s.reshape(dbsz, SSD_HEADS, SSD_HEAD_DIM, SSD_STATE))
        outs['conv_s'].append(hist_s[:, 5:8])

        if l % 2 == 0:
            wg = ffn_w_gate[l // 2].astype(BF16)
            wu = ffn_w_up[l // 2].astype(BF16)
            wd = ffn_w_down[l // 2].astype(BF16)
            xp = _ffn(h, x1, wg, wu, wd, 1024, 256)
            xs = _ffn(hs, x1s, wg, wu, wd, ts, 256)
        else:
            wg = moe_w_gate[l // 2].astype(BF16)
            wu = moe_w_up[l // 2].astype(BF16)
            wd = moe_w_down[l // 2].astype(BF16)
            xp = _moe_layer(h, x1, logits, wg, wu, wd, 1024, 512)
            xs = _moe_layer(hs, x1s, logits_s, wg, wu, wd, 128, 512)

    g_fin = ln_final.reshape(1, -1)
    y_prompt = _final_norm(xp, g_fin, 1024).reshape(bsz, seq, D_MODEL)
    y_sample = _final_norm(xs, g_fin, ts).reshape(dbsz, dseq, D_MODEL)
    st = lambda k: jnp.stack(outs[k])
    return (y_prompt, y_sample, st('kv_p'), st('kv_s'), st('win_p'), st('win_s'),
            st('ssm_p'), st('ssm_s'), st('conv_p'), st('conv_s'))
```

```python
import functools
import math

import jax
import jax.numpy as jnp
from jax import lax
from jax.experimental import pallas as pl
from jax.experimental.pallas import tpu as pltpu

F32 = jnp.float32
BF16 = jnp.bfloat16

D_MODEL = 1024
SSD_HEADS = 8
SSD_HEAD_DIM = 64
D_SSD = 512
SSD_GROUPS = 2
SSD_STATE = 128
CONV_W = 4
CONV_DIM = 1024
SSD_CHUNK = 128
NSA_HEADS = 8
HEAD_DIM = 64
KV_HEADS = 2
D_NSA = 512
CMP_BLOCK = 32
CMP_STRIDE = 16
CMP_HIDDEN = 128
SLC_BLOCK = 64
TOP_N = 16
WINDOW = 512
ROPE_THETA = 10000.0
FORCE_BONUS = 1.0e4
N_EXPERTS = 8
TOP_K = 2
EPS = 1e-6

PAGE = 128
LANES = 128
LOG2E = 1.4426950408889634
NEG = -1.0e30
SLAB_BLOCKS = 128
VMEM_LIMIT = 56 * 1024 * 1024

C_Z, C_XBC, C_Q, C_KV, C_DT, C_END = 0, 512, 1536, 2048, 2816, 2944


def _cparams(sem):
    return pltpu.CompilerParams(dimension_semantics=sem, vmem_limit_bytes=VMEM_LIMIT)


def _dot(a, b):
    return jnp.dot(a, b, preferred_element_type=F32)


def _dot_nt(a, b):
    return lax.dot_general(a, b, (((1,), (1,)), ((), ())), preferred_element_type=F32)


def _split3(x):
    h1 = x.astype(BF16)
    r1 = x - h1.astype(F32)
    h2 = r1.astype(BF16)
    h3 = (r1 - h2.astype(F32)).astype(BF16)
    return h1, h2, h3


def _dot_nt_hi(a, b):
    a1, a2, _ = _split3(a)
    b1, b2, _ = _split3(b)
    return _dot_nt(a1, b1) + (_dot_nt(a1, b2) + _dot_nt(a2, b1))


def _dot_hi(a, b):
    a1, a2, _ = _split3(a)
    b1, b2, _ = _split3(b)
    return _dot(a1, b1) + (_dot(a1, b2) + _dot(a2, b1))


def _silu(x):
    return x * jax.nn.sigmoid(x)


def _rms(x, g):
    return x * lax.rsqrt(jnp.mean(x * x, axis=-1, keepdims=True) + EPS) * g


def _front_kernel(x_ref, g_ref, w_ref, cs_ref, sn_ref,
                  z_ref, xbc_ref, dtg_ref, q_ref, qr_ref, kv_ref, win_ref):
    xn = _rms(x_ref[...], g_ref[...])
    p = _dot(xn.astype(BF16), w_ref[...])
    cs = cs_ref[...]
    sn = sn_ref[...]
    lane = lax.broadcasted_iota(jnp.int32, cs.shape, 1)
    first = (lane % HEAD_DIM) < (HEAD_DIM // 2)

    def rope(v):
        sw = jnp.where(first, pltpu.roll(v, LANES - HEAD_DIM // 2, 1), pltpu.roll(v, HEAD_DIM // 2, 1))
        return v * cs + sw * sn

    z_ref[...] = p[:, C_Z:C_XBC]
    xbc_ref[...] = p[:, C_XBC:C_Q]
    dtg_ref[...] = p[:, C_DT:C_END]
    q_ref[...] = p[:, C_Q:C_KV]
    for s in range(4):
        qr_ref[:, s * LANES:(s + 1) * LANES] = rope(p[:, C_Q + s * LANES:C_Q + (s + 1) * LANES])
    kv_ref[:, 0:256] = p[:, C_KV:C_KV + 256]
    kv_ref[:, 256:384] = rope(p[:, C_KV + 256:C_KV + 384])
    kv_ref[:, 384:512] = p[:, C_KV + 384:C_KV + 512]
    win_ref[:, 0:128] = rope(p[:, C_KV + 512:C_KV + 640])
    win_ref[:, 128:256] = p[:, C_KV + 640:C_KV + 768]


def _front(x, g, w, cs, sn, tm):
    t = x.shape[0]
    row = lambda n: pl.BlockSpec((tm, n), lambda i: (i, 0))
    full = lambda a: pl.BlockSpec(a.shape, lambda i: (0,) * a.ndim)
    outs = [512, 1024, 128, 512, 512, 512, 256]
    return pl.pallas_call(
        _front_kernel,
        grid=(t // tm,),
        in_specs=[row(D_MODEL), full(g), full(w), row(LANES), row(LANES)],
        out_specs=[row(n) for n in outs],
        out_shape=[jax.ShapeDtypeStruct((t, n), F32) for n in outs],
        compiler_params=_cparams(("arbitrary",)),
    )(x, g, w, cs, sn)


def _expand_heads(v):
    r = v.shape[0]
    lane = lax.broadcasted_iota(jnp.int32, (r, LANES), 1)
    outs = []
    for k in range(4):
        a = jnp.broadcast_to(v[:, 2 * k:2 * k + 1], (r, LANES))
        b = jnp.broadcast_to(v[:, 2 * k + 1:2 * k + 2], (r, LANES))
        outs.append(jnp.where(lane < SSD_HEAD_DIM, a, b))
    return jnp.concatenate(outs, axis=1)


def _ssd_kernel(z_ref, xbc_ref, dtg_ref, hist_ref, h0_ref, cw_ref, cb_ref, dtb_ref, a_ref, dsk_ref, ng_ref,
                y_ref, hist_o_ref, hfin_ref, xpad, ht, *, nchunks, valid_last):
    c = pl.program_id(1)
    q = SSD_CHUNK

    @pl.when(c == 0)
    def _():
        xpad[0:8, :] = hist_ref[0]
        ht[...] = h0_ref[0].T

    xpad[8:8 + q, :] = xbc_ref[0]
    conv = cb_ref[...]
    for k in range(CONV_W):
        conv = conv + xpad[5 + k:5 + k + q, :] * cw_ref[k:k + 1, :]
    xc = _silu(conv)
    xs = xc[:, 0:D_SSD]
    bm = xc[:, D_SSD:D_SSD + 256]
    cm = xc[:, D_SSD + 256:D_SSD + 512]

    row = lax.broadcasted_iota(jnp.int32, (q, q), 0)
    col = lax.broadcasted_iota(jnp.int32, (q, q), 1)
    nvalid = jnp.where(c == nchunks - 1, valid_last, q)
    xdt_in = dtg_ref[0] + dtb_ref[...]
    dt = jnp.maximum(xdt_in, 0.0) + jnp.log1p(jnp.exp(-jnp.abs(xdt_in)))
    dt = jnp.where(row < nvalid, dt, 0.0)
    dta = dt * a_ref[...]
    causal = row >= col
    cum = jnp.dot(causal.astype(F32), dta, preferred_element_type=F32, precision=lax.Precision.HIGHEST)
    cum_t = cum.T
    ecum = jnp.exp(cum)
    toend = jnp.exp(cum[q - 1:q, :] - cum)
    dt_e = _expand_heads(dt)
    ecum_e = _expand_heads(ecum)
    toend_e = _expand_heads(toend)
    xdt = xs * dt_e
    xdt_b = xdt.astype(BF16)
    xw_b = (xdt * toend_e).astype(BF16)
    lane = lax.broadcasted_iota(jnp.int32, (q, LANES), 1)

    y_intra = []
    y_inter = []
    for g in range(SSD_GROUPS):
        bg = bm[:, g * 128:(g + 1) * 128]
        cg_b = cm[:, g * 128:(g + 1) * 128].astype(BF16)
        bg_b = bg.astype(BF16)
        cb = _dot_nt(cg_b, bg_b)
        htg = ht[:, g * 256:(g + 1) * 256]
        y_inter.append(_dot(cg_b, htg.astype(BF16)))
        for kk in range(2):
            slab = xdt_b[:, (2 * g + kk) * LANES:(2 * g + kk + 1) * LANES]
            res = []
            for hh in range(2):
                h = 4 * g + 2 * kk + hh
                seg = cum[:, h:h + 1] - cum_t[h:h + 1, :]
                decay = jnp.exp(jnp.where(causal, seg, NEG))
                res.append(_dot((decay * cb).astype(BF16), slab))
            y_intra.append(jnp.where(lane < SSD_HEAD_DIM, res[0], res[1]))
        s_new = _dot(bg.T.astype(BF16), xw_b[:, g * 256:(g + 1) * 256])
        ht[:, g * 256:(g + 1) * 256] = htg * ecum_e[q - 1:q, g * 256:(g + 1) * 256] + s_new

    y = jnp.concatenate(y_intra, axis=1) + jnp.concatenate(y_inter, axis=1) * ecum_e + dsk_ref[...] * xs
    y = y * _silu(z_ref[0])
    y_ref[0] = _rms(y, ng_ref[...])

    xpad[0:8, :] = xpad[q:q + 8, :]

    @pl.when(c == nchunks - 1)
    def _():
        hist_o_ref[0] = xpad[valid_last:valid_last + 8, :]
        hfin_ref[0] = ht[...].T


def _ssd(z, xbc, dtg, hist8, h0, cw8, cb, dtb, a, dsk, ng, valid_last):
    b, l = z.shape[:2]
    nchunks = l // SSD_CHUNK
    rows = lambda n: pl.BlockSpec((1, SSD_CHUNK, n), lambda i, c: (i, c, 0))
    perb = lambda r, n: pl.BlockSpec((1, r, n), lambda i, c: (i, 0, 0))
    full = lambda arr: pl.BlockSpec(arr.shape, lambda i, c: (0,) * arr.ndim)
    kern = functools.partial(_ssd_kernel, nchunks=nchunks, valid_last=valid_last)
    return pl.pallas_call(
        kern,
        grid=(b, nchunks),
        in_specs=[rows(512), rows(1024), rows(128), perb(8, 1024), perb(512, 128),
                  full(cw8), full(cb), full(dtb), full(a), full(dsk), full(ng)],
        out_specs=[rows(512), perb(8, 1024), perb(512, 128)],
        out_shape=[jax.ShapeDtypeStruct((b, l, 512), F32),
                   jax.ShapeDtypeStruct((b, 8, 1024), F32),
                   jax.ShapeDtypeStruct((b, 512, 128), F32)],
        scratch_shapes=[pltpu.VMEM((SSD_CHUNK + 8, CONV_DIM), F32), pltpu.VMEM((SSD_STATE, D_SSD), F32)],
        compiler_params=_cparams(("arbitrary", "arbitrary")),
    )(z, xbc, dtg, hist8, h0, cw8, cb, dtb, a, dsk, ng)


CMP_CHUNK = 256
SUM_ROWS = 16
NSA_TK = 512
CMP_GROUP_PAGES = 16


def _compress_kernel(pt_ref, pgk_ref, pgv_ref, w1_ref, pe_ref, b1_ref, w2_ref, w2t_ref, kchi_ref, kclo_ref, vct_ref,
                     xs, ps, *, npages):
    p = pl.program_id(1)
    sub = p % CMP_GROUP_PAGES
    hb = PAGE // CMP_STRIDE
    r0 = pl.multiple_of(sub * hb, hb)
    for kv, pg_ref in enumerate((pgk_ref, pgv_ref)):
        for pos in range(CMP_STRIDE):
            xs[kv, pl.ds(r0, hb), pos * LANES:(pos + 1) * LANES] = pg_ref[0, pl.ds(pos, hb, stride=CMP_STRIDE), :]

    @pl.when(sub == CMP_GROUP_PAGES - 1)
    def _():
        rows = CMP_GROUP_PAGES * hb
        g0 = pl.multiple_of((p // CMP_GROUP_PAGES) * rows, rows)
        for kv in range(2):
            x = xs[kv]
            for s in range(2):
                ps[kv, s, pl.ds(g0, rows), :] = _dot((x + pe_ref[kv, s:s + 1, :]).astype(BF16), w1_ref[kv, s])

    @pl.when(p == npages - 1)
    def _():
        nh = npages * hb
        act = [jax.nn.gelu(b1_ref[kv:kv + 1, :] + ps[kv, 0] + pltpu.roll(ps[kv, 1], nh - 1, 0)).astype(BF16)
               for kv in range(2)]
        kc = _dot(act[0], w2_ref[0])
        hi = kc.astype(BF16)
        kchi_ref[0] = hi
        kclo_ref[0] = (kc - hi.astype(F32)).astype(BF16)
        vct_ref[0] = _dot_nt(w2t_ref[...], act[1]).astype(BF16)


def _compress(pages, table, w1e, pee, b1e, w2e):
    b, npages = table.shape
    w2t = w2e[1].T
    nh = npages * (PAGE // CMP_STRIDE)
    kern = functools.partial(_compress_kernel, npages=npages)
    full = lambda arr: pl.BlockSpec(arr.shape, lambda i, p, pt: (0,) * arr.ndim)
    return pl.pallas_call(
        kern,
        grid_spec=pltpu.PrefetchScalarGridSpec(
            num_scalar_prefetch=1,
            grid=(b, npages),
            in_specs=[pl.BlockSpec((1, PAGE, LANES), lambda i, p, pt: (pt[i, p], 0, 0)),
                      pl.BlockSpec((1, PAGE, LANES), lambda i, p, pt: (pt[i, p], 0, 1)),
                      full(w1e), full(pee), full(b1e), full(w2e), full(w2t)],
            out_specs=[pl.BlockSpec((1, nh, LANES), lambda i, p, pt: (i, 0, 0)),
                       pl.BlockSpec((1, nh, LANES), lambda i, p, pt: (i, 0, 0)),
                       pl.BlockSpec((1, LANES, nh), lambda i, p, pt: (i, 0, 0))],
            scratch_shapes=[pltpu.VMEM((2, CMP_GROUP_PAGES * 8, CMP_STRIDE * LANES), F32),
                            pltpu.VMEM((2, 2, nh, 256), F32)]),
        out_shape=[jax.ShapeDtypeStruct((b, nh, LANES), BF16), jax.ShapeDtypeStruct((b, nh, LANES), BF16),
                   jax.ShapeDtypeStruct((b, LANES, nh), BF16)],
        compiler_params=_cparams(("arbitrary", "arbitrary")),
    )(table, pages, pages, w1e, pee, b1e, w2e, w2t)


def _kvfmt_kernel(kv_ref, win_ref, k_ref, vt_ref, kw_ref, vwt_ref):
    p = pl.program_id(0)
    src = kv_ref[...]
    r = lax.broadcasted_iota(jnp.int32, (PAGE, LANES), 0)
    lane = lax.broadcasted_iota(jnp.int32, (PAGE, LANES), 1)
    blk = (2 * p + r // SLC_BLOCK) % SLAB_BLOCKS
    k_ref[:, 0:128] = src[:, 0:128].astype(BF16)
    k_ref[:, 128:256] = jnp.where(lane == blk, 1.0, 0.0).astype(BF16)
    vt_ref[0, 0:LANES, :] = src[:, 128:256].T.astype(BF16)
    vt_ref[0, LANES:LANES + SUM_ROWS, :] = jnp.ones((SUM_ROWS, PAGE), BF16)
    kw_ref[...] = win_ref[:, 0:128].astype(BF16)
    vwt_ref[0] = win_ref[:, 128:256].T.astype(BF16)


def _kvfmt(kv, win, tk):
    l = kv.shape[0]
    per = tk // PAGE
    return pl.pallas_call(
        _kvfmt_kernel,
        grid=(l // PAGE,),
        in_specs=[pl.BlockSpec((PAGE, 256), lambda p: (p, 1)), pl.BlockSpec((PAGE, 256), lambda p: (p, 0))],
        out_specs=[pl.BlockSpec((PAGE, 256), lambda p: (p, 0)),
                   pl.BlockSpec((1, LANES + SUM_ROWS, PAGE), lambda p: (p // per, 0, p % per)),
                   pl.BlockSpec((PAGE, LANES), lambda p: (p, 0)),
                   pl.BlockSpec((1, LANES, PAGE), lambda p: (p, 0, 0))],
        out_shape=[jax.ShapeDtypeStruct((l, 256), BF16),
                   jax.ShapeDtypeStruct((l // tk, LANES + SUM_ROWS, tk), BF16),
                   jax.ShapeDtypeStruct((l, LANES), BF16),
                   jax.ShapeDtypeStruct((l // PAGE, LANES, PAGE), BF16)],
        compiler_params=_cparams(("arbitrary",)),
    )(kv, win)


def _group_rows(qfull, g, tq):
    lane = lax.broadcasted_iota(jnp.int32, (tq, LANES), 1)
    keep = (lane >= HEAD_DIM) if g else (lane < HEAD_DIM)
    pieces = []
    for r in range(4):
        h = 4 * g + r
        slab = qfull[:, (h // 2) * LANES:(h // 2 + 1) * LANES]
        if h % 2 != g:
            slab = pltpu.roll(slab, HEAD_DIM, 1)
        pieces.append(jnp.where(keep, slab, 0.0))
    return jnp.concatenate(pieces, axis=0)


def _masked_softmax(s, mask):
    s = jnp.where(mask, s, NEG)
    m = jnp.max(s, axis=-1, keepdims=True)
    e = jnp.where(mask, jnp.exp(s - m), 0.0)
    return e / jnp.maximum(jnp.sum(e, axis=-1, keepdims=True), 1e-30)


def _importance(p4, ov):
    h1, h2, h3 = _split3(p4)
    return _dot(h1, ov) + (_dot(h2, ov) + _dot(h3, ov))


def _topk_bias(imp, qpos, ns):
    tq, nsp = imp.shape
    jf = lax.broadcasted_iota(jnp.int32, (tq, nsp), 1)
    jff = jf.astype(F32)
    cur = qpos >> 6
    valid = jf <= cur
    forced = valid & ((jf == 0) | (jf == cur) | (jf == cur - 1))
    score = jnp.where(forced, FORCE_BONUS, jnp.where(valid, imp, -1.0))
    score = jnp.where(jf < ns, score, -2.0)

    def pick(_, carry):
        sc, sb = carry
        mx = jnp.max(sc, axis=-1, keepdims=True)
        jm = jnp.min(jnp.where(sc == mx, jff, float(nsp)), axis=-1, keepdims=True)
        hit = jff == jm
        return jnp.where(hit, -3.0, sc), jnp.where(hit, 0.0, sb)

    return lax.fori_loop(0, TOP_N, pick, (score, jnp.full((tq, nsp), NEG, F32)))[1]


def _gate_cols(sg, g, k, tq):
    return jnp.concatenate([sg[:, 8 + 3 * (4 * g + r) + k:8 + 3 * (4 * g + r) + k + 1] for r in range(4)], axis=0)


def _emit_heads(outs, tq, y_ref):
    lane = lax.broadcasted_iota(jnp.int32, (tq, LANES), 1)
    for k in range(4):
        g = k // 2
        r0 = (2 * k) % 4
        a = outs[g][r0 * tq:(r0 + 1) * tq]
        b = outs[g][(r0 + 1) * tq:(r0 + 2) * tq]
        if g == 0:
            slab = jnp.where(lane < HEAD_DIM, a, pltpu.roll(b, HEAD_DIM, 1))
        else:
            slab = jnp.where(lane < HEAD_DIM, pltpu.roll(a, HEAD_DIM, 1), b)
        y_ref[0, :, k * LANES:(k + 1) * LANES] = slab


def _masked_softmax_t(s, mask):
    s = jnp.where(mask, s, NEG)
    m = jnp.max(s, axis=0, keepdims=True)
    e = jnp.exp(s - m)
    inv = jnp.where(m > 0.5 * NEG, 1.0 / jnp.maximum(jnp.sum(e, axis=0, keepdims=True), 1e-30), 0.0)
    return e * inv


def _topk_bias_t(imp, qpos, ns):
    nsp, tq = imp.shape
    jf = lax.broadcasted_iota(jnp.int32, (nsp, tq), 0)
    jff = jf.astype(F32)
    cur = qpos >> 6
    valid = jf <= cur
    forced = valid & ((jf == 0) | (jf == cur) | (jf == cur - 1))
    score = jnp.where(forced, FORCE_BONUS, jnp.where(valid, imp, -1.0))
    score = jnp.where(jf < ns, score, -2.0)

    def pick(_, carry):
        sc, sb = carry
        mx = jnp.max(sc, axis=0, keepdims=True)
        jm = jnp.min(jnp.where(sc == mx, jff, float(nsp)), axis=0, keepdims=True)
        hit = jff == jm
        return jnp.where(hit, -3.0, sc), jnp.where(hit, 0.0, sb)

    return lax.fori_loop(0, TOP_N, pick, (score, jnp.full((nsp, tq), NEG, F32)))[1]


def _nsa_kernel(qn_ref, qr_ref, dtg_ref, kchi_ref, kclo_ref, vct_ref, kaug_ref, vt_ref, kwin_ref, vwt_ref,
                y_ref, qat, p4_sc, cs_sc, sa_sc, sb_sc, m_sc, acc_sc, *, tq, ns, nsp, tk, wl):
    t = pl.program_id(1)
    scale = HEAD_DIM ** -0.5
    r4 = 4 * tq
    nslab = nsp // LANES
    lane4 = lax.broadcasted_iota(jnp.int32, (1, r4), 1)
    posr = t * tq + (lane4 & (tq - 1))
    qpos = t * tq + lax.broadcasted_iota(jnp.int32, (1, tq), 1)

    qn = qn_ref[0] * scale
    qr = qr_ref[0] * scale
    ncb = kchi_ref.shape[1]
    cch = cs_sc.shape[1]
    cmax = (t * tq + tq - CMP_BLOCK) // CMP_STRIDE
    nch = jnp.minimum(cmax // cch + 1, ncb // cch)

    ws = pl.multiple_of(jnp.maximum(t * tq - WINDOW, 0), LANES)
    kw = kwin_ref[pl.ds(ws, wl), :]
    kposw = ws + lax.broadcasted_iota(jnp.int32, (wl, 1), 0)
    dpos = posr - kposw
    mask_w = (dpos >= 0) & (dpos < WINDOW)
    wpage = ws // PAGE

    p4_sc[0:8, :] = jnp.zeros((8, tq), F32)
    p4_sc[8 + ncb:, :] = jnp.zeros((p4_sc.shape[0] - 8 - ncb, tq), F32)
    o_cmp = []
    o_win = []
    for g in range(KV_HEADS):
        qn_t = _group_rows(qn, g, tq).T
        qr_t = _group_rows(qr, g, tq).T
        qh = qn_t.astype(BF16)
        ql = (qn_t - qh.astype(F32)).astype(BF16)

        def cmp_scores(k, m):
            r0 = pl.multiple_of(k * cch, cch)
            kh = kchi_ref[0, pl.ds(r0, cch), :]
            kl = kclo_ref[0, pl.ds(r0, cch), :]
            s = _dot(kh, qh) + (_dot(kh, ql) + _dot(kl, qh))
            c_end = (r0 + lax.broadcasted_iota(jnp.int32, (cch, 1), 0)) * CMP_STRIDE + (CMP_BLOCK - 1)
            s = jnp.where(c_end <= posr, s, NEG)
            cs_sc[k] = s
            return jnp.maximum(m, jnp.max(s, axis=0, keepdims=True))

        m_c = lax.fori_loop(0, nch, cmp_scores, jnp.full((1, r4), NEG, F32))

        def cmp_exp(k, l):
            e = jnp.exp(cs_sc[k] - m_c)
            cs_sc[k] = e
            return l + jnp.sum(e, axis=0, keepdims=True)

        l_c = lax.fori_loop(0, nch, cmp_exp, jnp.zeros((1, r4), F32))
        inv_c = jnp.where(m_c > 0.5 * NEG, 1.0 / jnp.maximum(l_c, 1e-30), 0.0)

        def cmp_out(k, o):
            r0 = pl.multiple_of(k * cch, cch)
            p = cs_sc[k] * inv_c
            p4_sc[pl.ds(8 + r0, cch), :] = p[:, 0:tq] + p[:, tq:2 * tq] + p[:, 2 * tq:3 * tq] + p[:, 3 * tq:4 * tq]
            return o + _dot(vct_ref[k], p.astype(BF16))

        o_cmp.append(lax.fori_loop(0, nch, cmp_out, jnp.zeros((LANES, r4), F32)))

        def cmp_zero(k, carry):
            r0 = pl.multiple_of(k * cch, cch)
            p4_sc[pl.ds(8 + r0, cch), :] = jnp.zeros((cch, tq), F32)
            return carry

        lax.fori_loop(nch, ncb // cch, cmp_zero, 0)
        per = SLC_BLOCK // CMP_STRIDE
        imp = p4_sc[pl.ds(8 - 1, nsp, stride=per), :]
        for k in range(per):
            imp = imp + p4_sc[pl.ds(8 + k, nsp, stride=per), :]
        selb = _topk_bias_t(imp, qpos, ns)
        qs_b = (qr_t * LOG2E).astype(BF16)
        for sl in range(nslab):
            piece = selb[sl * LANES:(sl + 1) * LANES, :]
            qat[sl, g, 0:LANES, :] = qs_b
            qat[sl, g, LANES:2 * LANES, :] = jnp.concatenate([piece] * 4, axis=1).astype(BF16)
        p_w = _masked_softmax_t(_dot(kw, qr_t.astype(BF16)), mask_w).astype(BF16)
        o_w = _dot(vwt_ref[wpage], p_w[0:PAGE])
        for i in range(1, wl // PAGE):
            o_w = o_w + _dot(vwt_ref[wpage + i], p_w[i * PAGE:(i + 1) * PAGE])
        o_win.append(o_w)

    m_sc[...] = jnp.full(m_sc.shape, 2.0 * NEG, F32)
    acc_sc[...] = jnp.zeros(acc_sc.shape, F32)
    ndiag = (t * tq + tq - 1) // tk

    def scores(jt, s_buf):
        k0 = pl.multiple_of(jt * tk, tk)
        kt = kaug_ref[pl.ds(k0, tk), :]
        sb = k0 // (SLAB_BLOCKS * SLC_BLOCK)
        for g in range(KV_HEADS):
            s_buf[g] = _dot(kt, qat[sb, g])

    def accumulate(jt, s_buf, causal):
        vt = vt_ref[jt]
        for g in range(KV_HEADS):
            s = s_buf[g]
            if causal:
                kpos = jt * tk + lax.broadcasted_iota(jnp.int32, (tk, 1), 0)
                s = jnp.where(kpos <= posr, s, NEG)
            m_old = m_sc[g]
            m_new = jnp.maximum(m_old, jnp.max(s, axis=0, keepdims=True))
            pe = jnp.exp2(s - m_new).astype(BF16)
            acc_sc[g] = jnp.exp2(m_old - m_new) * acc_sc[g] + _dot(vt, pe)
            m_sc[g] = m_new

    scores(0, sa_sc)

    def tile_pair(p, carry):
        jt = 2 * p
        scores(jt + 1, sb_sc)
        accumulate(jt, sa_sc, False)
        scores(jt + 2, sa_sc)
        accumulate(jt + 1, sb_sc, False)
        return carry

    npair = ndiag // 2
    lax.fori_loop(0, npair, tile_pair, 0)

    @pl.when(ndiag % 2 == 0)
    def _():
        accumulate(ndiag, sa_sc, True)

    @pl.when(ndiag % 2 == 1)
    def _():
        scores(ndiag, sb_sc)
        accumulate(ndiag - 1, sa_sc, False)
        accumulate(ndiag, sb_sc, True)

    sg_t = jax.nn.sigmoid(dtg_ref[0]).T
    outs = []
    for g in range(KV_HEADS):
        gate = [jnp.concatenate([sg_t[8 + 3 * (4 * g + r) + k:8 + 3 * (4 * g + r) + k + 1, :] for r in range(4)], axis=1)
                for k in range(3)]
        o_sel = acc_sc[g, 0:LANES, :] / acc_sc[g, LANES:LANES + 1, :]
        o_t = gate[0] * o_cmp[g] + gate[1] * o_sel + gate[2] * o_win[g]
        outs.append(o_t.T)
    _emit_heads(outs, tq, y_ref)


def _nsa(qn, qr, dtg, kchi, kclo, vct, kaug, vt, kwin, vwt, *, tq, ns, nsp, tk, wl):
    b, lq = qn.shape[:2]
    ncb = kchi.shape[1]
    cch = min(CMP_CHUNK, ncb)
    assert nsp * (SLC_BLOCK // CMP_STRIDE) >= ncb and ncb % cch == 0
    vct = vct[0].reshape(LANES, ncb // cch, cch).transpose(1, 0, 2)
    nslab = nsp // LANES
    kern = functools.partial(_nsa_kernel, tq=tq, ns=ns, nsp=nsp, tk=tk, wl=wl)
    rows = lambda n: pl.BlockSpec((1, tq, n), lambda i, t: (i, t, 0))
    whole = lambda arr: pl.BlockSpec(arr.shape, lambda i, t: (0,) * arr.ndim, pipeline_mode=pl.Buffered(1))
    return pl.pallas_call(
        kern,
        grid=(b, lq // tq),
        in_specs=[rows(512), rows(512), rows(128), whole(kchi), whole(kclo), whole(vct),
                  whole(kaug), whole(vt), whole(kwin), whole(vwt)],
        out_specs=rows(512),
        out_shape=jax.ShapeDtypeStruct((b, lq, 512), F32),
        scratch_shapes=[pltpu.VMEM((nslab, KV_HEADS, 2 * LANES, 4 * tq), BF16),
                        pltpu.VMEM((nsp * (SLC_BLOCK // CMP_STRIDE) + 16, tq), F32),
                        pltpu.VMEM((ncb // cch, cch, 4 * tq), F32),
                        pltpu.VMEM((KV_HEADS, tk, 4 * tq), F32),
                        pltpu.VMEM((KV_HEADS, tk, 4 * tq), F32),
                        pltpu.VMEM((KV_HEADS, 1, 4 * tq), F32),
                        pltpu.VMEM((KV_HEADS, LANES + SUM_ROWS, 4 * tq), F32)],
        compiler_params=_cparams(("arbitrary", "arbitrary")),
    )(qn, qr, dtg, kchi, kclo, vct, kaug, vt, kwin, vwt)


SAMPLE_PAGES_PER_STEP = 16
SAMPLE_TQ = 8


def _nsa_sample_kernel(pt_ref, *refs, npages, past, ns, nsp):
    pps = SAMPLE_PAGES_PER_STEP
    tq = SAMPLE_TQ
    pages = refs[:pps]
    (tail_ref, qn_ref, qr_ref, dtg_ref, w1_ref, pe_ref, b1_ref, w2_ref, ov_ref, cwin_ref, wtail_ref,
     y_ref, tsc, xs, ps, s_all, vt_all, bias_sc) = refs[pps:]
    j = pl.program_id(1)
    ngroups = npages // pps
    hb = PAGE // CMP_STRIDE
    scale = HEAD_DIM ** -0.5
    qr_aug = jnp.concatenate([_group_rows(qr_ref[0] * scale, g, tq) for g in range(KV_HEADS)], axis=0).astype(BF16)

    for i, pg in enumerate(pages):
        for kv in range(2):
            tsc[i, kv] = pg[0, 0, kv * LANES:(kv + 1) * LANES, :].T
            for pos in range(CMP_STRIDE):
                xs[kv, i * hb:(i + 1) * hb, pos * LANES:(pos + 1) * LANES] = tsc[i, kv, pl.ds(pos, hb, stride=CMP_STRIDE), :]
        page_idx = j * pps + i
        s_all[page_idx] = _dot(qr_aug, pg[0, 0, 256:384, :].astype(BF16))
        vt_all[page_idx] = pg[0, 0, 384:512, :].astype(BF16)
    rows = pps * hb
    g0 = pl.multiple_of(j * rows, rows)
    for kv in range(2):
        x = xs[kv]
        for s in range(2):
            ps[kv, s, pl.ds(g0, rows), :] = _dot((x + pe_ref[kv, s:s + 1, :]).astype(BF16), w1_ref[kv, s])

    @pl.when(j == ngroups - 1)
    def _():
        nh = npages * hb
        kcvc = []
        for kv in range(2):
            hid = b1_ref[kv:kv + 1, :] + ps[kv, 0] + pltpu.roll(ps[kv, 1], nh - 1, 0)
            kcvc.append(_dot(jax.nn.gelu(hid).astype(BF16), w2_ref[kv]))
        kc, vc = kcvc
        qpos = past + lax.broadcasted_iota(jnp.int32, (tq, 1), 0)
        pos_r = jnp.concatenate([qpos] * (4 * KV_HEADS), axis=0)
        nrow = 4 * KV_HEADS * tq
        s_all[npages] = _dot(qr_aug, tail_ref[0, 256:384, :].astype(BF16))
        vt_all[npages] = tail_ref[0, 384:512, :].astype(BF16)

        qn_aug = jnp.concatenate([_group_rows(qn_ref[0] * scale, g, tq) for g in range(KV_HEADS)], axis=0)
        c_end = lax.broadcasted_iota(jnp.int32, (1, nh), 1) * CMP_STRIDE + (CMP_BLOCK - 1)
        p_c = _masked_softmax(_dot_nt_hi(qn_aug, kc), c_end <= pos_r)
        o_c = _dot(p_c.astype(BF16), vc.astype(BF16))
        imps = []
        for g in range(KV_HEADS):
            base = 4 * tq * g
            p4 = (p_c[base:base + tq] + p_c[base + tq:base + 2 * tq]
                  + p_c[base + 2 * tq:base + 3 * tq] + p_c[base + 3 * tq:base + 4 * tq])
            imps.append(_importance(p4, ov_ref[...]))
        selb = _topk_bias(jnp.concatenate(imps, axis=0), jnp.concatenate([qpos] * KV_HEADS, axis=0), ns)
        lane_b = lax.broadcasted_iota(jnp.int32, (KV_HEADS * tq, LANES), 1)
        per_page = PAGE // SLC_BLOCK
        assert per_page == 2 and nsp >= per_page * (npages + 1)
        for pp in range(npages + 1):
            lo = jnp.broadcast_to(selb[:, 2 * pp:2 * pp + 1], (KV_HEADS * tq, LANES))
            hi = jnp.broadcast_to(selb[:, 2 * pp + 1:2 * pp + 2], (KV_HEADS * tq, LANES))
            bias_sc[pp] = jnp.where(lane_b < SLC_BLOCK, lo, hi)

        def biased(pidx):
            b = bias_sc[pidx]
            b_r = jnp.concatenate([b[0:tq]] * 4 + [b[tq:2 * tq]] * 4, axis=0)
            return s_all[pidx] + b_r

        def p1(pidx, m):
            s = biased(pidx)
            s_all[pidx] = s
            return jnp.maximum(m, s)

        m = lax.fori_loop(0, npages, p1, jnp.full((nrow, LANES), 2.0 * NEG, F32))
        kpos_t = past + lax.broadcasted_iota(jnp.int32, (1, PAGE), 1)
        s_t = jnp.where(kpos_t <= pos_r, biased(npages), NEG)
        s_all[npages] = s_t
        mrow = jnp.max(jnp.maximum(m, s_t), axis=-1, keepdims=True)

        def p2(pidx, carry):
            l, acc = carry
            p = jnp.exp(s_all[pidx] - mrow)
            return l + p, acc + _dot_nt(p.astype(BF16), vt_all[pidx])

        zero = jnp.zeros((nrow, LANES), F32)
        l, acc = p2(npages, lax.fori_loop(0, npages, p2, (zero, zero), unroll=8))
        o_s = acc / jnp.sum(l, axis=-1, keepdims=True)

        wbuf = cwin_ref.shape[3]
        kw_b = cwin_ref[0, 0, 0:128, :].astype(BF16)
        vw_b = cwin_ref[0, 0, 128:256, :].astype(BF16)
        kt_b = wtail_ref[0, 0:128, :].astype(BF16)
        vt_b = wtail_ref[0, 128:256, :].astype(BF16)
        s_w = jnp.concatenate([_dot(qr_aug, kw_b), _dot(qr_aug, kt_b)], axis=1)
        kposw = jnp.concatenate([past - wbuf + lax.broadcasted_iota(jnp.int32, (1, wbuf), 1), kpos_t], axis=1)
        dpos = pos_r - kposw
        p_w = _masked_softmax(s_w, (dpos >= 0) & (dpos < WINDOW) & (kposw >= 0))
        o_w = _dot_nt(p_w[:, 0:wbuf].astype(BF16), vw_b) + _dot_nt(p_w[:, wbuf:].astype(BF16), vt_b)

        sg = jax.nn.sigmoid(dtg_ref[0])
        outs = []
        for g in range(KV_HEADS):
            sl = slice(4 * tq * g, 4 * tq * (g + 1))
            outs.append(_gate_cols(sg, g, 0, tq) * o_c[sl] + _gate_cols(sg, g, 1, tq) * o_s[sl]
                        + _gate_cols(sg, g, 2, tq) * o_w[sl])
        _emit_heads(outs, tq, y_ref)


def _nsa_sample(layer, cache_t, table, tail_t, qn, qr, dtg, w1e, pee, b1e, w2e, ov, cwin_t, wtail_t, *, past, ns):
    b, npages = table.shape
    pps = SAMPLE_PAGES_PER_STEP
    tq = SAMPLE_TQ
    nsp = ov.shape[1]
    nh = npages * (PAGE // CMP_STRIDE)
    nrow = 4 * KV_HEADS * tq
    kern = functools.partial(_nsa_sample_kernel, npages=npages, past=past, ns=ns, nsp=nsp)
    const = lambda arr: pl.BlockSpec(arr.shape, lambda i, j, pt: (0,) * arr.ndim, pipeline_mode=pl.Buffered(1))
    perb = lambda arr: pl.BlockSpec((1,) + arr.shape[1:], lambda i, j, pt: (i,) + (0,) * (arr.ndim - 1))

    def page_spec(k):
        return pl.BlockSpec((1, 1, 512, PAGE), lambda i, j, pt: (layer, pt[i, j * pps + k], 0, 0))

    return pl.pallas_call(
        kern,
        grid_spec=pltpu.PrefetchScalarGridSpec(
            num_scalar_prefetch=1,
            grid=(b, npages // pps),
            in_specs=[page_spec(k) for k in range(pps)] + [
                perb(tail_t), perb(qn), perb(qr), perb(dtg), const(w1e), const(pee), const(b1e), const(w2e),
                const(ov),
                pl.BlockSpec((1, 1) + cwin_t.shape[2:], lambda i, j, pt: (layer, i, 0, 0)), perb(wtail_t)],
            out_specs=pl.BlockSpec((1, tq, 512), lambda i, j, pt: (i, 0, 0)),
            scratch_shapes=[pltpu.VMEM((pps, 2, PAGE, LANES), F32),
                            pltpu.VMEM((2, pps * 8, CMP_STRIDE * LANES), F32),
                            pltpu.VMEM((2, 2, nh, 256), F32),
                            pltpu.VMEM((npages + 1, nrow, LANES), F32),
                            pltpu.VMEM((npages + 1, LANES, PAGE), BF16),
                            pltpu.VMEM((npages + 1, KV_HEADS * tq, LANES), F32)]),
        out_shape=jax.ShapeDtypeStruct((b, tq, 512), F32),
        compiler_params=_cparams(("arbitrary", "arbitrary")),
    )(table, *([cache_t] * pps), tail_t, qn, qr, dtg, w1e, pee, b1e, w2e, ov, cwin_t, wtail_t)


def _oproj_kernel(x_ref, ys_ref, yn_ref, w_ref, g_ref, r_ref, x1_ref, h_ref, lg_ref):
    a = jnp.concatenate([ys_ref[...], yn_ref[...]], axis=1).astype(BF16)
    x1 = x_ref[...] + _dot(a, w_ref[...])
    x1_ref[...] = x1
    h = _rms(x1, g_ref[...])
    h_ref[...] = h.astype(h_ref.dtype)
    lg_ref[...] = _dot_hi(h, r_ref[...])


def _oproj(x, ys, yn, w, g, router, tm, h_dtype):
    t = x.shape[0]
    row = lambda n: pl.BlockSpec((tm, n), lambda i: (i, 0))
    full = lambda a: pl.BlockSpec(a.shape, lambda i: (0,) * a.ndim)
    return pl.pallas_call(
        _oproj_kernel,
        grid=(t // tm,),
        in_specs=[row(D_MODEL), row(512), row(512), full(w), full(g), full(router)],
        out_specs=[row(D_MODEL), row(D_MODEL), row(LANES)],
        out_shape=[jax.ShapeDtypeStruct((t, D_MODEL), F32), jax.ShapeDtypeStruct((t, D_MODEL), h_dtype),
                   jax.ShapeDtypeStruct((t, LANES), F32)],
        compiler_params=_cparams(("arbitrary",)),
    )(x, ys, yn, w, g, router)


def _ffn_kernel(h_ref, x1_ref, wg_ref, wu_ref, wd_ref, o_ref):
    f = pl.program_id(1)

    @pl.when(f == 0)
    def _():
        o_ref[...] = x1_ref[...]

    hb = h_ref[...]
    act = _silu(_dot(hb, wg_ref[...])) * _dot(hb, wu_ref[...])
    o_ref[...] += _dot(act.astype(BF16), wd_ref[...])


def _ffn(h, x1, wg, wu, wd, tm, tf):
    t = h.shape[0]
    dff = wg.shape[1]
    return pl.pallas_call(
        _ffn_kernel,
        grid=(t // tm, dff // tf),
        in_specs=[pl.BlockSpec((tm, D_MODEL), lambda i, f: (i, 0)),
                  pl.BlockSpec((tm, D_MODEL), lambda i, f: (i, 0)),
                  pl.BlockSpec((D_MODEL, tf), lambda i, f: (0, f)),
                  pl.BlockSpec((D_MODEL, tf), lambda i, f: (0, f)),
                  pl.BlockSpec((tf, D_MODEL), lambda i, f: (f, 0))],
        out_specs=pl.BlockSpec((tm, D_MODEL), lambda i, f: (i, 0)),
        out_shape=jax.ShapeDtypeStruct((t, D_MODEL), F32),
        compiler_params=_cparams(("arbitrary", "arbitrary")),
    )(h, x1, wg, wu, wd)


def _moe_kernel(be_ref, nb_ref, xb_ref, wg_ref, wu_ref, wd_ref, o_ref, xb16):
    b = pl.program_id(0)
    f = pl.program_id(1)

    @pl.when(f == 0)
    def _():
        o_ref[...] = jnp.zeros(o_ref.shape, F32)
        xb16[...] = xb_ref[...].astype(BF16)

    @pl.when(b < nb_ref[0])
    def _():
        hb = xb16[...]
        act = _silu(_dot(hb, wg_ref[0].astype(BF16))) * _dot(hb, wu_ref[0].astype(BF16))
        o_ref[...] += _dot(act.astype(BF16), wd_ref[0].astype(BF16))


def _moe(blk_e, nblk_used, xb, wg, wu, wd, tmb, tf):
    n_blk = xb.shape[0] // tmb
    dex = wg.shape[2]
    nf = dex // tf

    def feff(b, f, nb):
        return jnp.where(b < nb[0], f, nf - 1)

    return pl.pallas_call(
        _moe_kernel,
        grid_spec=pltpu.PrefetchScalarGridSpec(
            num_scalar_prefetch=2,
            grid=(n_blk, nf),
            in_specs=[pl.BlockSpec((tmb, D_MODEL), lambda b, f, be, nb: (b, 0)),
                      pl.BlockSpec((1, D_MODEL, tf), lambda b, f, be, nb: (be[b], 0, feff(b, f, nb))),
                      pl.BlockSpec((1, D_MODEL, tf), lambda b, f, be, nb: (be[b], 0, feff(b, f, nb))),
                      pl.BlockSpec((1, tf, D_MODEL), lambda b, f, be, nb: (be[b], feff(b, f, nb), 0))],
            out_specs=pl.BlockSpec((tmb, D_MODEL), lambda b, f, be, nb: (b, 0)),
            scratch_shapes=[pltpu.VMEM((tmb, D_MODEL), BF16)]),
        out_shape=jax.ShapeDtypeStruct((n_blk * tmb, D_MODEL), F32),
        compiler_params=_cparams(("arbitrary", "arbitrary")),
    )(blk_e, nblk_used, xb, wg, wu, wd)


def _moe_layer(h, x1, logits, wg, wu, wd, tmb, tf):
    t = h.shape[0]
    top_v, top_e = lax.top_k(logits[:, :N_EXPERTS], TOP_K)
    gate = jax.nn.softmax(top_v, axis=-1)
    e_flat = top_e.reshape(-1)
    tok = jnp.repeat(jnp.arange(t, dtype=jnp.int32), TOP_K)
    order = jnp.argsort(e_flat)
    e_s = e_flat[order]
    tok_s = tok[order]
    counts = jnp.bincount(e_flat, length=N_EXPERTS)
    starts = jnp.cumsum(counts) - counts
    padded = (counts + tmb - 1) // tmb * tmb
    pends = jnp.cumsum(padded)
    pstarts = pends - padded
    dest_s = (pstarts[e_s] + jnp.arange(t * TOP_K, dtype=jnp.int32) - starts[e_s]).astype(jnp.int32)
    n_blk = -(-(t * TOP_K) // tmb) + N_EXPERTS
    buf_tok = jnp.full((n_blk * tmb,), t, dtype=jnp.int32).at[dest_s].set(tok_s)
    blk_e = jnp.minimum(jnp.searchsorted(pends, jnp.arange(n_blk, dtype=jnp.int32) * tmb, side='right'),
                        N_EXPERTS - 1).astype(jnp.int32)
    nblk_used = (pends[-1] // tmb).astype(jnp.int32).reshape(1)
    h_pad = jnp.concatenate([h, jnp.zeros((1, D_MODEL), h.dtype)], axis=0)
    out = _moe(blk_e, nblk_used, h_pad[buf_tok], wg, wu, wd, tmb, tf)
    dest = jnp.zeros((t * TOP_K,), jnp.int32).at[order].set(dest_s).reshape(t, TOP_K)
    return x1 + out[dest[:, 0]] * gate[:, 0:1] + out[dest[:, 1]] * gate[:, 1:2]


def _final_kernel(x_ref, g_ref, o_ref):
    o_ref[...] = _rms(x_ref[...], g_ref[...])


def _final_norm(x, g, tm):
    t = x.shape[0]
    return pl.pallas_call(
        _final_kernel,
        grid=(t // tm,),
        in_specs=[pl.BlockSpec((tm, D_MODEL), lambda i: (i, 0)), pl.BlockSpec(g.shape, lambda i: (0, 0))],
        out_specs=pl.BlockSpec((tm, D_MODEL), lambda i: (i, 0)),
        out_shape=jax.ShapeDtypeStruct((t, D_MODEL), F32),
        compiler_params=_cparams(("arbitrary",)),
    )(x, g)


def _rope_tables(pos):
    inv = 1.0 / (ROPE_THETA ** (jnp.arange(0, HEAD_DIM, 2, dtype=F32) / HEAD_DIM))
    ang = pos.astype(F32)[:, None] * inv[None, :]
    cos = jnp.cos(ang)
    sin = jnp.sin(ang)
    return jnp.concatenate([cos, cos, cos, cos], axis=1), jnp.concatenate([-sin, sin, -sin, sin], axis=1)


def _overlap(ncb, nsp, ns):
    c = jnp.arange(ncb, dtype=jnp.int32)[:, None] * CMP_STRIDE
    s = jnp.arange(nsp, dtype=jnp.int32)[None, :] * SLC_BLOCK
    ov = (c <= s + SLC_BLOCK - 1) & (c + CMP_BLOCK - 1 >= s) & (jnp.arange(nsp)[None, :] < ns)
    return ov.astype(BF16)


def _pad_lanes(v, n):
    return jnp.zeros((1, n), F32).at[0, :v.shape[0]].set(v.astype(F32))


def kernel(x_prompt, x_sample, cache_kv, cache_win, state_ssm, state_conv, page_table, ln_mix, w_in, conv_w,
           conv_b, dt_bias, a_log, d_skip, ssd_norm, cmp_pos, cmp_w1, cmp_b1, cmp_w2, w_out, ln_ffn,
           ffn_w_gate, ffn_w_up, ffn_w_down, moe_router, moe_w_gate, moe_w_up, moe_w_down, ln_final):
    depth = w_in.shape[0]
    bsz, seq = x_prompt.shape[:2]
    dbsz, dseq = x_sample.shape[:2]
    n_pages = page_table.shape[1]
    past = n_pages * cache_kv.shape[2]
    win_buf = cache_win.shape[2]
    npool = cache_kv.shape[1]
    assert bsz == 1 and cache_kv.shape[2] == PAGE and seq % PAGE == 0 and seq >= WINDOW + PAGE
    assert win_buf == WINDOW and past % PAGE == 0
    assert not any(past <= CMP_STRIDE * c + CMP_BLOCK - 1 <= past + dseq - 1
                   for c in range(past // CMP_STRIDE - 2, past // CMP_STRIDE + 2))

    ts = dbsz * dseq
    ns_p = seq // SLC_BLOCK
    ns_s = -(-(past + dseq) // SLC_BLOCK)
    nsp_p = -(-ns_p // LANES) * LANES
    nsp_s = -(-ns_s // LANES) * LANES
    assert dseq <= SAMPLE_TQ and n_pages % SAMPLE_PAGES_PER_STEP == 0
    cache_t = jnp.transpose(cache_kv, (0, 1, 3, 4, 5, 2)).reshape(depth, npool, 512, PAGE)
    cwin_t = jnp.transpose(cache_win, (0, 1, 3, 4, 5, 2)).reshape(depth, dbsz, 256, win_buf)

    cs_p, sn_p = _rope_tables(jnp.arange(seq, dtype=jnp.int32))
    cs_s, sn_s = _rope_tables(past + jnp.arange(ts, dtype=jnp.int32) % dseq)
    ov_s = _overlap(past // CMP_STRIDE, nsp_s, ns_s)
    ident = jnp.arange(seq // PAGE, dtype=jnp.int32).reshape(1, -1)

    xp = x_prompt.reshape(seq, D_MODEL)
    xs = x_sample.reshape(ts, D_MODEL)
    outs = {k: [] for k in ('kv_p', 'kv_s', 'win_p', 'win_s', 'ssm_p', 'ssm_s', 'conv_p', 'conv_s')}

    for l in range(depth):
        w = w_in[l]
        wr = jnp.concatenate([w[:, :1536], w[:, 1544:2824], w[:, 1536:1544], w[:, 2824:2848],
                              jnp.zeros((D_MODEL, C_END - 2848), F32)], axis=1).astype(BF16)
        g_mix = ln_mix[l].reshape(1, -1)
        cw8 = jnp.zeros((8, CONV_DIM), F32).at[:CONV_W].set(conv_w[l])
        cb = conv_b[l].reshape(1, -1)
        dtb = _pad_lanes(dt_bias[l], LANES)
        a_neg = _pad_lanes(-jnp.exp(a_log[l].astype(F32)), LANES)
        dsk = jnp.repeat(d_skip[l].astype(F32), SSD_HEAD_DIM).reshape(1, -1)
        ng = ssd_norm[l].reshape(1, -1)
        w1 = cmp_w1[l].reshape(2, 2, CMP_STRIDE, HEAD_DIM, CMP_HIDDEN)
        eye = jnp.eye(KV_HEADS, dtype=F32)
        w1e = jnp.einsum('vspdh,gk->vspgdkh', w1, eye).reshape(2, 2, CMP_STRIDE * LANES, 256).astype(BF16)
        pe = cmp_pos[l].reshape(2, 2, CMP_STRIDE, 1, HEAD_DIM)
        pee = jnp.broadcast_to(pe, (2, 2, CMP_STRIDE, KV_HEADS, HEAD_DIM)).reshape(2, 2, CMP_STRIDE * LANES)
        b1e = jnp.concatenate([cmp_b1[l], cmp_b1[l]], axis=1)
        w2e = jnp.einsum('vhd,gk->vghkd', cmp_w2[l], eye).reshape(2, 256, LANES).astype(BF16)
        wo = w_out[l].astype(BF16)
        g_ffn = ln_ffn[l].reshape(1, -1)
        if l % 2 == 0:
            router = jnp.zeros((D_MODEL, LANES), F32)
        else:
            router = jnp.zeros((D_MODEL, LANES), F32).at[:, :N_EXPERTS].set(moe_router[l // 2])

        z, xbc, dtg, q, qr, kv, win = _front(xp, g_mix, wr, cs_p, sn_p, 256)
        y_ssd, hist, hfin = _ssd(z[None], xbc[None], dtg[None], jnp.zeros((1, 8, CONV_DIM), F32),
                                 jnp.zeros((1, D_SSD, SSD_STATE), F32), cw8, cb, dtb, a_neg, dsk, ng, SSD_CHUNK)
        kv_pages = kv.reshape(seq // PAGE, PAGE, 512)
        kchi, kclo, vct = _compress(kv_pages, ident, w1e, pee, b1e, w2e)
        kaug, vt, kwin, vwt = _kvfmt(kv, win, NSA_TK)
        y_nsa = _nsa(q[None], qr[None], dtg[None], kchi, kclo, vct, kaug, vt, kwin, vwt,
                     tq=128, ns=ns_p, nsp=nsp_p, tk=NSA_TK, wl=WINDOW + 128)
        h_dtype = BF16 if l % 2 == 0 else F32
        x1, h, logits = _oproj(xp, y_ssd[0], y_nsa[0], wo, g_ffn, router, 512, h_dtype)
        outs['kv_p'].append(kv.reshape(1, seq, 4, KV_HEADS, HEAD_DIM))
        outs['win_p'].append(win[seq - WINDOW:].reshape(1, WINDOW, 2, KV_HEADS, HEAD_DIM))
        outs['ssm_p'].append(hfin.reshape(1, SSD_HEADS, SSD_HEAD_DIM, SSD_STATE))
        outs['conv_p'].append(hist[:, 5:8])

        zs, xbcs, dtgs, qs, qrs, kvs, wins = _front(xs, g_mix, wr, cs_s, sn_s, ts)
        padl = lambda v: jnp.pad(v.reshape(dbsz, dseq, -1), ((0, 0), (0, SSD_CHUNK - dseq), (0, 0)))
        hist8 = jnp.pad(state_conv[l], ((0, 0), (8 - (CONV_W - 1), 0), (0, 0)))
        y_ssd_s, hist_s, hfin_s = _ssd(padl(zs), padl(xbcs), padl(dtgs), hist8,
                                       state_ssm[l].reshape(dbsz, D_SSD, SSD_STATE),
                                       cw8, cb, dtb, a_neg, dsk, ng, dseq)
        tail_t = jnp.pad(jnp.transpose(kvs.reshape(dbsz, dseq, 512), (0, 2, 1)), ((0, 0), (0, 0), (0, PAGE - dseq)))
        wtail_t = jnp.pad(jnp.transpose(wins.reshape(dbsz, dseq, 256), (0, 2, 1)), ((0, 0), (0, 0), (0, PAGE - dseq)))
        padq = lambda v: jnp.pad(v.reshape(dbsz, dseq, -1), ((0, 0), (0, SAMPLE_TQ - dseq), (0, 0)))
        y_nsa_s = _nsa_sample(l, cache_t, page_table, tail_t, padq(qs), padq(qrs), padq(dtgs), w1e, pee, b1e, w2e,
                              ov_s, cwin_t, wtail_t, past=past, ns=ns_s)
        y_nsa_s = y_nsa_s[:, :dseq].reshape(ts, D_NSA)
        x1s, hs, logits_s = _oproj(xs, y_ssd_s[:, :dseq].reshape(ts, D_SSD), y_nsa_s, wo, g_ffn, router, ts, h_dtype)
        outs['kv_s'].append(kvs.reshape(dbsz, dseq, 4, KV_HEADS, HEAD_DIM))
        win_all = jnp.concatenate([cache_win[l], wins.reshape(dbsz, dseq, 2, KV_HEADS, HEAD_DIM)], axis=1)
        outs['win_s'].append(win_all[:, dseq:])
        outs['ssm_s'].append(hfin_s.reshape(dbsz, SSD_HEADS, SSD_HEAD_DIM, SSD_STATE))
        outs['conv_s'].append(hist_s[:, 5:8])

        if l % 2 == 0:
            wg = ffn_w_gate[l // 2].astype(BF16)
            wu = ffn_w_up[l // 2].astype(BF16)
            wd = ffn_w_down[l // 2].astype(BF16)
            xp = _ffn(h, x1, wg, wu, wd, 1024, 256)
            xs = _ffn(hs, x1s, wg, wu, wd, ts, 256)
        else:
            wg = moe_w_gate[l // 2]
            wu = moe_w_up[l // 2]
            wd = moe_w_down[l // 2]
            xp = _moe_layer(h, x1, logits, wg, wu, wd, 1024, 512)
            xs = _moe_layer(hs, x1s, logits_s, wg, wu, wd, 128, 512)

    g_fin = ln_final.reshape(1, -1)
    y_prompt = _final_norm(xp, g_fin, 1024).reshape(bsz, seq, D_MODEL)
    y_sample = _final_norm(xs, g_fin, ts).reshape(dbsz, dseq, D_MODEL)
    st = lambda k: jnp.stack(outs[k])
    return (y_prompt, y_sample, st('kv_p'), st('kv_s'), st('win_p'), st('win_s'),
            st('ssm_p'), st('ssm_s'), st('conv_p'), st('conv_s'))
```

```python
import functools
import math

import jax
import jax.numpy as jnp
from jax import lax
from jax.experimental import pallas as pl
from jax.experimental.pallas import tpu as pltpu

F32 = jnp.float32
BF16 = jnp.bfloat16

D_MODEL = 1024
SSD_HEADS = 8
SSD_HEAD_DIM = 64
D_SSD = 512
SSD_GROUPS = 2
SSD_STATE = 128
CONV_W = 4
CONV_DIM = 1024
SSD_CHUNK = 128
NSA_HEADS = 8
HEAD_DIM = 64
KV_HEADS = 2
D_NSA = 512
CMP_BLOCK = 32
CMP_STRIDE = 16
CMP_HIDDEN = 128
SLC_BLOCK = 64
TOP_N = 16
WINDOW = 512
ROPE_THETA = 10000.0
FORCE_BONUS = 1.0e4
N_EXPERTS = 8
TOP_K = 2
EPS = 1e-6

PAGE = 128
LANES = 128
LOG2E = 1.4426950408889634
NEG = -1.0e30
SLAB_BLOCKS = 128
VMEM_LIMIT = 56 * 1024 * 1024

C_Z, C_XBC, C_Q, C_KV, C_DT, C_END = 0, 512, 1536, 2048, 2816, 2944


def _cparams(sem):
    return pltpu.CompilerParams(dimension_semantics=sem, vmem_limit_bytes=VMEM_LIMIT)


def _dot(a, b):
    return jnp.dot(a, b, preferred_element_type=F32)


def _dot_nt(a, b):
    return lax.dot_general(a, b, (((1,), (1,)), ((), ())), preferred_element_type=F32)


def _split3(x):
    h1 = x.astype(BF16)
    r1 = x - h1.astype(F32)
    h2 = r1.astype(BF16)
    h3 = (r1 - h2.astype(F32)).astype(BF16)
    return h1, h2, h3


def _dot_nt_hi(a, b):
    a1, a2, _ = _split3(a)
    b1, b2, _ = _split3(b)
    return _dot_nt(a1, b1) + (_dot_nt(a1, b2) + _dot_nt(a2, b1))


def _dot_hi(a, b):
    a1, a2, _ = _split3(a)
    b1, b2, _ = _split3(b)
    return _dot(a1, b1) + (_dot(a1, b2) + _dot(a2, b1))


def _silu(x):
    return x * jax.nn.sigmoid(x)


def _rms(x, g):
    return x * lax.rsqrt(jnp.mean(x * x, axis=-1, keepdims=True) + EPS) * g


def _front_kernel(x_ref, g_ref, w_ref, cs_ref, sn_ref,
                  z_ref, xbc_ref, dtg_ref, q_ref, qr_ref, kv_ref, win_ref):
    xn = _rms(x_ref[...], g_ref[...])
    p = _dot(xn.astype(BF16), w_ref[...])
    cs = cs_ref[...]
    sn = sn_ref[...]
    lane = lax.broadcasted_iota(jnp.int32, cs.shape, 1)
    first = (lane % HEAD_DIM) < (HEAD_DIM // 2)

    def rope(v):
        sw = jnp.where(first, pltpu.roll(v, LANES - HEAD_DIM // 2, 1), pltpu.roll(v, HEAD_DIM // 2, 1))
        return v * cs + sw * sn

    z_ref[...] = p[:, C_Z:C_XBC]
    xbc_ref[...] = p[:, C_XBC:C_Q]
    dtg_ref[...] = p[:, C_DT:C_END]
    q_ref[...] = p[:, C_Q:C_KV]
    for s in range(4):
        qr_ref[:, s * LANES:(s + 1) * LANES] = rope(p[:, C_Q + s * LANES:C_Q + (s + 1) * LANES])
    kv_ref[:, 0:256] = p[:, C_KV:C_KV + 256]
    kv_ref[:, 256:384] = rope(p[:, C_KV + 256:C_KV + 384])
    kv_ref[:, 384:512] = p[:, C_KV + 384:C_KV + 512]
    win_ref[:, 0:128] = rope(p[:, C_KV + 512:C_KV + 640])
    win_ref[:, 128:256] = p[:, C_KV + 640:C_KV + 768]


def _front(x, g, w, cs, sn, tm):
    t = x.shape[0]
    row = lambda n: pl.BlockSpec((tm, n), lambda i: (i, 0))
    full = lambda a: pl.BlockSpec(a.shape, lambda i: (0,) * a.ndim)
    outs = [512, 1024, 128, 512, 512, 512, 256]
    return pl.pallas_call(
        _front_kernel,
        grid=(t // tm,),
        in_specs=[row(D_MODEL), full(g), full(w), row(LANES), row(LANES)],
        out_specs=[row(n) for n in outs],
        out_shape=[jax.ShapeDtypeStruct((t, n), F32) for n in outs],
        compiler_params=_cparams(("arbitrary",)),
    )(x, g, w, cs, sn)


def _expand_heads(v):
    r = v.shape[0]
    lane = lax.broadcasted_iota(jnp.int32, (r, LANES), 1)
    outs = []
    for k in range(4):
        a = jnp.broadcast_to(v[:, 2 * k:2 * k + 1], (r, LANES))
        b = jnp.broadcast_to(v[:, 2 * k + 1:2 * k + 2], (r, LANES))
        outs.append(jnp.where(lane < SSD_HEAD_DIM, a, b))
    return jnp.concatenate(outs, axis=1)


def _ssd_kernel(z_ref, xbc_ref, dtg_ref, hist_ref, h0_ref, cw_ref, cb_ref, dtb_ref, a_ref, dsk_ref, ng_ref,
                y_ref, hist_o_ref, hfin_ref, xpad, ht, *, nchunks, valid_last):
    c = pl.program_id(1)
    q = SSD_CHUNK

    @pl.when(c == 0)
    def _():
        xpad[0:8, :] = hist_ref[0]
        ht[...] = h0_ref[0].T

    xpad[8:8 + q, :] = xbc_ref[0]
    conv = cb_ref[...]
    for k in range(CONV_W):
        conv = conv + xpad[5 + k:5 + k + q, :] * cw_ref[k:k + 1, :]
    xc = _silu(conv)
    xs = xc[:, 0:D_SSD]
    bm = xc[:, D_SSD:D_SSD + 256]
    cm = xc[:, D_SSD + 256:D_SSD + 512]

    row = lax.broadcasted_iota(jnp.int32, (q, q), 0)
    col = lax.broadcasted_iota(jnp.int32, (q, q), 1)
    nvalid = jnp.where(c == nchunks - 1, valid_last, q)
    xdt_in = dtg_ref[0] + dtb_ref[...]
    dt = jnp.maximum(xdt_in, 0.0) + jnp.log1p(jnp.exp(-jnp.abs(xdt_in)))
    dt = jnp.where(row < nvalid, dt, 0.0)
    dta = dt * a_ref[...]
    causal = row >= col
    cum = jnp.dot(causal.astype(F32), dta, preferred_element_type=F32, precision=lax.Precision.HIGHEST)
    cum_t = cum.T
    ecum = jnp.exp(cum)
    toend = jnp.exp(cum[q - 1:q, :] - cum)
    dt_e = _expand_heads(dt)
    ecum_e = _expand_heads(ecum)
    toend_e = _expand_heads(toend)
    xdt = xs * dt_e
    xdt_b = xdt.astype(BF16)
    xw_b = (xdt * toend_e).astype(BF16)
    lane = lax.broadcasted_iota(jnp.int32, (q, LANES), 1)

    y_intra = []
    y_inter = []
    for g in range(SSD_GROUPS):
        bg = bm[:, g * 128:(g + 1) * 128]
        cg_b = cm[:, g * 128:(g + 1) * 128].astype(BF16)
        bg_b = bg.astype(BF16)
        cb = _dot_nt(cg_b, bg_b)
        htg = ht[:, g * 256:(g + 1) * 256]
        y_inter.append(_dot(cg_b, htg.astype(BF16)))
        for kk in range(2):
            slab = xdt_b[:, (2 * g + kk) * LANES:(2 * g + kk + 1) * LANES]
            res = []
            for hh in range(2):
                h = 4 * g + 2 * kk + hh
                seg = cum[:, h:h + 1] - cum_t[h:h + 1, :]
                decay = jnp.exp(jnp.where(causal, seg, NEG))
                res.append(_dot((decay * cb).astype(BF16), slab))
            y_intra.append(jnp.where(lane < SSD_HEAD_DIM, res[0], res[1]))
        s_new = _dot(bg.T.astype(BF16), xw_b[:, g * 256:(g + 1) * 256])
        ht[:, g * 256:(g + 1) * 256] = htg * ecum_e[q - 1:q, g * 256:(g + 1) * 256] + s_new

    y = jnp.concatenate(y_intra, axis=1) + jnp.concatenate(y_inter, axis=1) * ecum_e + dsk_ref[...] * xs
    y = y * _silu(z_ref[0])
    y_ref[0] = _rms(y, ng_ref[...])

    xpad[0:8, :] = xpad[q:q + 8, :]

    @pl.when(c == nchunks - 1)
    def _():
        hist_o_ref[0] = xpad[valid_last:valid_last + 8, :]
        hfin_ref[0] = ht[...].T


def _ssd(z, xbc, dtg, hist8, h0, cw8, cb, dtb, a, dsk, ng, valid_last):
    b, l = z.shape[:2]
    nchunks = l // SSD_CHUNK
    rows = lambda n: pl.BlockSpec((1, SSD_CHUNK, n), lambda i, c: (i, c, 0))
    perb = lambda r, n: pl.BlockSpec((1, r, n), lambda i, c: (i, 0, 0))
    full = lambda arr: pl.BlockSpec(arr.shape, lambda i, c: (0,) * arr.ndim)
    kern = functools.partial(_ssd_kernel, nchunks=nchunks, valid_last=valid_last)
    return pl.pallas_call(
        kern,
        grid=(b, nchunks),
        in_specs=[rows(512), rows(1024), rows(128), perb(8, 1024), perb(512, 128),
                  full(cw8), full(cb), full(dtb), full(a), full(dsk), full(ng)],
        out_specs=[rows(512), perb(8, 1024), perb(512, 128)],
        out_shape=[jax.ShapeDtypeStruct((b, l, 512), F32),
                   jax.ShapeDtypeStruct((b, 8, 1024), F32),
                   jax.ShapeDtypeStruct((b, 512, 128), F32)],
        scratch_shapes=[pltpu.VMEM((SSD_CHUNK + 8, CONV_DIM), F32), pltpu.VMEM((SSD_STATE, D_SSD), F32)],
        compiler_params=_cparams(("arbitrary", "arbitrary")),
    )(z, xbc, dtg, hist8, h0, cw8, cb, dtb, a, dsk, ng)


CMP_CHUNK = 256
SUM_ROWS = 16
NSA_TK = 512
CMP_GROUP_PAGES = 16


def _compress_kernel(pt_ref, pgk_ref, pgv_ref, w1_ref, pe_ref, b1_ref, w2_ref, w2t_ref, kchi_ref, kclo_ref, vct_ref,
                     xs, ps, *, npages):
    p = pl.program_id(1)
    sub = p % CMP_GROUP_PAGES
    hb = PAGE // CMP_STRIDE
    r0 = pl.multiple_of(sub * hb, hb)
    for kv, pg_ref in enumerate((pgk_ref, pgv_ref)):
        for pos in range(CMP_STRIDE):
            xs[kv, pl.ds(r0, hb), pos * LANES:(pos + 1) * LANES] = pg_ref[0, pl.ds(pos, hb, stride=CMP_STRIDE), :]

    @pl.when(sub == CMP_GROUP_PAGES - 1)
    def _():
        rows = CMP_GROUP_PAGES * hb
        g0 = pl.multiple_of((p // CMP_GROUP_PAGES) * rows, rows)
        for kv in range(2):
            x = xs[kv]
            for s in range(2):
                ps[kv, s, pl.ds(g0, rows), :] = _dot((x + pe_ref[kv, s:s + 1, :]).astype(BF16), w1_ref[kv, s])

    @pl.when(p == npages - 1)
    def _():
        nh = npages * hb
        act = [jax.nn.gelu(b1_ref[kv:kv + 1, :] + ps[kv, 0] + pltpu.roll(ps[kv, 1], nh - 1, 0)).astype(BF16)
               for kv in range(2)]
        kc = _dot(act[0], w2_ref[0])
        hi = kc.astype(BF16)
        kchi_ref[0] = hi
        kclo_ref[0] = (kc - hi.astype(F32)).astype(BF16)
        vct_ref[0] = _dot_nt(w2t_ref[...], act[1]).astype(BF16)


def _compress(pages, table, w1e, pee, b1e, w2e):
    b, npages = table.shape
    w2t = w2e[1].T
    nh = npages * (PAGE // CMP_STRIDE)
    kern = functools.partial(_compress_kernel, npages=npages)
    full = lambda arr: pl.BlockSpec(arr.shape, lambda i, p, pt: (0,) * arr.ndim)
    return pl.pallas_call(
        kern,
        grid_spec=pltpu.PrefetchScalarGridSpec(
            num_scalar_prefetch=1,
            grid=(b, npages),
            in_specs=[pl.BlockSpec((1, PAGE, LANES), lambda i, p, pt: (pt[i, p], 0, 0)),
                      pl.BlockSpec((1, PAGE, LANES), lambda i, p, pt: (pt[i, p], 0, 1)),
                      full(w1e), full(pee), full(b1e), full(w2e), full(w2t)],
            out_specs=[pl.BlockSpec((1, nh, LANES), lambda i, p, pt: (i, 0, 0)),
                       pl.BlockSpec((1, nh, LANES), lambda i, p, pt: (i, 0, 0)),
                       pl.BlockSpec((1, LANES, nh), lambda i, p, pt: (i, 0, 0))],
            scratch_shapes=[pltpu.VMEM((2, CMP_GROUP_PAGES * 8, CMP_STRIDE * LANES), F32),
                            pltpu.VMEM((2, 2, nh, 256), F32)]),
        out_shape=[jax.ShapeDtypeStruct((b, nh, LANES), BF16), jax.ShapeDtypeStruct((b, nh, LANES), BF16),
                   jax.ShapeDtypeStruct((b, LANES, nh), BF16)],
        compiler_params=_cparams(("arbitrary", "arbitrary")),
    )(table, pages, pages, w1e, pee, b1e, w2e, w2t)


def _kvfmt_kernel(kv_ref, win_ref, k_ref, vt_ref, kw_ref, vwt_ref):
    p = pl.program_id(0)
    src = kv_ref[...]
    r = lax.broadcasted_iota(jnp.int32, (PAGE, LANES), 0)
    lane = lax.broadcasted_iota(jnp.int32, (PAGE, LANES), 1)
    blk = (2 * p + r // SLC_BLOCK) % SLAB_BLOCKS
    k_ref[:, 0:128] = src[:, 0:128].astype(BF16)
    k_ref[:, 128:256] = jnp.where(lane == blk, 1.0, 0.0).astype(BF16)
    v_t = src[:, 128:256].T.astype(BF16)
    for g in range(KV_HEADS):
        vt_ref[0, g, 0:HEAD_DIM, :] = v_t[g * HEAD_DIM:(g + 1) * HEAD_DIM, :]
        vt_ref[0, g, HEAD_DIM:HEAD_DIM + SUM_ROWS, :] = jnp.ones((SUM_ROWS, PAGE), BF16)
    kw_ref[...] = win_ref[:, 0:128].astype(BF16)
    vwt_ref[0] = win_ref[:, 128:256].T.astype(BF16)


def _kvfmt(kv, win, tk):
    l = kv.shape[0]
    per = tk // PAGE
    return pl.pallas_call(
        _kvfmt_kernel,
        grid=(l // PAGE,),
        in_specs=[pl.BlockSpec((PAGE, 256), lambda p: (p, 1)), pl.BlockSpec((PAGE, 256), lambda p: (p, 0))],
        out_specs=[pl.BlockSpec((PAGE, 256), lambda p: (p, 0)),
                   pl.BlockSpec((1, KV_HEADS, HEAD_DIM + SUM_ROWS, PAGE), lambda p: (p // per, 0, 0, p % per)),
                   pl.BlockSpec((PAGE, LANES), lambda p: (p, 0)),
                   pl.BlockSpec((1, LANES, PAGE), lambda p: (p, 0, 0))],
        out_shape=[jax.ShapeDtypeStruct((l, 256), BF16),
                   jax.ShapeDtypeStruct((l // tk, KV_HEADS, HEAD_DIM + SUM_ROWS, tk), BF16),
                   jax.ShapeDtypeStruct((l, LANES), BF16),
                   jax.ShapeDtypeStruct((l // PAGE, LANES, PAGE), BF16)],
        compiler_params=_cparams(("arbitrary",)),
    )(kv, win)


def _group_rows(qfull, g, tq):
    lane = lax.broadcasted_iota(jnp.int32, (tq, LANES), 1)
    keep = (lane >= HEAD_DIM) if g else (lane < HEAD_DIM)
    pieces = []
    for r in range(4):
        h = 4 * g + r
        slab = qfull[:, (h // 2) * LANES:(h // 2 + 1) * LANES]
        if h % 2 != g:
            slab = pltpu.roll(slab, HEAD_DIM, 1)
        pieces.append(jnp.where(keep, slab, 0.0))
    return jnp.concatenate(pieces, axis=0)


def _masked_softmax(s, mask):
    s = jnp.where(mask, s, NEG)
    m = jnp.max(s, axis=-1, keepdims=True)
    e = jnp.where(mask, jnp.exp(s - m), 0.0)
    return e / jnp.maximum(jnp.sum(e, axis=-1, keepdims=True), 1e-30)


def _importance(p4, ov):
    h1, h2, h3 = _split3(p4)
    return _dot(h1, ov) + (_dot(h2, ov) + _dot(h3, ov))


def _topk_bias(imp, qpos, ns):
    tq, nsp = imp.shape
    jf = lax.broadcasted_iota(jnp.int32, (tq, nsp), 1)
    jff = jf.astype(F32)
    cur = qpos >> 6
    valid = jf <= cur
    forced = valid & ((jf == 0) | (jf == cur) | (jf == cur - 1))
    score = jnp.where(forced, FORCE_BONUS, jnp.where(valid, imp, -1.0))
    score = jnp.where(jf < ns, score, -2.0)

    def pick(_, carry):
        sc, sb = carry
        mx = jnp.max(sc, axis=-1, keepdims=True)
        jm = jnp.min(jnp.where(sc == mx, jff, float(nsp)), axis=-1, keepdims=True)
        hit = jff == jm
        return jnp.where(hit, -3.0, sc), jnp.where(hit, 0.0, sb)

    return lax.fori_loop(0, TOP_N, pick, (score, jnp.full((tq, nsp), NEG, F32)))[1]


def _gate_cols(sg, g, k, tq):
    return jnp.concatenate([sg[:, 8 + 3 * (4 * g + r) + k:8 + 3 * (4 * g + r) + k + 1] for r in range(4)], axis=0)


def _emit_heads(outs, tq, y_ref):
    lane = lax.broadcasted_iota(jnp.int32, (tq, LANES), 1)
    for k in range(4):
        g = k // 2
        r0 = (2 * k) % 4
        a = outs[g][r0 * tq:(r0 + 1) * tq]
        b = outs[g][(r0 + 1) * tq:(r0 + 2) * tq]
        if g == 0:
            slab = jnp.where(lane < HEAD_DIM, a, pltpu.roll(b, HEAD_DIM, 1))
        else:
            slab = jnp.where(lane < HEAD_DIM, pltpu.roll(a, HEAD_DIM, 1), b)
        y_ref[0, :, k * LANES:(k + 1) * LANES] = slab


def _masked_softmax_t(s, mask):
    s = jnp.where(mask, s, NEG)
    m = jnp.max(s, axis=0, keepdims=True)
    e = jnp.exp(s - m)
    inv = jnp.where(m > 0.5 * NEG, 1.0 / jnp.maximum(jnp.sum(e, axis=0, keepdims=True), 1e-30), 0.0)
    return e * inv


def _topk_bias_t(imp, qpos, ns):
    nsp, tq = imp.shape
    jf = lax.broadcasted_iota(jnp.int32, (nsp, tq), 0)
    jff = jf.astype(F32)
    cur = qpos >> 6
    valid = jf <= cur
    forced = valid & ((jf == 0) | (jf == cur) | (jf == cur - 1))
    score = jnp.where(forced, FORCE_BONUS, jnp.where(valid, imp, -1.0))
    score = jnp.where(jf < ns, score, -2.0)

    def pick(_, carry):
        sc, sb = carry
        mx = jnp.max(sc, axis=0, keepdims=True)
        jm = jnp.min(jnp.where(sc == mx, jff, float(nsp)), axis=0, keepdims=True)
        hit = jff == jm
        return jnp.where(hit, -3.0, sc), jnp.where(hit, 0.0, sb)

    return lax.fori_loop(0, TOP_N, pick, (score, jnp.full((nsp, tq), NEG, F32)))[1]


def _nsa_kernel(qn_ref, qr_ref, dtg_ref, kchi_ref, kclo_ref, vct_ref, kaug_ref, vt_ref, kwin_ref, vwt_ref,
                y_ref, qat, p4_sc, cs_sc, sa_sc, sb_sc, m_sc, acc_sc, *, tq, ns, nsp, tk, wl):
    t = pl.program_id(1)
    scale = HEAD_DIM ** -0.5
    r4 = 4 * tq
    nslab = nsp // LANES
    lane4 = lax.broadcasted_iota(jnp.int32, (1, r4), 1)
    posr = t * tq + (lane4 & (tq - 1))
    qpos = t * tq + lax.broadcasted_iota(jnp.int32, (1, tq), 1)

    qn = qn_ref[0] * scale
    qr = qr_ref[0] * scale
    ncb = kchi_ref.shape[1]
    cch = cs_sc.shape[1]
    cmax = (t * tq + tq - CMP_BLOCK) // CMP_STRIDE
    nch = jnp.minimum(cmax // cch + 1, ncb // cch)

    ws = pl.multiple_of(jnp.maximum(t * tq - WINDOW, 0), LANES)
    kw = kwin_ref[pl.ds(ws, wl), :]
    kposw = ws + lax.broadcasted_iota(jnp.int32, (wl, 1), 0)
    dpos = posr - kposw
    mask_w = (dpos >= 0) & (dpos < WINDOW)
    wpage = ws // PAGE

    p4_sc[0:8, :] = jnp.zeros((8, tq), F32)
    p4_sc[8 + ncb:, :] = jnp.zeros((p4_sc.shape[0] - 8 - ncb, tq), F32)
    o_cmp = []
    o_win = []
    imps = []
    for g in range(KV_HEADS):
        qn_t = _group_rows(qn, g, tq).T
        qr_t = _group_rows(qr, g, tq).T
        qh = qn_t.astype(BF16)
        ql = (qn_t - qh.astype(F32)).astype(BF16)

        def cmp_scores(k, m):
            r0 = pl.multiple_of(k * cch, cch)
            kh = kchi_ref[0, pl.ds(r0, cch), :]
            kl = kclo_ref[0, pl.ds(r0, cch), :]
            s = _dot(kh, qh) + (_dot(kh, ql) + _dot(kl, qh))
            c_end = (r0 + lax.broadcasted_iota(jnp.int32, (cch, 1), 0)) * CMP_STRIDE + (CMP_BLOCK - 1)
            s = jnp.where(c_end <= posr, s, NEG)
            cs_sc[k] = s
            return jnp.maximum(m, jnp.max(s, axis=0, keepdims=True))

        m_c = lax.fori_loop(0, nch, cmp_scores, jnp.full((1, r4), NEG, F32))

        def cmp_exp(k, l):
            e = jnp.exp(cs_sc[k] - m_c)
            cs_sc[k] = e
            return l + jnp.sum(e, axis=0, keepdims=True)

        l_c = lax.fori_loop(0, nch, cmp_exp, jnp.zeros((1, r4), F32))
        inv_c = jnp.where(m_c > 0.5 * NEG, 1.0 / jnp.maximum(l_c, 1e-30), 0.0)

        def cmp_out(k, o):
            r0 = pl.multiple_of(k * cch, cch)
            p = cs_sc[k] * inv_c
            p4_sc[pl.ds(8 + r0, cch), :] = p[:, 0:tq] + p[:, tq:2 * tq] + p[:, 2 * tq:3 * tq] + p[:, 3 * tq:4 * tq]
            return o + _dot(vct_ref[k], p.astype(BF16))

        o_cmp.append(lax.fori_loop(0, nch, cmp_out, jnp.zeros((LANES, r4), F32)))

        def cmp_zero(k, carry):
            r0 = pl.multiple_of(k * cch, cch)
            p4_sc[pl.ds(8 + r0, cch), :] = jnp.zeros((cch, tq), F32)
            return carry

        lax.fori_loop(nch, ncb // cch, cmp_zero, 0)
        per = SLC_BLOCK // CMP_STRIDE
        imp = p4_sc[pl.ds(8 - 1, nsp, stride=per), :]
        for k in range(per):
            imp = imp + p4_sc[pl.ds(8 + k, nsp, stride=per), :]
        imps.append(imp)
        qs_b = (qr_t * LOG2E).astype(BF16)
        for sl in range(nslab):
            qat[sl, g, 0:LANES, :] = qs_b
        p_w = _masked_softmax_t(_dot(kw, qr_t.astype(BF16)), mask_w).astype(BF16)
        o_w = _dot(vwt_ref[wpage], p_w[0:PAGE])
        for i in range(1, wl // PAGE):
            o_w = o_w + _dot(vwt_ref[wpage + i], p_w[i * PAGE:(i + 1) * PAGE])
        o_win.append(o_w)

    for g in range(KV_HEADS):
        selb = _topk_bias_t(imps[g], qpos, ns)
        for sl in range(nslab):
            piece = selb[sl * LANES:(sl + 1) * LANES, :]
            qat[sl, g, LANES:2 * LANES, :] = jnp.concatenate([piece] * 4, axis=1).astype(BF16)

    m_sc[...] = jnp.full(m_sc.shape, 2.0 * NEG, F32)
    acc_sc[...] = jnp.zeros(acc_sc.shape, F32)
    ndiag = (t * tq + tq - 1) // tk

    def scores(jt, s_buf):
        k0 = pl.multiple_of(jt * tk, tk)
        kt = kaug_ref[pl.ds(k0, tk), :]
        sb = k0 // (SLAB_BLOCKS * SLC_BLOCK)
        for g in range(KV_HEADS):
            s_buf[g] = _dot(kt, qat[sb, g])

    def accumulate(jt, s_buf, causal):
        for g in range(KV_HEADS):
            s = s_buf[g]
            if causal:
                kpos = jt * tk + lax.broadcasted_iota(jnp.int32, (tk, 1), 0)
                s = jnp.where(kpos <= posr, s, NEG)
            m_old = m_sc[g]
            m_new = jnp.maximum(m_old, jnp.max(s, axis=0, keepdims=True))
            pe = jnp.exp2(s - m_new).astype(BF16)
            acc_sc[g] = jnp.exp2(m_old - m_new) * acc_sc[g] + _dot(vt_ref[jt, g], pe)
            m_sc[g] = m_new

    scores(0, sa_sc)

    def tile_pair(p, carry):
        jt = 2 * p
        scores(jt + 1, sb_sc)
        accumulate(jt, sa_sc, False)
        scores(jt + 2, sa_sc)
        accumulate(jt + 1, sb_sc, False)
        return carry

    npair = ndiag // 2
    lax.fori_loop(0, npair, tile_pair, 0)

    @pl.when(ndiag % 2 == 0)
    def _():
        accumulate(ndiag, sa_sc, True)

    @pl.when(ndiag % 2 == 1)
    def _():
        scores(ndiag, sb_sc)
        accumulate(ndiag - 1, sa_sc, False)
        accumulate(ndiag, sb_sc, True)

    sg_t = jax.nn.sigmoid(dtg_ref[0]).T
    outs = []
    for g in range(KV_HEADS):
        gate = [jnp.concatenate([sg_t[8 + 3 * (4 * g + r) + k:8 + 3 * (4 * g + r) + k + 1, :] for r in range(4)], axis=1)
                for k in range(3)]
        o_sel = acc_sc[g, 0:HEAD_DIM, :] / acc_sc[g, HEAD_DIM:HEAD_DIM + 1, :]
        zero = jnp.zeros((HEAD_DIM, r4), F32)
        o_sel = jnp.concatenate([o_sel, zero] if g == 0 else [zero, o_sel], axis=0)
        o_t = gate[0] * o_cmp[g] + gate[1] * o_sel + gate[2] * o_win[g]
        outs.append(o_t.T)
    _emit_heads(outs, tq, y_ref)


def _nsa(qn, qr, dtg, kchi, kclo, vct, kaug, vt, kwin, vwt, *, tq, ns, nsp, tk, wl):
    b, lq = qn.shape[:2]
    ncb = kchi.shape[1]
    cch = min(CMP_CHUNK, ncb)
    assert nsp * (SLC_BLOCK // CMP_STRIDE) >= ncb and ncb % cch == 0
    vct = vct[0].reshape(LANES, ncb // cch, cch).transpose(1, 0, 2)
    nslab = nsp // LANES
    kern = functools.partial(_nsa_kernel, tq=tq, ns=ns, nsp=nsp, tk=tk, wl=wl)
    rows = lambda n: pl.BlockSpec((1, tq, n), lambda i, t: (i, t, 0))
    whole = lambda arr: pl.BlockSpec(arr.shape, lambda i, t: (0,) * arr.ndim, pipeline_mode=pl.Buffered(1))
    return pl.pallas_call(
        kern,
        grid=(b, lq // tq),
        in_specs=[rows(512), rows(512), rows(128), whole(kchi), whole(kclo), whole(vct),
                  whole(kaug), whole(vt), whole(kwin), whole(vwt)],
        out_specs=rows(512),
        out_shape=jax.ShapeDtypeStruct((b, lq, 512), F32),
        scratch_shapes=[pltpu.VMEM((nslab, KV_HEADS, 2 * LANES, 4 * tq), BF16),
                        pltpu.VMEM((nsp * (SLC_BLOCK // CMP_STRIDE) + 16, tq), F32),
                        pltpu.VMEM((ncb // cch, cch, 4 * tq), F32),
                        pltpu.VMEM((KV_HEADS, tk, 4 * tq), F32),
                        pltpu.VMEM((KV_HEADS, tk, 4 * tq), F32),
                        pltpu.VMEM((KV_HEADS, 1, 4 * tq), F32),
                        pltpu.VMEM((KV_HEADS, HEAD_DIM + SUM_ROWS, 4 * tq), F32)],
        compiler_params=_cparams(("arbitrary", "arbitrary")),
    )(qn, qr, dtg, kchi, kclo, vct, kaug, vt, kwin, vwt)


SAMPLE_PAGES_PER_STEP = 16
SAMPLE_TQ = 8


def _nsa_sample_kernel(pt_ref, *refs, npages, past, ns, nsp):
    pps = SAMPLE_PAGES_PER_STEP
    tq = SAMPLE_TQ
    pages = refs[:pps]
    (tail_ref, qn_ref, qr_ref, dtg_ref, w1_ref, pe_ref, b1_ref, w2_ref, ov_ref, cwin_ref, wtail_ref,
     y_ref, tsc, xs, ps, s_all, vt_all, bias_sc) = refs[pps:]
    j = pl.program_id(1)
    ngroups = npages // pps
    hb = PAGE // CMP_STRIDE
    scale = HEAD_DIM ** -0.5
    qr_aug = jnp.concatenate([_group_rows(qr_ref[0] * scale, g, tq) for g in range(KV_HEADS)], axis=0).astype(BF16)

    for i, pg in enumerate(pages):
        for kv in range(2):
            tsc[i, kv] = pg[0, 0, kv * LANES:(kv + 1) * LANES, :].T
            for pos in range(CMP_STRIDE):
                xs[kv, i * hb:(i + 1) * hb, pos * LANES:(pos + 1) * LANES] = tsc[i, kv, pl.ds(pos, hb, stride=CMP_STRIDE), :]
        page_idx = j * pps + i
        s_all[page_idx] = _dot(qr_aug, pg[0, 0, 256:384, :].astype(BF16))
        vt_all[page_idx] = pg[0, 0, 384:512, :].astype(BF16)
    rows = pps * hb
    g0 = pl.multiple_of(j * rows, rows)
    for kv in range(2):
        x = xs[kv]
        for s in range(2):
            ps[kv, s, pl.ds(g0, rows), :] = _dot((x + pe_ref[kv, s:s + 1, :]).astype(BF16), w1_ref[kv, s])

    @pl.when(j == ngroups - 1)
    def _():
        nh = npages * hb
        kcvc = []
        for kv in range(2):
            hid = b1_ref[kv:kv + 1, :] + ps[kv, 0] + pltpu.roll(ps[kv, 1], nh - 1, 0)
            kcvc.append(_dot(jax.nn.gelu(hid).astype(BF16), w2_ref[kv]))
        kc, vc = kcvc
        qpos = past + lax.broadcasted_iota(jnp.int32, (tq, 1), 0)
        pos_r = jnp.concatenate([qpos] * (4 * KV_HEADS), axis=0)
        nrow = 4 * KV_HEADS * tq
        s_all[npages] = _dot(qr_aug, tail_ref[0, 256:384, :].astype(BF16))
        vt_all[npages] = tail_ref[0, 384:512, :].astype(BF16)

        qn_aug = jnp.concatenate([_group_rows(qn_ref[0] * scale, g, tq) for g in range(KV_HEADS)], axis=0)
        c_end = lax.broadcasted_iota(jnp.int32, (1, nh), 1) * CMP_STRIDE + (CMP_BLOCK - 1)
        p_c = _masked_softmax(_dot_nt_hi(qn_aug, kc), c_end <= pos_r)
        o_c = _dot(p_c.astype(BF16), vc.astype(BF16))
        imps = []
        for g in range(KV_HEADS):
            base = 4 * tq * g
            p4 = (p_c[base:base + tq] + p_c[base + tq:base + 2 * tq]
                  + p_c[base + 2 * tq:base + 3 * tq] + p_c[base + 3 * tq:base + 4 * tq])
            imps.append(_importance(p4, ov_ref[...]))
        selb = _topk_bias(jnp.concatenate(imps, axis=0), jnp.concatenate([qpos] * KV_HEADS, axis=0), ns)
        lane_b = lax.broadcasted_iota(jnp.int32, (KV_HEADS * tq, LANES), 1)
        per_page = PAGE // SLC_BLOCK
        assert per_page == 2 and nsp >= per_page * (npages + 1)
        for pp in range(npages + 1):
            lo = jnp.broadcast_to(selb[:, 2 * pp:2 * pp + 1], (KV_HEADS * tq, LANES))
            hi = jnp.broadcast_to(selb[:, 2 * pp + 1:2 * pp + 2], (KV_HEADS * tq, LANES))
            bias_sc[pp] = jnp.where(lane_b < SLC_BLOCK, lo, hi)

        def biased(pidx):
            b = bias_sc[pidx]
            b_r = jnp.concatenate([b[0:tq]] * 4 + [b[tq:2 * tq]] * 4, axis=0)
            return s_all[pidx] + b_r

        def p1(pidx, m):
            s = biased(pidx)
            s_all[pidx] = s
            return jnp.maximum(m, s)

        m = lax.fori_loop(0, npages, p1, jnp.full((nrow, LANES), 2.0 * NEG, F32))
        kpos_t = past + lax.broadcasted_iota(jnp.int32, (1, PAGE), 1)
        s_t = jnp.where(kpos_t <= pos_r, biased(npages), NEG)
        s_all[npages] = s_t
        mrow = jnp.max(jnp.maximum(m, s_t), axis=-1, keepdims=True)

        def p2(pidx, carry):
            l, acc = carry
            p = jnp.exp(s_all[pidx] - mrow)
            return l + p, acc + _dot_nt(p.astype(BF16), vt_all[pidx])

        zero = jnp.zeros((nrow, LANES), F32)
        l, acc = p2(npages, lax.fori_loop(0, npages, p2, (zero, zero), unroll=8))
        o_s = acc / jnp.sum(l, axis=-1, keepdims=True)

        wbuf = cwin_ref.shape[3]
        kw_b = cwin_ref[0, 0, 0:128, :].astype(BF16)
        vw_b = cwin_ref[0, 0, 128:256, :].astype(BF16)
        kt_b = wtail_ref[0, 0:128, :].astype(BF16)
        vt_b = wtail_ref[0, 128:256, :].astype(BF16)
        s_w = jnp.concatenate([_dot(qr_aug, kw_b), _dot(qr_aug, kt_b)], axis=1)
        kposw = jnp.concatenate([past - wbuf + lax.broadcasted_iota(jnp.int32, (1, wbuf), 1), kpos_t], axis=1)
        dpos = pos_r - kposw
        p_w = _masked_softmax(s_w, (dpos >= 0) & (dpos < WINDOW) & (kposw >= 0))
        o_w = _dot_nt(p_w[:, 0:wbuf].astype(BF16), vw_b) + _dot_nt(p_w[:, wbuf:].astype(BF16), vt_b)

        sg = jax.nn.sigmoid(dtg_ref[0])
        outs = []
        for g in range(KV_HEADS):
            sl = slice(4 * tq * g, 4 * tq * (g + 1))
            outs.append(_gate_cols(sg, g, 0, tq) * o_c[sl] + _gate_cols(sg, g, 1, tq) * o_s[sl]
                        + _gate_cols(sg, g, 2, tq) * o_w[sl])
        _emit_heads(outs, tq, y_ref)


def _nsa_sample(layer, cache_t, table, tail_t, qn, qr, dtg, w1e, pee, b1e, w2e, ov, cwin_t, wtail_t, *, past, ns):
    b, npages = table.shape
    pps = SAMPLE_PAGES_PER_STEP
    tq = SAMPLE_TQ
    nsp = ov.shape[1]
    nh = npages * (PAGE // CMP_STRIDE)
    nrow = 4 * KV_HEADS * tq
    kern = functools.partial(_nsa_sample_kernel, npages=npages, past=past, ns=ns, nsp=nsp)
    const = lambda arr: pl.BlockSpec(arr.shape, lambda i, j, pt: (0,) * arr.ndim, pipeline_mode=pl.Buffered(1))
    perb = lambda arr: pl.BlockSpec((1,) + arr.shape[1:], lambda i, j, pt: (i,) + (0,) * (arr.ndim - 1))

    def page_spec(k):
        return pl.BlockSpec((1, 1, 512, PAGE), lambda i, j, pt: (layer, pt[i, j * pps + k], 0, 0))

    return pl.pallas_call(
        kern,
        grid_spec=pltpu.PrefetchScalarGridSpec(
            num_scalar_prefetch=1,
            grid=(b, npages // pps),
            in_specs=[page_spec(k) for k in range(pps)] + [
                perb(tail_t), perb(qn), perb(qr), perb(dtg), const(w1e), const(pee), const(b1e), const(w2e),
                const(ov),
                pl.BlockSpec((1, 1) + cwin_t.shape[2:], lambda i, j, pt: (layer, i, 0, 0)), perb(wtail_t)],
            out_specs=pl.BlockSpec((1, tq, 512), lambda i, j, pt: (i, 0, 0)),
            scratch_shapes=[pltpu.VMEM((pps, 2, PAGE, LANES), F32),
                            pltpu.VMEM((2, pps * 8, CMP_STRIDE * LANES), F32),
                            pltpu.VMEM((2, 2, nh, 256), F32),
                            pltpu.VMEM((npages + 1, nrow, LANES), F32),
                            pltpu.VMEM((npages + 1, LANES, PAGE), BF16),
                            pltpu.VMEM((npages + 1, KV_HEADS * tq, LANES), F32)]),
        out_shape=jax.ShapeDtypeStruct((b, tq, 512), F32),
        compiler_params=_cparams(("arbitrary", "arbitrary")),
    )(table, *([cache_t] * pps), tail_t, qn, qr, dtg, w1e, pee, b1e, w2e, ov, cwin_t, wtail_t)


def _oproj_kernel(x_ref, ys_ref, yn_ref, w_ref, g_ref, r_ref, x1_ref, h_ref, lg_ref):
    a = jnp.concatenate([ys_ref[...], yn_ref[...]], axis=1).astype(BF16)
    x1 = x_ref[...] + _dot(a, w_ref[...])
    x1_ref[...] = x1
    h = _rms(x1, g_ref[...])
    h_ref[...] = h.astype(h_ref.dtype)
    lg_ref[...] = _dot_hi(h, r_ref[...])


def _oproj(x, ys, yn, w, g, router, tm, h_dtype):
    t = x.shape[0]
    row = lambda n: pl.BlockSpec((tm, n), lambda i: (i, 0))
    full = lambda a: pl.BlockSpec(a.shape, lambda i: (0,) * a.ndim)
    return pl.pallas_call(
        _oproj_kernel,
        grid=(t // tm,),
        in_specs=[row(D_MODEL), row(512), row(512), full(w), full(g), full(router)],
        out_specs=[row(D_MODEL), row(D_MODEL), row(LANES)],
        out_shape=[jax.ShapeDtypeStruct((t, D_MODEL), F32), jax.ShapeDtypeStruct((t, D_MODEL), h_dtype),
                   jax.ShapeDtypeStruct((t, LANES), F32)],
        compiler_params=_cparams(("arbitrary",)),
    )(x, ys, yn, w, g, router)


def _ffn_kernel(h_ref, x1_ref, wg_ref, wu_ref, wd_ref, o_ref):
    f = pl.program_id(1)

    @pl.when(f == 0)
    def _():
        o_ref[...] = x1_ref[...]

    hb = h_ref[...]
    act = _silu(_dot(hb, wg_ref[...])) * _dot(hb, wu_ref[...])
    o_ref[...] += _dot(act.astype(BF16), wd_ref[...])


def _ffn(h, x1, wg, wu, wd, tm, tf):
    t = h.shape[0]
    dff = wg.shape[1]
    return pl.pallas_call(
        _ffn_kernel,
        grid=(t // tm, dff // tf),
        in_specs=[pl.BlockSpec((tm, D_MODEL), lambda i, f: (i, 0)),
                  pl.BlockSpec((tm, D_MODEL), lambda i, f: (i, 0)),
                  pl.BlockSpec((D_MODEL, tf), lambda i, f: (0, f)),
                  pl.BlockSpec((D_MODEL, tf), lambda i, f: (0, f)),
                  pl.BlockSpec((tf, D_MODEL), lambda i, f: (f, 0))],
        out_specs=pl.BlockSpec((tm, D_MODEL), lambda i, f: (i, 0)),
        out_shape=jax.ShapeDtypeStruct((t, D_MODEL), F32),
        compiler_params=_cparams(("arbitrary", "arbitrary")),
    )(h, x1, wg, wu, wd)


def _moe_kernel(be_ref, nb_ref, xb_ref, wg_ref, wu_ref, wd_ref, o_ref, xb16):
    b = pl.program_id(0)
    f = pl.program_id(1)

    @pl.when(f == 0)
    def _():
        o_ref[...] = jnp.zeros(o_ref.shape, F32)
        xb16[...] = xb_ref[...].astype(BF16)

    @pl.when(b < nb_ref[0])
    def _():
        hb = xb16[...]
        act = _silu(_dot(hb, wg_ref[0].astype(BF16))) * _dot(hb, wu_ref[0].astype(BF16))
        o_ref[...] += _dot(act.astype(BF16), wd_ref[0].astype(BF16))


def _moe(blk_e, nblk_used, xb, wg, wu, wd, tmb, tf):
    n_blk = xb.shape[0] // tmb
    dex = wg.shape[2]
    nf = dex // tf

    def feff(b, f, nb):
        return jnp.where(b < nb[0], f, nf - 1)

    return pl.pallas_call(
        _moe_kernel,
        grid_spec=pltpu.PrefetchScalarGridSpec(
            num_scalar_prefetch=2,
            grid=(n_blk, nf),
            in_specs=[pl.BlockSpec((tmb, D_MODEL), lambda b, f, be, nb: (b, 0)),
                      pl.BlockSpec((1, D_MODEL, tf), lambda b, f, be, nb: (be[b], 0, feff(b, f, nb))),
                      pl.BlockSpec((1, D_MODEL, tf), lambda b, f, be, nb: (be[b], 0, feff(b, f, nb))),
                      pl.BlockSpec((1, tf, D_MODEL), lambda b, f, be, nb: (be[b], feff(b, f, nb), 0))],
            out_specs=pl.BlockSpec((tmb, D_MODEL), lambda b, f, be, nb: (b, 0)),
            scratch_shapes=[pltpu.VMEM((tmb, D_MODEL), BF16)]),
        out_shape=jax.ShapeDtypeStruct((n_blk * tmb, D_MODEL), F32),
        compiler_params=_cparams(("arbitrary", "arbitrary")),
    )(blk_e, nblk_used, xb, wg, wu, wd)


def _moe_layer(h, x1, logits, wg, wu, wd, tmb, tf):
    t = h.shape[0]
    top_v, top_e = lax.top_k(logits[:, :N_EXPERTS], TOP_K)
    gate = jax.nn.softmax(top_v, axis=-1)
    e_flat = top_e.reshape(-1)
    npair = t * TOP_K
    order = jnp.argsort(e_flat)
    rank = jnp.argsort(order).astype(jnp.int32)
    counts = jnp.bincount(e_flat, length=N_EXPERTS).astype(jnp.int32)
    starts = jnp.cumsum(counts) - counts
    padded = (counts + tmb - 1) // tmb * tmb
    pends = jnp.cumsum(padded)
    pstarts = pends - padded
    dest = (pstarts[e_flat] + rank - starts[e_flat]).astype(jnp.int32).reshape(t, TOP_K)
    n_blk = -(-npair // tmb) + N_EXPERTS
    blk_e = jnp.minimum(jnp.searchsorted(pends, jnp.arange(n_blk, dtype=jnp.int32) * tmb, side='right'),
                        N_EXPERTS - 1).astype(jnp.int32)
    nblk_used = (pends[-1] // tmb).astype(jnp.int32).reshape(1)
    e_p = jnp.repeat(blk_e, tmb)
    off = jnp.arange(n_blk * tmb, dtype=jnp.int32) - pstarts[e_p]
    src = order[jnp.clip(starts[e_p] + off, 0, npair - 1)].astype(jnp.int32)
    buf_tok = jnp.where(off < counts[e_p], src // TOP_K, t)
    h_pad = jnp.concatenate([h, jnp.zeros((1, D_MODEL), h.dtype)], axis=0)
    out = _moe(blk_e, nblk_used, h_pad[buf_tok], wg, wu, wd, tmb, tf)
    return x1 + out[dest[:, 0]] * gate[:, 0:1] + out[dest[:, 1]] * gate[:, 1:2]


def _final_kernel(x_ref, g_ref, o_ref):
    o_ref[...] = _rms(x_ref[...], g_ref[...])


def _final_norm(x, g, tm):
    t = x.shape[0]
    return pl.pallas_call(
        _final_kernel,
        grid=(t // tm,),
        in_specs=[pl.BlockSpec((tm, D_MODEL), lambda i: (i, 0)), pl.BlockSpec(g.shape, lambda i: (0, 0))],
        out_specs=pl.BlockSpec((tm, D_MODEL), lambda i: (i, 0)),
        out_shape=jax.ShapeDtypeStruct((t, D_MODEL), F32),
        compiler_params=_cparams(("arbitrary",)),
    )(x, g)


def _rope_tables(pos):
    inv = 1.0 / (ROPE_THETA ** (jnp.arange(0, HEAD_DIM, 2, dtype=F32) / HEAD_DIM))
    ang = pos.astype(F32)[:, None] * inv[None, :]
    cos = jnp.cos(ang)
    sin = jnp.sin(ang)
    return jnp.concatenate([cos, cos, cos, cos], axis=1), jnp.concatenate([-sin, sin, -sin, sin], axis=1)


def _overlap(ncb, nsp, ns):
    c = jnp.arange(ncb, dtype=jnp.int32)[:, None] * CMP_STRIDE
    s = jnp.arange(nsp, dtype=jnp.int32)[None, :] * SLC_BLOCK
    ov = (c <= s + SLC_BLOCK - 1) & (c + CMP_BLOCK - 1 >= s) & (jnp.arange(nsp)[None, :] < ns)
    return ov.astype(BF16)


def _pad_lanes(v, n):
    return jnp.zeros((1, n), F32).at[0, :v.shape[0]].set(v.astype(F32))


def kernel(x_prompt, x_sample, cache_kv, cache_win, state_ssm, state_conv, page_table, ln_mix, w_in, conv_w,
           conv_b, dt_bias, a_log, d_skip, ssd_norm, cmp_pos, cmp_w1, cmp_b1, cmp_w2, w_out, ln_ffn,
           ffn_w_gate, ffn_w_up, ffn_w_down, moe_router, moe_w_gate, moe_w_up, moe_w_down, ln_final):
    depth = w_in.shape[0]
    bsz, seq = x_prompt.shape[:2]
    dbsz, dseq = x_sample.shape[:2]
    n_pages = page_table.shape[1]
    past = n_pages * cache_kv.shape[2]
    win_buf = cache_win.shape[2]
    npool = cache_kv.shape[1]
    assert bsz == 1 and cache_kv.shape[2] == PAGE and seq % PAGE == 0 and seq >= WINDOW + PAGE
    assert win_buf == WINDOW and past % PAGE == 0
    assert not any(past <= CMP_STRIDE * c + CMP_BLOCK - 1 <= past + dseq - 1
                   for c in range(past // CMP_STRIDE - 2, past // CMP_STRIDE + 2))

    ts = dbsz * dseq
    ns_p = seq // SLC_BLOCK
    ns_s = -(-(past + dseq) // SLC_BLOCK)
    nsp_p = -(-ns_p // LANES) * LANES
    nsp_s = -(-ns_s // LANES) * LANES
    assert dseq <= SAMPLE_TQ and n_pages % SAMPLE_PAGES_PER_STEP == 0
    cache_t = jnp.transpose(cache_kv, (0, 1, 3, 4, 5, 2)).reshape(depth, npool, 512, PAGE)
    cwin_t = jnp.transpose(cache_win, (0, 1, 3, 4, 5, 2)).reshape(depth, dbsz, 256, win_buf)

    cs_p, sn_p = _rope_tables(jnp.arange(seq, dtype=jnp.int32))
    cs_s, sn_s = _rope_tables(past + jnp.arange(ts, dtype=jnp.int32) % dseq)
    ov_s = _overlap(past // CMP_STRIDE, nsp_s, ns_s)
    ident = jnp.arange(seq // PAGE, dtype=jnp.int32).reshape(1, -1)

    xp = x_prompt.reshape(seq, D_MODEL)
    xs = x_sample.reshape(ts, D_MODEL)
    outs = {k: [] for k in ('kv_p', 'kv_s', 'win_p', 'win_s', 'ssm_p', 'ssm_s', 'conv_p', 'conv_s')}

    for l in range(depth):
        w = w_in[l]
        wr = jnp.concatenate([w[:, :1536], w[:, 1544:2824], w[:, 1536:1544], w[:, 2824:2848],
                              jnp.zeros((D_MODEL, C_END - 2848), F32)], axis=1).astype(BF16)
        g_mix = ln_mix[l].reshape(1, -1)
        cw8 = jnp.zeros((8, CONV_DIM), F32).at[:CONV_W].set(conv_w[l])
        cb = conv_b[l].reshape(1, -1)
        dtb = _pad_lanes(dt_bias[l], LANES)
        a_neg = _pad_lanes(-jnp.exp(a_log[l].astype(F32)), LANES)
        dsk = jnp.repeat(d_skip[l].astype(F32), SSD_HEAD_DIM).reshape(1, -1)
        ng = ssd_norm[l].reshape(1, -1)
        w1 = cmp_w1[l].reshape(2, 2, CMP_STRIDE, HEAD_DIM, CMP_HIDDEN)
        eye = jnp.eye(KV_HEADS, dtype=F32)
        w1e = jnp.einsum('vspdh,gk->vspgdkh', w1, eye).reshape(2, 2, CMP_STRIDE * LANES, 256).astype(BF16)
        pe = cmp_pos[l].reshape(2, 2, CMP_STRIDE, 1, HEAD_DIM)
        pee = jnp.broadcast_to(pe, (2, 2, CMP_STRIDE, KV_HEADS, HEAD_DIM)).reshape(2, 2, CMP_STRIDE * LANES)
        b1e = jnp.concatenate([cmp_b1[l], cmp_b1[l]], axis=1)
        w2e = jnp.einsum('vhd,gk->vghkd', cmp_w2[l], eye).reshape(2, 256, LANES).astype(BF16)
        wo = w_out[l].astype(BF16)
        g_ffn = ln_ffn[l].reshape(1, -1)
        if l % 2 == 0:
            router = jnp.zeros((D_MODEL, LANES), F32)
        else:
            router = jnp.zeros((D_MODEL, LANES), F32).at[:, :N_EXPERTS].set(moe_router[l // 2])

        z, xbc, dtg, q, qr, kv, win = _front(xp, g_mix, wr, cs_p, sn_p, 256)
        y_ssd, hist, hfin = _ssd(z[None], xbc[None], dtg[None], jnp.zeros((1, 8, CONV_DIM), F32),
                                 jnp.zeros((1, D_SSD, SSD_STATE), F32), cw8, cb, dtb, a_neg, dsk, ng, SSD_CHUNK)
        kv_pages = kv.reshape(seq // PAGE, PAGE, 512)
        kchi, kclo, vct = _compress(kv_pages, ident, w1e, pee, b1e, w2e)
        kaug, vt, kwin, vwt = _kvfmt(kv, win, NSA_TK)
        y_nsa = _nsa(q[None], qr[None], dtg[None], kchi, kclo, vct, kaug, vt, kwin, vwt,
                     tq=128, ns=ns_p, nsp=nsp_p, tk=NSA_TK, wl=WINDOW + 128)
        h_dtype = BF16 if l % 2 == 0 else F32
        x1, h, logits = _oproj(xp, y_ssd[0], y_nsa[0], wo, g_ffn, router, 512, h_dtype)
        outs['kv_p'].append(kv.reshape(1, seq, 4, KV_HEADS, HEAD_DIM))
        outs['win_p'].append(win[seq - WINDOW:].reshape(1, WINDOW, 2, KV_HEADS, HEAD_DIM))
        outs['ssm_p'].append(hfin.reshape(1, SSD_HEADS, SSD_HEAD_DIM, SSD_STATE))
        outs['conv_p'].append(hist[:, 5:8])

        zs, xbcs, dtgs, qs, qrs, kvs, wins = _front(xs, g_mix, wr, cs_s, sn_s, ts)
        padl = lambda v: jnp.pad(v.reshape(dbsz, dseq, -1), ((0, 0), (0, SSD_CHUNK - dseq), (0, 0)))
        hist8 = jnp.pad(state_conv[l], ((0, 0), (8 - (CONV_W - 1), 0), (0, 0)))
        y_ssd_s, hist_s, hfin_s = _ssd(padl(zs), padl(xbcs), padl(dtgs), hist8,
                                       state_ssm[l].reshape(dbsz, D_SSD, SSD_STATE),
                                       cw8, cb, dtb, a_neg, dsk, ng, dseq)
        tail_t = jnp.pad(jnp.transpose(kvs.reshape(dbsz, dseq, 512), (0, 2, 1)), ((0, 0), (0, 0), (0, PAGE - dseq)))
        wtail_t = jnp.pad(jnp.transpose(wins.reshape(dbsz, dseq, 256), (0, 2, 1)), ((0, 0), (0, 0), (0, PAGE - dseq)))
        padq = lambda v: jnp.pad(v.reshape(dbsz, dseq, -1), ((0, 0), (0, SAMPLE_TQ - dseq), (0, 0)))
        y_nsa_s = _nsa_sample(l, cache_t, page_table, tail_t, padq(qs), padq(qrs), padq(dtgs), w1e, pee, b1e, w2e,
                              ov_s, cwin_t, wtail_t, past=past, ns=ns_s)
        y_nsa_s = y_nsa_s[:, :dseq].reshape(ts, D_NSA)
        x1s, hs, logits_s = _oproj(xs, y_ssd_s[:, :dseq].reshape(ts, D_SSD), y_nsa_s, wo, g_ffn, router, ts, h_dtype)
        outs['kv_s'].append(kvs.reshape(dbsz, dseq, 4, KV_HEADS, HEAD_DIM))
        win_all = jnp.concatenate([cache_win[l], wins.reshape(dbsz, dseq, 2, KV_HEADS, HEAD_DIM)], axis=1)
        outs['win_s'].append(win_all[:, dseq:])
        outs['ssm_s'].append(hfin_s.reshape(dbsz, SSD_HEADS, SSD_HEAD_DIM, SSD_STATE))
        outs['conv_s'].append(hist_s[:, 5:8])

        if l % 2 == 0:
            wg = ffn_w_gate[l // 2].astype(BF16)
            wu = ffn_w_up[l // 2].astype(BF16)
            wd = ffn_w_down[l // 2].astype(BF16)
            xp = _ffn(h, x1, wg, wu, wd, 1024, 256)
            xs = _ffn(hs, x1s, wg, wu, wd, ts, 256)
        else:
            wg = moe_w_gate[l // 2]
            wu = moe_w_up[l // 2]
            wd = moe_w_down[l // 2]
            xp = _moe_layer(h, x1, logits, wg, wu, wd, 1024, 512)
            xs = _moe_layer(hs, x1s, logits_s, wg, wu, wd, 128, 512)

    g_fin = ln_final.reshape(1, -1)
    y_prompt = _final_norm(xp, g_fin, 1024).reshape(bsz, seq, D_MODEL)
    y_sample = _final_norm(xs, g_fin, ts).reshape(dbsz, dseq, D_MODEL)
    st = lambda k: jnp.stack(outs[k])
    return (y_prompt, y_sample, st('kv_p'), st('kv_s'), st('win_p'), st('win_s'),
            st('ssm_p'), st('ssm_s'), st('conv_p'), st('conv_s'))
```

```python
import functools

import jax
import jax.numpy as jnp
from jax import lax
from jax.experimental import pallas as pl
from jax.experimental.pallas import tpu as pltpu

F32 = jnp.float32
BF16 = jnp.bfloat16

D_MODEL = 1024
SSD_HEADS = 8
SSD_HEAD_DIM = 64
D_SSD = 512
SSD_GROUPS = 2
SSD_STATE = 128
CONV_W = 4
CONV_DIM = 1024
SSD_CHUNK = 128
NSA_HEADS = 8
HEAD_DIM = 64
KV_HEADS = 2
D_NSA = 512
CMP_BLOCK = 32
CMP_STRIDE = 16
CMP_HIDDEN = 128
SLC_BLOCK = 64
TOP_N = 16
WINDOW = 512
ROPE_THETA = 10000.0
FORCE_BONUS = 1.0e4
N_EXPERTS = 8
TOP_K = 2
EPS = 1e-6

PAGE = 128
LANES = 128
LOG2E = 1.4426950408889634
NEG = -1.0e30
SLAB_BLOCKS = 128
VMEM_LIMIT = 56 * 1024 * 1024

C_Z, C_XBC, C_Q, C_KV, C_DT, C_END = 0, 512, 1536, 2048, 2816, 2944


def _cparams(sem):
    return pltpu.CompilerParams(dimension_semantics=sem, vmem_limit_bytes=VMEM_LIMIT)


def _dot(a, b):
    return jnp.dot(a, b, preferred_element_type=F32)


def _dot_nt(a, b):
    return lax.dot_general(a, b, (((1,), (1,)), ((), ())), preferred_element_type=F32)


def _split3(x):
    h1 = x.astype(BF16)
    r1 = x - h1.astype(F32)
    h2 = r1.astype(BF16)
    h3 = (r1 - h2.astype(F32)).astype(BF16)
    return h1, h2, h3


def _dot_nt_hi(a, b):
    a1, a2, _ = _split3(a)
    b1, b2, _ = _split3(b)
    return _dot_nt(a1, b1) + (_dot_nt(a1, b2) + _dot_nt(a2, b1))


def _dot_hi(a, b):
    a1, a2, _ = _split3(a)
    b1, b2, _ = _split3(b)
    return _dot(a1, b1) + (_dot(a1, b2) + _dot(a2, b1))


def _silu(x):
    return x * jax.nn.sigmoid(x)


def _rms(x, g):
    return x * lax.rsqrt(jnp.mean(x * x, axis=-1, keepdims=True) + EPS) * g


def _front_kernel(x_ref, g_ref, w_ref, cs_ref, sn_ref,
                  z_ref, xbc_ref, dtg_ref, q_ref, qr_ref, kv_ref, win_ref):
    xn = _rms(x_ref[...], g_ref[...])
    p = _dot(xn.astype(BF16), w_ref[...])
    cs = cs_ref[...]
    sn = sn_ref[...]
    lane = lax.broadcasted_iota(jnp.int32, cs.shape, 1)
    first = (lane % HEAD_DIM) < (HEAD_DIM // 2)

    def rope(v):
        sw = jnp.where(first, pltpu.roll(v, LANES - HEAD_DIM // 2, 1), pltpu.roll(v, HEAD_DIM // 2, 1))
        return v * cs + sw * sn

    z_ref[...] = p[:, C_Z:C_XBC]
    xbc_ref[...] = p[:, C_XBC:C_Q]
    dtg_ref[...] = p[:, C_DT:C_END]
    q_ref[...] = p[:, C_Q:C_KV]
    for s in range(4):
        qr_ref[:, s * LANES:(s + 1) * LANES] = rope(p[:, C_Q + s * LANES:C_Q + (s + 1) * LANES])
    kv_ref[:, 0:256] = p[:, C_KV:C_KV + 256]
    kv_ref[:, 256:384] = rope(p[:, C_KV + 256:C_KV + 384])
    kv_ref[:, 384:512] = p[:, C_KV + 384:C_KV + 512]
    win_ref[:, 0:128] = rope(p[:, C_KV + 512:C_KV + 640])
    win_ref[:, 128:256] = p[:, C_KV + 640:C_KV + 768]


def _front(x, g, w, cs, sn, tm):
    t = x.shape[0]
    row = lambda n: pl.BlockSpec((tm, n), lambda i: (i, 0))
    full = lambda a: pl.BlockSpec(a.shape, lambda i: (0,) * a.ndim)
    outs = [512, 1024, 128, 512, 512, 512, 256]
    return pl.pallas_call(
        _front_kernel,
        grid=(t // tm,),
        in_specs=[row(D_MODEL), full(g), full(w), row(LANES), row(LANES)],
        out_specs=[row(n) for n in outs],
        out_shape=[jax.ShapeDtypeStruct((t, n), F32) for n in outs],
        compiler_params=_cparams(("arbitrary",)),
    )(x, g, w, cs, sn)


def _expand_heads(v):
    r = v.shape[0]
    lane = lax.broadcasted_iota(jnp.int32, (r, LANES), 1)
    outs = []
    for k in range(4):
        a = jnp.broadcast_to(v[:, 2 * k:2 * k + 1], (r, LANES))
        b = jnp.broadcast_to(v[:, 2 * k + 1:2 * k + 2], (r, LANES))
        outs.append(jnp.where(lane < SSD_HEAD_DIM, a, b))
    return jnp.concatenate(outs, axis=1)


def _ssd_kernel(z_ref, xbc_ref, dtg_ref, hist_ref, h0_ref, cw_ref, cb_ref, dtb_ref, a_ref, dsk_ref, ng_ref,
                y_ref, hist_o_ref, hfin_ref, xpad, ht, *, nchunks, valid_last):
    c = pl.program_id(1)
    q = SSD_CHUNK

    @pl.when(c == 0)
    def _():
        xpad[0:8, :] = hist_ref[0]
        ht[...] = h0_ref[0].T

    xpad[8:8 + q, :] = xbc_ref[0]
    conv = cb_ref[...]
    for k in range(CONV_W):
        conv = conv + xpad[5 + k:5 + k + q, :] * cw_ref[k:k + 1, :]
    xc = _silu(conv)
    xs = xc[:, 0:D_SSD]
    bm = xc[:, D_SSD:D_SSD + 256]
    cm = xc[:, D_SSD + 256:D_SSD + 512]

    row = lax.broadcasted_iota(jnp.int32, (q, q), 0)
    col = lax.broadcasted_iota(jnp.int32, (q, q), 1)
    nvalid = jnp.where(c == nchunks - 1, valid_last, q)
    xdt_in = dtg_ref[0] + dtb_ref[...]
    dt = jnp.maximum(xdt_in, 0.0) + jnp.log1p(jnp.exp(-jnp.abs(xdt_in)))
    dt = jnp.where(row < nvalid, dt, 0.0)
    dta = dt * a_ref[...]
    causal = row >= col
    cum = jnp.dot(causal.astype(F32), dta, preferred_element_type=F32, precision=lax.Precision.HIGHEST)
    cum_t = cum.T
    ecum = jnp.exp(cum)
    toend = jnp.exp(cum[q - 1:q, :] - cum)
    dt_e = _expand_heads(dt)
    ecum_e = _expand_heads(ecum)
    toend_e = _expand_heads(toend)
    xdt = xs * dt_e
    xdt_b = xdt.astype(BF16)
    xw_b = (xdt * toend_e).astype(BF16)
    lane = lax.broadcasted_iota(jnp.int32, (q, LANES), 1)

    y_intra = []
    y_inter = []
    for g in range(SSD_GROUPS):
        bg = bm[:, g * 128:(g + 1) * 128]
        cg_b = cm[:, g * 128:(g + 1) * 128].astype(BF16)
        bg_b = bg.astype(BF16)
        cb = _dot_nt(cg_b, bg_b)
        htg = ht[:, g * 256:(g + 1) * 256]
        y_inter.append(_dot(cg_b, htg.astype(BF16)))
        for kk in range(2):
            slab = xdt_b[:, (2 * g + kk) * LANES:(2 * g + kk + 1) * LANES]
            res = []
            for hh in range(2):
                h = 4 * g + 2 * kk + hh
                seg = cum[:, h:h + 1] - cum_t[h:h + 1, :]
                decay = jnp.exp(jnp.where(causal, seg, NEG))
                res.append(_dot((decay * cb).astype(BF16), slab))
            y_intra.append(jnp.where(lane < SSD_HEAD_DIM, res[0], res[1]))
        s_new = _dot(bg.T.astype(BF16), xw_b[:, g * 256:(g + 1) * 256])
        ht[:, g * 256:(g + 1) * 256] = htg * ecum_e[q - 1:q, g * 256:(g + 1) * 256] + s_new

    y = jnp.concatenate(y_intra, axis=1) + jnp.concatenate(y_inter, axis=1) * ecum_e + dsk_ref[...] * xs
    y = y * _silu(z_ref[0])
    y_ref[0] = _rms(y, ng_ref[...])

    xpad[0:8, :] = xpad[q:q + 8, :]

    @pl.when(c == nchunks - 1)
    def _():
        hist_o_ref[0] = xpad[valid_last:valid_last + 8, :]
        hfin_ref[0] = ht[...].T


def _ssd(z, xbc, dtg, hist8, h0, cw8, cb, dtb, a, dsk, ng, valid_last):
    b, l = z.shape[:2]
    nchunks = l // SSD_CHUNK
    rows = lambda n: pl.BlockSpec((1, SSD_CHUNK, n), lambda i, c: (i, c, 0))
    perb = lambda r, n: pl.BlockSpec((1, r, n), lambda i, c: (i, 0, 0))
    full = lambda arr: pl.BlockSpec(arr.shape, lambda i, c: (0,) * arr.ndim)
    kern = functools.partial(_ssd_kernel, nchunks=nchunks, valid_last=valid_last)
    return pl.pallas_call(
        kern,
        grid=(b, nchunks),
        in_specs=[rows(512), rows(1024), rows(128), perb(8, 1024), perb(512, 128),
                  full(cw8), full(cb), full(dtb), full(a), full(dsk), full(ng)],
        out_specs=[rows(512), perb(8, 1024), perb(512, 128)],
        out_shape=[jax.ShapeDtypeStruct((b, l, 512), F32),
                   jax.ShapeDtypeStruct((b, 8, 1024), F32),
                   jax.ShapeDtypeStruct((b, 512, 128), F32)],
        scratch_shapes=[pltpu.VMEM((SSD_CHUNK + 8, CONV_DIM), F32), pltpu.VMEM((SSD_STATE, D_SSD), F32)],
        compiler_params=_cparams(("arbitrary", "arbitrary")),
    )(z, xbc, dtg, hist8, h0, cw8, cb, dtb, a, dsk, ng)


CMP_CHUNK = 256
SUM_ROWS = 16
NSA_TK = 512
CMP_GROUP_PAGES = 16


def _compress_kernel(pt_ref, pgk_ref, pgv_ref, w1_ref, pe_ref, b1_ref, w2_ref, w2t_ref, kchi_ref, kclo_ref, vct_ref,
                     xs, ps, *, npages):
    p = pl.program_id(1)
    sub = p % CMP_GROUP_PAGES
    hb = PAGE // CMP_STRIDE
    r0 = pl.multiple_of(sub * hb, hb)
    for kv, pg_ref in enumerate((pgk_ref, pgv_ref)):
        for pos in range(CMP_STRIDE):
            xs[kv, pl.ds(r0, hb), pos * LANES:(pos + 1) * LANES] = pg_ref[0, pl.ds(pos, hb, stride=CMP_STRIDE), :]

    @pl.when(sub == CMP_GROUP_PAGES - 1)
    def _():
        rows = CMP_GROUP_PAGES * hb
        g0 = pl.multiple_of((p // CMP_GROUP_PAGES) * rows, rows)
        for kv in range(2):
            x = xs[kv]
            for s in range(2):
                ps[kv, s, pl.ds(g0, rows), :] = _dot((x + pe_ref[kv, s:s + 1, :]).astype(BF16), w1_ref[kv, s])

    @pl.when(p == npages - 1)
    def _():
        nh = npages * hb
        act = [jax.nn.gelu(b1_ref[kv:kv + 1, :] + ps[kv, 0] + pltpu.roll(ps[kv, 1], nh - 1, 0)).astype(BF16)
               for kv in range(2)]
        kc = _dot(act[0], w2_ref[0])
        hi = kc.astype(BF16)
        kchi_ref[0] = hi
        kclo_ref[0] = (kc - hi.astype(F32)).astype(BF16)
        vct_ref[0] = _dot_nt(w2t_ref[...], act[1]).astype(BF16)


def _compress(pages, table, w1e, pee, b1e, w2e):
    b, npages = table.shape
    w2t = w2e[1].T
    nh = npages * (PAGE // CMP_STRIDE)
    kern = functools.partial(_compress_kernel, npages=npages)
    full = lambda arr: pl.BlockSpec(arr.shape, lambda i, p, pt: (0,) * arr.ndim)
    return pl.pallas_call(
        kern,
        grid_spec=pltpu.PrefetchScalarGridSpec(
            num_scalar_prefetch=1,
            grid=(b, npages),
            in_specs=[pl.BlockSpec((1, PAGE, LANES), lambda i, p, pt: (pt[i, p], 0, 0)),
                      pl.BlockSpec((1, PAGE, LANES), lambda i, p, pt: (pt[i, p], 0, 1)),
                      full(w1e), full(pee), full(b1e), full(w2e), full(w2t)],
            out_specs=[pl.BlockSpec((1, nh, LANES), lambda i, p, pt: (i, 0, 0)),
                       pl.BlockSpec((1, nh, LANES), lambda i, p, pt: (i, 0, 0)),
                       pl.BlockSpec((1, LANES, nh), lambda i, p, pt: (i, 0, 0))],
            scratch_shapes=[pltpu.VMEM((2, CMP_GROUP_PAGES * 8, CMP_STRIDE * LANES), F32),
                            pltpu.VMEM((2, 2, nh, 256), F32)]),
        out_shape=[jax.ShapeDtypeStruct((b, nh, LANES), BF16), jax.ShapeDtypeStruct((b, nh, LANES), BF16),
                   jax.ShapeDtypeStruct((b, LANES, nh), BF16)],
        compiler_params=_cparams(("arbitrary", "arbitrary")),
    )(table, pages, pages, w1e, pee, b1e, w2e, w2t)


def _kvfmt_kernel(kv_ref, win_ref, k_ref, vt_ref, kw_ref, vwt_ref):
    p = pl.program_id(0)
    src = kv_ref[...]
    r = lax.broadcasted_iota(jnp.int32, (PAGE, LANES), 0)
    lane = lax.broadcasted_iota(jnp.int32, (PAGE, LANES), 1)
    blk = (2 * p + r // SLC_BLOCK) % SLAB_BLOCKS
    k_ref[:, 0:128] = src[:, 0:128].astype(BF16)
    k_ref[:, 128:256] = jnp.where(lane == blk, 1.0, 0.0).astype(BF16)
    v_t = src[:, 128:256].T.astype(BF16)
    for g in range(KV_HEADS):
        vt_ref[0, g, 0:HEAD_DIM, :] = v_t[g * HEAD_DIM:(g + 1) * HEAD_DIM, :]
        vt_ref[0, g, HEAD_DIM:HEAD_DIM + SUM_ROWS, :] = jnp.ones((SUM_ROWS, PAGE), BF16)
    kw_ref[...] = win_ref[:, 0:128].astype(BF16)
    vwt_ref[0] = win_ref[:, 128:256].T.astype(BF16)


def _kvfmt(kv, win, tk):
    l = kv.shape[0]
    per = tk // PAGE
    return pl.pallas_call(
        _kvfmt_kernel,
        grid=(l // PAGE,),
        in_specs=[pl.BlockSpec((PAGE, 256), lambda p: (p, 1)), pl.BlockSpec((PAGE, 256), lambda p: (p, 0))],
        out_specs=[pl.BlockSpec((PAGE, 256), lambda p: (p, 0)),
                   pl.BlockSpec((1, KV_HEADS, HEAD_DIM + SUM_ROWS, PAGE), lambda p: (p // per, 0, 0, p % per)),
                   pl.BlockSpec((PAGE, LANES), lambda p: (p, 0)),
                   pl.BlockSpec((1, LANES, PAGE), lambda p: (p, 0, 0))],
        out_shape=[jax.ShapeDtypeStruct((l, 256), BF16),
                   jax.ShapeDtypeStruct((l // tk, KV_HEADS, HEAD_DIM + SUM_ROWS, tk), BF16),
                   jax.ShapeDtypeStruct((l, LANES), BF16),
                   jax.ShapeDtypeStruct((l // PAGE, LANES, PAGE), BF16)],
        compiler_params=_cparams(("arbitrary",)),
    )(kv, win)


def _group_rows(qfull, g, tq):
    lane = lax.broadcasted_iota(jnp.int32, (tq, LANES), 1)
    keep = (lane >= HEAD_DIM) if g else (lane < HEAD_DIM)
    pieces = []
    for r in range(4):
        h = 4 * g + r
        slab = qfull[:, (h // 2) * LANES:(h // 2 + 1) * LANES]
        if h % 2 != g:
            slab = pltpu.roll(slab, HEAD_DIM, 1)
        pieces.append(jnp.where(keep, slab, 0.0))
    return jnp.concatenate(pieces, axis=0)


def _masked_softmax(s, mask):
    s = jnp.where(mask, s, NEG)
    m = jnp.max(s, axis=-1, keepdims=True)
    e = jnp.where(mask, jnp.exp(s - m), 0.0)
    return e / jnp.maximum(jnp.sum(e, axis=-1, keepdims=True), 1e-30)


def _importance(p4, ov):
    h1, h2, h3 = _split3(p4)
    return _dot(h1, ov) + (_dot(h2, ov) + _dot(h3, ov))


def _topk_bias(imp, qpos, ns):
    tq, nsp = imp.shape
    jf = lax.broadcasted_iota(jnp.int32, (tq, nsp), 1)
    jff = jf.astype(F32)
    cur = qpos >> 6
    valid = jf <= cur
    forced = valid & ((jf == 0) | (jf == cur) | (jf == cur - 1))
    score = jnp.where(forced, FORCE_BONUS, jnp.where(valid, imp, -1.0))
    score = jnp.where(jf < ns, score, -2.0)

    def pick(_, carry):
        sc, sb = carry
        mx = jnp.max(sc, axis=-1, keepdims=True)
        jm = jnp.min(jnp.where(sc == mx, jff, float(nsp)), axis=-1, keepdims=True)
        hit = jff == jm
        return jnp.where(hit, -3.0, sc), jnp.where(hit, 0.0, sb)

    return lax.fori_loop(0, TOP_N, pick, (score, jnp.full((tq, nsp), NEG, F32)))[1]


def _gate_cols(sg, g, k, tq):
    return jnp.concatenate([sg[:, 8 + 3 * (4 * g + r) + k:8 + 3 * (4 * g + r) + k + 1] for r in range(4)], axis=0)


def _emit_heads(outs, tq, y_ref):
    lane = lax.broadcasted_iota(jnp.int32, (tq, LANES), 1)
    for k in range(4):
        g = k // 2
        r0 = (2 * k) % 4
        a = outs[g][r0 * tq:(r0 + 1) * tq]
        b = outs[g][(r0 + 1) * tq:(r0 + 2) * tq]
        if g == 0:
            slab = jnp.where(lane < HEAD_DIM, a, pltpu.roll(b, HEAD_DIM, 1))
        else:
            slab = jnp.where(lane < HEAD_DIM, pltpu.roll(a, HEAD_DIM, 1), b)
        y_ref[0, :, k * LANES:(k + 1) * LANES] = slab


def _masked_softmax_t(s, mask):
    s = jnp.where(mask, s, NEG)
    m = jnp.max(s, axis=0, keepdims=True)
    e = jnp.exp(s - m)
    inv = jnp.where(m > 0.5 * NEG, 1.0 / jnp.maximum(jnp.sum(e, axis=0, keepdims=True), 1e-30), 0.0)
    return e * inv


def _topk_bias_t(imp, qpos, ns):
    nsp, tq = imp.shape
    jf = lax.broadcasted_iota(jnp.int32, (nsp, tq), 0)
    jff = jf.astype(F32)
    cur = qpos >> 6
    valid = jf <= cur
    forced = valid & ((jf == 0) | (jf == cur) | (jf == cur - 1))
    score = jnp.where(forced, FORCE_BONUS, jnp.where(valid, imp, -1.0))
    score = jnp.where(jf < ns, score, -2.0)

    def pick(_, carry):
        sc, sb = carry
        mx = jnp.max(sc, axis=0, keepdims=True)
        jm = jnp.min(jnp.where(sc == mx, jff, float(nsp)), axis=0, keepdims=True)
        hit = jff == jm
        return jnp.where(hit, -3.0, sc), jnp.where(hit, 0.0, sb)

    return lax.fori_loop(0, TOP_N, pick, (score, jnp.full((nsp, tq), NEG, F32)))[1]


def _nsa_kernel(qn_ref, qr_ref, dtg_ref, kchi_ref, kclo_ref, vct_ref, kaug_ref, vt_ref, kwin_ref, vwt_ref,
                y_ref, qat, p4_sc, cs_sc, sa_sc, sb_sc, m_sc, acc_sc, *, tq, ns, nsp, tk, wl):
    t = pl.program_id(1)
    scale = HEAD_DIM ** -0.5
    r4 = 4 * tq
    nslab = nsp // LANES
    lane4 = lax.broadcasted_iota(jnp.int32, (1, r4), 1)
    posr = t * tq + (lane4 & (tq - 1))
    qpos = t * tq + lax.broadcasted_iota(jnp.int32, (1, tq), 1)

    qn = qn_ref[0] * scale
    qr = qr_ref[0] * scale
    ncb = kchi_ref.shape[1]
    cch = cs_sc.shape[1]
    cmax = (t * tq + tq - CMP_BLOCK) // CMP_STRIDE
    nch = jnp.minimum(cmax // cch + 1, ncb // cch)

    ws = pl.multiple_of(jnp.maximum(t * tq - WINDOW, 0), LANES)
    kw = kwin_ref[pl.ds(ws, wl), :]
    kposw = ws + lax.broadcasted_iota(jnp.int32, (wl, 1), 0)
    dpos = posr - kposw
    mask_w = (dpos >= 0) & (dpos < WINDOW)
    wpage = ws // PAGE

    p4_sc[0:8, :] = jnp.zeros((8, tq), F32)
    p4_sc[8 + ncb:, :] = jnp.zeros((p4_sc.shape[0] - 8 - ncb, tq), F32)
    o_cmp = []
    o_win = []
    imps = []
    for g in range(KV_HEADS):
        qn_t = _group_rows(qn, g, tq).T
        qr_t = _group_rows(qr, g, tq).T
        qh = qn_t.astype(BF16)
        ql = (qn_t - qh.astype(F32)).astype(BF16)

        def cmp_scores(k, m):
            r0 = pl.multiple_of(k * cch, cch)
            kh = kchi_ref[0, pl.ds(r0, cch), :]
            kl = kclo_ref[0, pl.ds(r0, cch), :]
            s = _dot(kh, qh) + (_dot(kh, ql) + _dot(kl, qh))
            c_end = (r0 + lax.broadcasted_iota(jnp.int32, (cch, 1), 0)) * CMP_STRIDE + (CMP_BLOCK - 1)
            s = jnp.where(c_end <= posr, s, NEG)
            cs_sc[k] = s
            return jnp.maximum(m, jnp.max(s, axis=0, keepdims=True))

        m_c = lax.fori_loop(0, nch, cmp_scores, jnp.full((1, r4), NEG, F32))

        def cmp_exp(k, l):
            e = jnp.exp(cs_sc[k] - m_c)
            cs_sc[k] = e
            return l + jnp.sum(e, axis=0, keepdims=True)

        l_c = lax.fori_loop(0, nch, cmp_exp, jnp.zeros((1, r4), F32))
        inv_c = jnp.where(m_c > 0.5 * NEG, 1.0 / jnp.maximum(l_c, 1e-30), 0.0)

        def cmp_out(k, o):
            r0 = pl.multiple_of(k * cch, cch)
            p = cs_sc[k] * inv_c
            p4_sc[pl.ds(8 + r0, cch), :] = p[:, 0:tq] + p[:, tq:2 * tq] + p[:, 2 * tq:3 * tq] + p[:, 3 * tq:4 * tq]
            return o + _dot(vct_ref[k], p.astype(BF16))

        o_cmp.append(lax.fori_loop(0, nch, cmp_out, jnp.zeros((LANES, r4), F32)))

        def cmp_zero(k, carry):
            r0 = pl.multiple_of(k * cch, cch)
            p4_sc[pl.ds(8 + r0, cch), :] = jnp.zeros((cch, tq), F32)
            return carry

        lax.fori_loop(nch, ncb // cch, cmp_zero, 0)
        per = SLC_BLOCK // CMP_STRIDE
        imp = p4_sc[pl.ds(8 - 1, nsp, stride=per), :]
        for k in range(per):
            imp = imp + p4_sc[pl.ds(8 + k, nsp, stride=per), :]
        imps.append(imp)
        qs_b = (qr_t * LOG2E).astype(BF16)
        for sl in range(nslab):
            qat[sl, g, 0:LANES, :] = qs_b
        p_w = _masked_softmax_t(_dot(kw, qr_t.astype(BF16)), mask_w).astype(BF16)
        o_w = _dot(vwt_ref[wpage], p_w[0:PAGE])
        for i in range(1, wl // PAGE):
            o_w = o_w + _dot(vwt_ref[wpage + i], p_w[i * PAGE:(i + 1) * PAGE])
        o_win.append(o_w)

    def select(rows):
        for g in range(KV_HEADS):
            selb = _topk_bias_t(imps[g][0:rows], qpos, min(ns, rows))
            if rows < nsp:
                selb = jnp.concatenate([selb, jnp.full((nsp - rows, tq), NEG, F32)], axis=0)
            for sl in range(nslab):
                piece = selb[sl * LANES:(sl + 1) * LANES, :]
                qat[sl, g, LANES:2 * LANES, :] = jnp.concatenate([piece] * 4, axis=1).astype(BF16)

    half = nsp // 2
    if half % 8 == 0 and half >= TOP_N:
        last_block = (t * tq + tq - 1) // SLC_BLOCK

        @pl.when(last_block < half)
        def _():
            select(half)

        @pl.when(last_block >= half)
        def _():
            select(nsp)
    else:
        select(nsp)

    m_sc[...] = jnp.full(m_sc.shape, 2.0 * NEG, F32)
    acc_sc[...] = jnp.zeros(acc_sc.shape, F32)
    ndiag = (t * tq + tq - 1) // tk

    def scores(jt, s_buf):
        k0 = pl.multiple_of(jt * tk, tk)
        kt = kaug_ref[pl.ds(k0, tk), :]
        sb = k0 // (SLAB_BLOCKS * SLC_BLOCK)
        for g in range(KV_HEADS):
            s_buf[g] = _dot(kt, qat[sb, g])

    def accumulate(jt, s_buf, causal):
        for g in range(KV_HEADS):
            s = s_buf[g]
            if causal:
                kpos = jt * tk + lax.broadcasted_iota(jnp.int32, (tk, 1), 0)
                s = jnp.where(kpos <= posr, s, NEG)
            m_old = m_sc[g]
            m_new = jnp.maximum(m_old, jnp.max(s, axis=0, keepdims=True))
            pe = jnp.exp2(s - m_new).astype(BF16)
            acc_sc[g] = jnp.exp2(m_old - m_new) * acc_sc[g] + _dot(vt_ref[jt, g], pe)
            m_sc[g] = m_new

    scores(0, sa_sc)

    def tile_pair(p, carry):
        jt = 2 * p
        scores(jt + 1, sb_sc)
        accumulate(jt, sa_sc, False)
        scores(jt + 2, sa_sc)
        accumulate(jt + 1, sb_sc, False)
        return carry

    npair = ndiag // 2
    lax.fori_loop(0, npair, tile_pair, 0)

    @pl.when(ndiag % 2 == 0)
    def _():
        accumulate(ndiag, sa_sc, True)

    @pl.when(ndiag % 2 == 1)
    def _():
        scores(ndiag, sb_sc)
        accumulate(ndiag - 1, sa_sc, False)
        accumulate(ndiag, sb_sc, True)

    sg_t = jax.nn.sigmoid(dtg_ref[0]).T
    outs = []
    for g in range(KV_HEADS):
        gate = [jnp.concatenate([sg_t[8 + 3 * (4 * g + r) + k:8 + 3 * (4 * g + r) + k + 1, :] for r in range(4)], axis=1)
                for k in range(3)]
        o_sel = acc_sc[g, 0:HEAD_DIM, :] / acc_sc[g, HEAD_DIM:HEAD_DIM + 1, :]
        zero = jnp.zeros((HEAD_DIM, r4), F32)
        o_sel = jnp.concatenate([o_sel, zero] if g == 0 else [zero, o_sel], axis=0)
        o_t = gate[0] * o_cmp[g] + gate[1] * o_sel + gate[2] * o_win[g]
        outs.append(o_t.T)
    _emit_heads(outs, tq, y_ref)


def _nsa(qn, qr, dtg, kchi, kclo, vct, kaug, vt, kwin, vwt, *, tq, ns, nsp, tk, wl):
    b, lq = qn.shape[:2]
    ncb = kchi.shape[1]
    cch = min(CMP_CHUNK, ncb)
    assert nsp * (SLC_BLOCK // CMP_STRIDE) >= ncb and ncb % cch == 0
    vct = vct[0].reshape(LANES, ncb // cch, cch).transpose(1, 0, 2)
    nslab = nsp // LANES
    kern = functools.partial(_nsa_kernel, tq=tq, ns=ns, nsp=nsp, tk=tk, wl=wl)
    rows = lambda n: pl.BlockSpec((1, tq, n), lambda i, t: (i, t, 0))
    whole = lambda arr: pl.BlockSpec(arr.shape, lambda i, t: (0,) * arr.ndim, pipeline_mode=pl.Buffered(1))
    return pl.pallas_call(
        kern,
        grid=(b, lq // tq),
        in_specs=[rows(512), rows(512), rows(128), whole(kchi), whole(kclo), whole(vct),
                  whole(kaug), whole(vt), whole(kwin), whole(vwt)],
        out_specs=rows(512),
        out_shape=jax.ShapeDtypeStruct((b, lq, 512), F32),
        scratch_shapes=[pltpu.VMEM((nslab, KV_HEADS, 2 * LANES, 4 * tq), BF16),
                        pltpu.VMEM((nsp * (SLC_BLOCK // CMP_STRIDE) + 16, tq), F32),
                        pltpu.VMEM((ncb // cch, cch, 4 * tq), F32),
                        pltpu.VMEM((KV_HEADS, tk, 4 * tq), F32),
                        pltpu.VMEM((KV_HEADS, tk, 4 * tq), F32),
                        pltpu.VMEM((KV_HEADS, 1, 4 * tq), F32),
                        pltpu.VMEM((KV_HEADS, HEAD_DIM + SUM_ROWS, 4 * tq), F32)],
        compiler_params=_cparams(("arbitrary", "arbitrary")),
    )(qn, qr, dtg, kchi, kclo, vct, kaug, vt, kwin, vwt)


SAMPLE_PAGES_PER_STEP = 16
SAMPLE_TQ = 8


def _nsa_sample_kernel(pt_ref, *refs, npages, past, ns, nsp):
    pps = SAMPLE_PAGES_PER_STEP
    tq = SAMPLE_TQ
    pages = refs[:pps]
    (tail_ref, qn_ref, qr_ref, dtg_ref, w1_ref, pe_ref, b1_ref, w2_ref, ov_ref, cwin_ref, wtail_ref,
     y_ref, tsc, xs, ps, s_all, vt_all, bias_sc) = refs[pps:]
    j = pl.program_id(1)
    ngroups = npages // pps
    hb = PAGE // CMP_STRIDE
    scale = HEAD_DIM ** -0.5
    qr_aug = jnp.concatenate([_group_rows(qr_ref[0] * scale, g, tq) for g in range(KV_HEADS)], axis=0).astype(BF16)

    for i, pg in enumerate(pages):
        for kv in range(2):
            tsc[i, kv] = pg[0, 0, kv * LANES:(kv + 1) * LANES, :].T
            for pos in range(CMP_STRIDE):
                xs[kv, i * hb:(i + 1) * hb, pos * LANES:(pos + 1) * LANES] = tsc[i, kv, pl.ds(pos, hb, stride=CMP_STRIDE), :]
        page_idx = j * pps + i
        s_all[page_idx] = _dot(qr_aug, pg[0, 0, 256:384, :].astype(BF16))
        vt_all[page_idx] = pg[0, 0, 384:512, :].astype(BF16)
    rows = pps * hb
    g0 = pl.multiple_of(j * rows, rows)
    for kv in range(2):
        x = xs[kv]
        for s in range(2):
            ps[kv, s, pl.ds(g0, rows), :] = _dot((x + pe_ref[kv, s:s + 1, :]).astype(BF16), w1_ref[kv, s])

    @pl.when(j == ngroups - 1)
    def _():
        nh = npages * hb
        kcvc = []
        for kv in range(2):
            hid = b1_ref[kv:kv + 1, :] + ps[kv, 0] + pltpu.roll(ps[kv, 1], nh - 1, 0)
            kcvc.append(_dot(jax.nn.gelu(hid).astype(BF16), w2_ref[kv]))
        kc, vc = kcvc
        qpos = past + lax.broadcasted_iota(jnp.int32, (tq, 1), 0)
        pos_r = jnp.concatenate([qpos] * (4 * KV_HEADS), axis=0)
        nrow = 4 * KV_HEADS * tq
        s_all[npages] = _dot(qr_aug, tail_ref[0, 256:384, :].astype(BF16))
        vt_all[npages] = tail_ref[0, 384:512, :].astype(BF16)

        qn_aug = jnp.concatenate([_group_rows(qn_ref[0] * scale, g, tq) for g in range(KV_HEADS)], axis=0)
        c_end = lax.broadcasted_iota(jnp.int32, (1, nh), 1) * CMP_STRIDE + (CMP_BLOCK - 1)
        p_c = _masked_softmax(_dot_nt_hi(qn_aug, kc), c_end <= pos_r)
        o_c = _dot(p_c.astype(BF16), vc.astype(BF16))
        imps = []
        for g in range(KV_HEADS):
            base = 4 * tq * g
            p4 = (p_c[base:base + tq] + p_c[base + tq:base + 2 * tq]
                  + p_c[base + 2 * tq:base + 3 * tq] + p_c[base + 3 * tq:base + 4 * tq])
            imps.append(_importance(p4, ov_ref[...]))
        selb = _topk_bias(jnp.concatenate(imps, axis=0), jnp.concatenate([qpos] * KV_HEADS, axis=0), ns)
        lane_b = lax.broadcasted_iota(jnp.int32, (KV_HEADS * tq, LANES), 1)
        per_page = PAGE // SLC_BLOCK
        assert per_page == 2 and nsp >= per_page * (npages + 1)
        for pp in range(npages + 1):
            lo = jnp.broadcast_to(selb[:, 2 * pp:2 * pp + 1], (KV_HEADS * tq, LANES))
            hi = jnp.broadcast_to(selb[:, 2 * pp + 1:2 * pp + 2], (KV_HEADS * tq, LANES))
            bias_sc[pp] = jnp.where(lane_b < SLC_BLOCK, lo, hi)

        def biased(pidx):
            b = bias_sc[pidx]
            b_r = jnp.concatenate([b[0:tq]] * 4 + [b[tq:2 * tq]] * 4, axis=0)
            return s_all[pidx] + b_r

        def p1(pidx, m):
            s = biased(pidx)
            s_all[pidx] = s
            return jnp.maximum(m, s)

        m = lax.fori_loop(0, npages, p1, jnp.full((nrow, LANES), 2.0 * NEG, F32))
        kpos_t = past + lax.broadcasted_iota(jnp.int32, (1, PAGE), 1)
        s_t = jnp.where(kpos_t <= pos_r, biased(npages), NEG)
        s_all[npages] = s_t
        mrow = jnp.max(jnp.maximum(m, s_t), axis=-1, keepdims=True)

        def p2(pidx, carry):
            l, acc = carry
            p = jnp.exp(s_all[pidx] - mrow)
            return l + p, acc + _dot_nt(p.astype(BF16), vt_all[pidx])

        zero = jnp.zeros((nrow, LANES), F32)
        l, acc = p2(npages, lax.fori_loop(0, npages, p2, (zero, zero), unroll=8))
        o_s = acc / jnp.sum(l, axis=-1, keepdims=True)

        wbuf = cwin_ref.shape[3]
        kw_b = cwin_ref[0, 0, 0:128, :].astype(BF16)
        vw_b = cwin_ref[0, 0, 128:256, :].astype(BF16)
        kt_b = wtail_ref[0, 0:128, :].astype(BF16)
        vt_b = wtail_ref[0, 128:256, :].astype(BF16)
        s_w = jnp.concatenate([_dot(qr_aug, kw_b), _dot(qr_aug, kt_b)], axis=1)
        kposw = jnp.concatenate([past - wbuf + lax.broadcasted_iota(jnp.int32, (1, wbuf), 1), kpos_t], axis=1)
        dpos = pos_r - kposw
        p_w = _masked_softmax(s_w, (dpos >= 0) & (dpos < WINDOW) & (kposw >= 0))
        o_w = _dot_nt(p_w[:, 0:wbuf].astype(BF16), vw_b) + _dot_nt(p_w[:, wbuf:].astype(BF16), vt_b)

        sg = jax.nn.sigmoid(dtg_ref[0])
        outs = []
        for g in range(KV_HEADS):
            sl = slice(4 * tq * g, 4 * tq * (g + 1))
            outs.append(_gate_cols(sg, g, 0, tq) * o_c[sl] + _gate_cols(sg, g, 1, tq) * o_s[sl]
                        + _gate_cols(sg, g, 2, tq) * o_w[sl])
        _emit_heads(outs, tq, y_ref)


def _nsa_sample(layer, cache_t, table, tail_t, qn, qr, dtg, w1e, pee, b1e, w2e, ov, cwin_t, wtail_t, *, past, ns):
    b, npages = table.shape
    pps = SAMPLE_PAGES_PER_STEP
    tq = SAMPLE_TQ
    nsp = ov.shape[1]
    nh = npages * (PAGE // CMP_STRIDE)
    nrow = 4 * KV_HEADS * tq
    kern = functools.partial(_nsa_sample_kernel, npages=npages, past=past, ns=ns, nsp=nsp)
    const = lambda arr: pl.BlockSpec(arr.shape, lambda i, j, pt: (0,) * arr.ndim, pipeline_mode=pl.Buffered(1))
    perb = lambda arr: pl.BlockSpec((1,) + arr.shape[1:], lambda i, j, pt: (i,) + (0,) * (arr.ndim - 1))

    def page_spec(k):
        return pl.BlockSpec((1, 1, 512, PAGE), lambda i, j, pt: (layer, pt[i, j * pps + k], 0, 0))

    return pl.pallas_call(
        kern,
        grid_spec=pltpu.PrefetchScalarGridSpec(
            num_scalar_prefetch=1,
            grid=(b, npages // pps),
            in_specs=[page_spec(k) for k in range(pps)] + [
                perb(tail_t), perb(qn), perb(qr), perb(dtg), const(w1e), const(pee), const(b1e), const(w2e),
                const(ov),
                pl.BlockSpec((1, 1) + cwin_t.shape[2:], lambda i, j, pt: (layer, i, 0, 0)), perb(wtail_t)],
            out_specs=pl.BlockSpec((1, tq, 512), lambda i, j, pt: (i, 0, 0)),
            scratch_shapes=[pltpu.VMEM((pps, 2, PAGE, LANES), F32),
                            pltpu.VMEM((2, pps * 8, CMP_STRIDE * LANES), F32),
                            pltpu.VMEM((2, 2, nh, 256), F32),
                            pltpu.VMEM((npages + 1, nrow, LANES), F32),
                            pltpu.VMEM((npages + 1, LANES, PAGE), BF16),
                            pltpu.VMEM((npages + 1, KV_HEADS * tq, LANES), F32)]),
        out_shape=jax.ShapeDtypeStruct((b, tq, 512), F32),
        compiler_params=_cparams(("arbitrary", "arbitrary")),
    )(table, *([cache_t] * pps), tail_t, qn, qr, dtg, w1e, pee, b1e, w2e, ov, cwin_t, wtail_t)


def _oproj_kernel(x_ref, ys_ref, yn_ref, w_ref, g_ref, r_ref, x1_ref, h_ref, lg_ref):
    a = jnp.concatenate([ys_ref[...], yn_ref[...]], axis=1).astype(BF16)
    x1 = x_ref[...] + _dot(a, w_ref[...])
    x1_ref[...] = x1
    h = _rms(x1, g_ref[...])
    h_ref[...] = h.astype(h_ref.dtype)
    lg_ref[...] = _dot_hi(h, r_ref[...])


def _oproj(x, ys, yn, w, g, router, tm, h_dtype):
    t = x.shape[0]
    row = lambda n: pl.BlockSpec((tm, n), lambda i: (i, 0))
    full = lambda a: pl.BlockSpec(a.shape, lambda i: (0,) * a.ndim)
    return pl.pallas_call(
        _oproj_kernel,
        grid=(t // tm,),
        in_specs=[row(D_MODEL), row(512), row(512), full(w), full(g), full(router)],
        out_specs=[row(D_MODEL), row(D_MODEL), row(LANES)],
        out_shape=[jax.ShapeDtypeStruct((t, D_MODEL), F32), jax.ShapeDtypeStruct((t, D_MODEL), h_dtype),
                   jax.ShapeDtypeStruct((t, LANES), F32)],
        compiler_params=_cparams(("arbitrary",)),
    )(x, ys, yn, w, g, router)


def _ffn_kernel(h_ref, x1_ref, wg_ref, wu_ref, wd_ref, o_ref):
    f = pl.program_id(1)

    @pl.when(f == 0)
    def _():
        o_ref[...] = x1_ref[...]

    hb = h_ref[...]
    act = _silu(_dot(hb, wg_ref[...])) * _dot(hb, wu_ref[...])
    o_ref[...] += _dot(act.astype(BF16), wd_ref[...])


def _ffn(h, x1, wg, wu, wd, tm, tf):
    t = h.shape[0]
    dff = wg.shape[1]
    return pl.pallas_call(
        _ffn_kernel,
        grid=(t // tm, dff // tf),
        in_specs=[pl.BlockSpec((tm, D_MODEL), lambda i, f: (i, 0)),
                  pl.BlockSpec((tm, D_MODEL), lambda i, f: (i, 0)),
                  pl.BlockSpec((D_MODEL, tf), lambda i, f: (0, f)),
                  pl.BlockSpec((D_MODEL, tf), lambda i, f: (0, f)),
                  pl.BlockSpec((tf, D_MODEL), lambda i, f: (f, 0))],
        out_specs=pl.BlockSpec((tm, D_MODEL), lambda i, f: (i, 0)),
        out_shape=jax.ShapeDtypeStruct((t, D_MODEL), F32),
        compiler_params=_cparams(("arbitrary", "arbitrary")),
    )(h, x1, wg, wu, wd)


def _moe_kernel(be_ref, nb_ref, xb_ref, wg_ref, wu_ref, wd_ref, o_ref, xb16):
    b = pl.program_id(0)
    f = pl.program_id(1)

    @pl.when(f == 0)
    def _():
        o_ref[...] = jnp.zeros(o_ref.shape, F32)
        xb16[...] = xb_ref[...].astype(BF16)

    @pl.when(b < nb_ref[0])
    def _():
        hb = xb16[...]
        act = _silu(_dot(hb, wg_ref[0].astype(BF16))) * _dot(hb, wu_ref[0].astype(BF16))
        o_ref[...] += _dot(act.astype(BF16), wd_ref[0].astype(BF16))


def _moe(blk_e, nblk_used, xb, wg, wu, wd, tmb, tf):
    n_blk = xb.shape[0] // tmb
    dex = wg.shape[2]
    nf = dex // tf

    def feff(b, f, nb):
        return jnp.where(b < nb[0], f, nf - 1)

    return pl.pallas_call(
        _moe_kernel,
        grid_spec=pltpu.PrefetchScalarGridSpec(
            num_scalar_prefetch=2,
            grid=(n_blk, nf),
            in_specs=[pl.BlockSpec((tmb, D_MODEL), lambda b, f, be, nb: (b, 0)),
                      pl.BlockSpec((1, D_MODEL, tf), lambda b, f, be, nb: (be[b], 0, feff(b, f, nb))),
                      pl.BlockSpec((1, D_MODEL, tf), lambda b, f, be, nb: (be[b], 0, feff(b, f, nb))),
                      pl.BlockSpec((1, tf, D_MODEL), lambda b, f, be, nb: (be[b], feff(b, f, nb), 0))],
            out_specs=pl.BlockSpec((tmb, D_MODEL), lambda b, f, be, nb: (b, 0)),
            scratch_shapes=[pltpu.VMEM((tmb, D_MODEL), BF16)]),
        out_shape=jax.ShapeDtypeStruct((n_blk * tmb, D_MODEL), F32),
        compiler_params=_cparams(("arbitrary", "arbitrary")),
    )(blk_e, nblk_used, xb, wg, wu, wd)


def _moe_layer(h, x1, logits, wg, wu, wd, tmb, tf):
    t = h.shape[0]
    top_v, top_e = lax.top_k(logits[:, :N_EXPERTS], TOP_K)
    gate = jax.nn.softmax(top_v, axis=-1)
    e_flat = top_e.reshape(-1)
    npair = t * TOP_K
    order = jnp.argsort(e_flat)
    rank = jnp.argsort(order).astype(jnp.int32)
    counts = jnp.bincount(e_flat, length=N_EXPERTS).astype(jnp.int32)
    starts = jnp.cumsum(counts) - counts
    padded = (counts + tmb - 1) // tmb * tmb
    pends = jnp.cumsum(padded)
    pstarts = pends - padded
    dest = (pstarts[e_flat] + rank - starts[e_flat]).astype(jnp.int32).reshape(t, TOP_K)
    n_blk = -(-npair // tmb) + N_EXPERTS
    blk_e = jnp.minimum(jnp.searchsorted(pends, jnp.arange(n_blk, dtype=jnp.int32) * tmb, side='right'),
                        N_EXPERTS - 1).astype(jnp.int32)
    nblk_used = (pends[-1] // tmb).astype(jnp.int32).reshape(1)
    e_p = jnp.repeat(blk_e, tmb)
    off = jnp.arange(n_blk * tmb, dtype=jnp.int32) - pstarts[e_p]
    src = order[jnp.clip(starts[e_p] + off, 0, npair - 1)].astype(jnp.int32)
    buf_tok = jnp.where(off < counts[e_p], src // TOP_K, t)
    h_pad = jnp.concatenate([h, jnp.zeros((1, D_MODEL), h.dtype)], axis=0)
    out = _moe(blk_e, nblk_used, h_pad[buf_tok], wg, wu, wd, tmb, tf)
    return x1 + out[dest[:, 0]] * gate[:, 0:1] + out[dest[:, 1]] * gate[:, 1:2]


def _final_kernel(x_ref, g_ref, o_ref):
    o_ref[...] = _rms(x_ref[...], g_ref[...])


def _final_norm(x, g, tm):
    t = x.shape[0]
    return pl.pallas_call(
        _final_kernel,
        grid=(t // tm,),
        in_specs=[pl.BlockSpec((tm, D_MODEL), lambda i: (i, 0)), pl.BlockSpec(g.shape, lambda i: (0, 0))],
        out_specs=pl.BlockSpec((tm, D_MODEL), lambda i: (i, 0)),
        out_shape=jax.ShapeDtypeStruct((t, D_MODEL), F32),
        compiler_params=_cparams(("arbitrary",)),
    )(x, g)


def _rope_tables(pos):
    inv = 1.0 / (ROPE_THETA ** (jnp.arange(0, HEAD_DIM, 2, dtype=F32) / HEAD_DIM))
    ang = pos.astype(F32)[:, None] * inv[None, :]
    cos = jnp.cos(ang)
    sin = jnp.sin(ang)
    return jnp.concatenate([cos, cos, cos, cos], axis=1), jnp.concatenate([-sin, sin, -sin, sin], axis=1)


def _overlap(ncb, nsp, ns):
    c = jnp.arange(ncb, dtype=jnp.int32)[:, None] * CMP_STRIDE
    s = jnp.arange(nsp, dtype=jnp.int32)[None, :] * SLC_BLOCK
    ov = (c <= s + SLC_BLOCK - 1) & (c + CMP_BLOCK - 1 >= s) & (jnp.arange(nsp)[None, :] < ns)
    return ov.astype(BF16)


def _pad_lanes(v, n):
    return jnp.zeros((1, n), F32).at[0, :v.shape[0]].set(v.astype(F32))


def kernel(x_prompt, x_sample, cache_kv, cache_win, state_ssm, state_conv, page_table, ln_mix, w_in, conv_w,
           conv_b, dt_bias, a_log, d_skip, ssd_norm, cmp_pos, cmp_w1, cmp_b1, cmp_w2, w_out, ln_ffn,
           ffn_w_gate, ffn_w_up, ffn_w_down, moe_router, moe_w_gate, moe_w_up, moe_w_down, ln_final):
    depth = w_in.shape[0]
    bsz, seq = x_prompt.shape[:2]
    dbsz, dseq = x_sample.shape[:2]
    n_pages = page_table.shape[1]
    past = n_pages * cache_kv.shape[2]
    win_buf = cache_win.shape[2]
    npool = cache_kv.shape[1]
    assert bsz == 1 and cache_kv.shape[2] == PAGE and seq % PAGE == 0 and seq >= WINDOW + PAGE
    assert win_buf == WINDOW and past % PAGE == 0
    assert not any(past <= CMP_STRIDE * c + CMP_BLOCK - 1 <= past + dseq - 1
                   for c in range(past // CMP_STRIDE - 2, past // CMP_STRIDE + 2))

    ts = dbsz * dseq
    ns_p = seq // SLC_BLOCK
    ns_s = -(-(past + dseq) // SLC_BLOCK)
    nsp_p = -(-ns_p // LANES) * LANES
    nsp_s = -(-ns_s // LANES) * LANES
    assert dseq <= SAMPLE_TQ and n_pages % SAMPLE_PAGES_PER_STEP == 0
    cache_t = jnp.transpose(cache_kv, (0, 1, 3, 4, 5, 2)).reshape(depth, npool, 512, PAGE)
    cwin_t = jnp.transpose(cache_win, (0, 1, 3, 4, 5, 2)).reshape(depth, dbsz, 256, win_buf)

    cs_p, sn_p = _rope_tables(jnp.arange(seq, dtype=jnp.int32))
    cs_s, sn_s = _rope_tables(past + jnp.arange(ts, dtype=jnp.int32) % dseq)
    ov_s = _overlap(past // CMP_STRIDE, nsp_s, ns_s)
    ident = jnp.arange(seq // PAGE, dtype=jnp.int32).reshape(1, -1)

    xp = x_prompt.reshape(seq, D_MODEL)
    xs = x_sample.reshape(ts, D_MODEL)
    outs = {k: [] for k in ('kv_p', 'kv_s', 'win_p', 'win_s', 'ssm_p', 'ssm_s', 'conv_p', 'conv_s')}

    for l in range(depth):
        w = w_in[l]
        wr = jnp.concatenate([w[:, :1536], w[:, 1544:2824], w[:, 1536:1544], w[:, 2824:2848],
                              jnp.zeros((D_MODEL, C_END - 2848), F32)], axis=1).astype(BF16)
        g_mix = ln_mix[l].reshape(1, -1)
        cw8 = jnp.zeros((8, CONV_DIM), F32).at[:CONV_W].set(conv_w[l])
        cb = conv_b[l].reshape(1, -1)
        dtb = _pad_lanes(dt_bias[l], LANES)
        a_neg = _pad_lanes(-jnp.exp(a_log[l].astype(F32)), LANES)
        dsk = jnp.repeat(d_skip[l].astype(F32), SSD_HEAD_DIM).reshape(1, -1)
        ng = ssd_norm[l].reshape(1, -1)
        w1 = cmp_w1[l].reshape(2, 2, CMP_STRIDE, HEAD_DIM, CMP_HIDDEN)
        eye = jnp.eye(KV_HEADS, dtype=F32)
        w1e = jnp.einsum('vspdh,gk->vspgdkh', w1, eye).reshape(2, 2, CMP_STRIDE * LANES, 256).astype(BF16)
        pe = cmp_pos[l].reshape(2, 2, CMP_STRIDE, 1, HEAD_DIM)
        pee = jnp.broadcast_to(pe, (2, 2, CMP_STRIDE, KV_HEADS, HEAD_DIM)).reshape(2, 2, CMP_STRIDE * LANES)
        b1e = jnp.concatenate([cmp_b1[l], cmp_b1[l]], axis=1)
        w2e = jnp.einsum('vhd,gk->vghkd', cmp_w2[l], eye).reshape(2, 256, LANES).astype(BF16)
        wo = w_out[l].astype(BF16)
        g_ffn = ln_ffn[l].reshape(1, -1)
        if l % 2 == 0:
            router = jnp.zeros((D_MODEL, LANES), F32)
        else:
            router = jnp.zeros((D_MODEL, LANES), F32).at[:, :N_EXPERTS].set(moe_router[l // 2])

        z, xbc, dtg, q, qr, kv, win = _front(xp, g_mix, wr, cs_p, sn_p, 512)
        y_ssd, hist, hfin = _ssd(z[None], xbc[None], dtg[None], jnp.zeros((1, 8, CONV_DIM), F32),
                                 jnp.zeros((1, D_SSD, SSD_STATE), F32), cw8, cb, dtb, a_neg, dsk, ng, SSD_CHUNK)
        kv_pages = kv.reshape(seq // PAGE, PAGE, 512)
        kchi, kclo, vct = _compress(kv_pages, ident, w1e, pee, b1e, w2e)
        kaug, vt, kwin, vwt = _kvfmt(kv, win, NSA_TK)
        y_nsa = _nsa(q[None], qr[None], dtg[None], kchi, kclo, vct, kaug, vt, kwin, vwt,
                     tq=128, ns=ns_p, nsp=nsp_p, tk=NSA_TK, wl=WINDOW + 128)
        h_dtype = BF16 if l % 2 == 0 else F32
        x1, h, logits = _oproj(xp, y_ssd[0], y_nsa[0], wo, g_ffn, router, 512, h_dtype)
        outs['kv_p'].append(kv.reshape(1, seq, 4, KV_HEADS, HEAD_DIM))
        outs['win_p'].append(win[seq - WINDOW:].reshape(1, WINDOW, 2, KV_HEADS, HEAD_DIM))
        outs['ssm_p'].append(hfin.reshape(1, SSD_HEADS, SSD_HEAD_DIM, SSD_STATE))
        outs['conv_p'].append(hist[:, 5:8])

        zs, xbcs, dtgs, qs, qrs, kvs, wins = _front(xs, g_mix, wr, cs_s, sn_s, ts)
        padl = lambda v: jnp.pad(v.reshape(dbsz, dseq, -1), ((0, 0), (0, SSD_CHUNK - dseq), (0, 0)))
        hist8 = jnp.pad(state_conv[l], ((0, 0), (8 - (CONV_W - 1), 0), (0, 0)))
        y_ssd_s, hist_s, hfin_s = _ssd(padl(zs), padl(xbcs), padl(dtgs), hist8,
                                       state_ssm[l].reshape(dbsz, D_SSD, SSD_STATE),
                                       cw8, cb, dtb, a_neg, dsk, ng, dseq)
        tail_t = jnp.pad(jnp.transpose(kvs.reshape(dbsz, dseq, 512), (0, 2, 1)), ((0, 0), (0, 0), (0, PAGE - dseq)))
        wtail_t = jnp.pad(jnp.transpose(wins.reshape(dbsz, dseq, 256), (0, 2, 1)), ((0, 0), (0, 0), (0, PAGE - dseq)))
        padq = lambda v: jnp.pad(v.reshape(dbsz, dseq, -1), ((0, 0), (0, SAMPLE_TQ - dseq), (0, 0)))
        y_nsa_s = _nsa_sample(l, cache_t, page_table, tail_t, padq(qs), padq(qrs), padq(dtgs), w1e, pee, b1e, w2e,
                              ov_s, cwin_t, wtail_t, past=past, ns=ns_s)
        y_nsa_s = y_nsa_s[:, :dseq].reshape(ts, D_NSA)
        x1s, hs, logits_s = _oproj(xs, y_ssd_s[:, :dseq].reshape(ts, D_SSD), y_nsa_s, wo, g_ffn, router, ts, h_dtype)
        outs['kv_s'].append(kvs.reshape(dbsz, dseq, 4, KV_HEADS, HEAD_DIM))
        win_all = jnp.concatenate([cache_win[l], wins.reshape(dbsz, dseq, 2, KV_HEADS, HEAD_DIM)], axis=1)
        outs['win_s'].append(win_all[:, dseq:])
        outs['ssm_s'].append(hfin_s.reshape(dbsz, SSD_HEADS, SSD_HEAD_DIM, SSD_STATE))
        outs['conv_s'].append(hist_s[:, 5:8])

        if l % 2 == 0:
            wg = ffn_w_gate[l // 2].astype(BF16)
            wu = ffn_w_up[l // 2].astype(BF16)
            wd = ffn_w_down[l // 2].astype(BF16)
            xp = _ffn(h, x1, wg, wu, wd, 1024, 256)
            xs = _ffn(hs, x1s, wg, wu, wd, ts, 256)
        else:
            wg = moe_w_gate[l // 2]
            wu = moe_w_up[l // 2]
            wd = moe_w_down[l // 2]
            xp = _moe_layer(h, x1, logits, wg, wu, wd, 1024, 512)
            xs = _moe_layer(hs, x1s, logits_s, wg, wu, wd, 128, 512)

    g_fin = ln_final.reshape(1, -1)
    y_prompt = _final_norm(xp, g_fin, 1024).reshape(bsz, seq, D_MODEL)
    y_sample = _final_norm(xs, g_fin, ts).reshape(dbsz, dseq, D_MODEL)
    st = lambda k: jnp.stack(outs[k])
    return (y_prompt, y_sample, st('kv_p'), st('kv_s'), st('win_p'), st('win_s'),
            st('ssm_p'), st('ssm_s'), st('conv_p'), st('conv_s'))
```

```python
import functools

import jax
import jax.numpy as jnp
from jax import lax
from jax.experimental import pallas as pl
from jax.experimental.pallas import tpu as pltpu

F32 = jnp.float32
BF16 = jnp.bfloat16

D_MODEL = 1024
SSD_HEADS = 8
SSD_HEAD_DIM = 64
D_SSD = 512
SSD_GROUPS = 2
SSD_STATE = 128
CONV_W = 4
CONV_DIM = 1024
SSD_CHUNK = 128
NSA_HEADS = 8
HEAD_DIM = 64
KV_HEADS = 2
D_NSA = 512
CMP_BLOCK = 32
CMP_STRIDE = 16
CMP_HIDDEN = 128
SLC_BLOCK = 64
TOP_N = 16
WINDOW = 512
ROPE_THETA = 10000.0
FORCE_BONUS = 1.0e4
N_EXPERTS = 8
TOP_K = 2
EPS = 1e-6

PAGE = 128
LANES = 128
LOG2E = 1.4426950408889634
NEG = -1.0e30
SLAB_BLOCKS = 128
VMEM_LIMIT = 56 * 1024 * 1024

C_Z, C_XBC, C_Q, C_KV, C_DT, C_END = 0, 512, 1536, 2048, 2816, 2944


def _cparams(sem):
    return pltpu.CompilerParams(dimension_semantics=sem, vmem_limit_bytes=VMEM_LIMIT)


def _dot(a, b):
    return jnp.dot(a, b, preferred_element_type=F32)


def _dot_nt(a, b):
    return lax.dot_general(a, b, (((1,), (1,)), ((), ())), preferred_element_type=F32)


def _split3(x):
    h1 = x.astype(BF16)
    r1 = x - h1.astype(F32)
    h2 = r1.astype(BF16)
    h3 = (r1 - h2.astype(F32)).astype(BF16)
    return h1, h2, h3


def _dot_nt_hi(a, b):
    a1, a2, _ = _split3(a)
    b1, b2, _ = _split3(b)
    return _dot_nt(a1, b1) + (_dot_nt(a1, b2) + _dot_nt(a2, b1))


def _dot_hi(a, b):
    a1, a2, _ = _split3(a)
    b1, b2, _ = _split3(b)
    return _dot(a1, b1) + (_dot(a1, b2) + _dot(a2, b1))


def _silu(x):
    return x * jax.nn.sigmoid(x)


def _rms(x, g):
    return x * lax.rsqrt(jnp.mean(x * x, axis=-1, keepdims=True) + EPS) * g


def _front_kernel(x_ref, g_ref, w_ref, cs_ref, sn_ref,
                  z_ref, xbc_ref, dtg_ref, q_ref, qr_ref, kv_ref, win_ref):
    xn = _rms(x_ref[...], g_ref[...])
    p = _dot(xn.astype(BF16), w_ref[...])
    cs = cs_ref[...]
    sn = sn_ref[...]
    lane = lax.broadcasted_iota(jnp.int32, cs.shape, 1)
    first = (lane % HEAD_DIM) < (HEAD_DIM // 2)

    def rope(v):
        sw = jnp.where(first, pltpu.roll(v, LANES - HEAD_DIM // 2, 1), pltpu.roll(v, HEAD_DIM // 2, 1))
        return v * cs + sw * sn

    z_ref[...] = p[:, C_Z:C_XBC]
    xbc_ref[...] = p[:, C_XBC:C_Q]
    dtg_ref[...] = p[:, C_DT:C_END]
    q_ref[...] = p[:, C_Q:C_KV]
    for s in range(4):
        qr_ref[:, s * LANES:(s + 1) * LANES] = rope(p[:, C_Q + s * LANES:C_Q + (s + 1) * LANES])
    kv_ref[:, 0:256] = p[:, C_KV:C_KV + 256]
    kv_ref[:, 256:384] = rope(p[:, C_KV + 256:C_KV + 384])
    kv_ref[:, 384:512] = p[:, C_KV + 384:C_KV + 512]
    win_ref[:, 0:128] = rope(p[:, C_KV + 512:C_KV + 640])
    win_ref[:, 128:256] = p[:, C_KV + 640:C_KV + 768]


def _front(x, g, w, cs, sn, tm):
    t = x.shape[0]
    row = lambda n: pl.BlockSpec((tm, n), lambda i: (i, 0))
    full = lambda a: pl.BlockSpec(a.shape, lambda i: (0,) * a.ndim)
    outs = [512, 1024, 128, 512, 512, 512, 256]
    return pl.pallas_call(
        _front_kernel,
        grid=(t // tm,),
        in_specs=[row(D_MODEL), full(g), full(w), row(LANES), row(LANES)],
        out_specs=[row(n) for n in outs],
        out_shape=[jax.ShapeDtypeStruct((t, n), F32) for n in outs],
        compiler_params=_cparams(("arbitrary",)),
    )(x, g, w, cs, sn)


def _expand_heads(v):
    r = v.shape[0]
    lane = lax.broadcasted_iota(jnp.int32, (r, LANES), 1)
    outs = []
    for k in range(4):
        a = jnp.broadcast_to(v[:, 2 * k:2 * k + 1], (r, LANES))
        b = jnp.broadcast_to(v[:, 2 * k + 1:2 * k + 2], (r, LANES))
        outs.append(jnp.where(lane < SSD_HEAD_DIM, a, b))
    return jnp.concatenate(outs, axis=1)


def _ssd_kernel(z_ref, xbc_ref, dtg_ref, hist_ref, h0_ref, cw_ref, cb_ref, dtb_ref, a_ref, dsk_ref, ng_ref,
                y_ref, hist_o_ref, hfin_ref, xpad, ht, *, nchunks, valid_last):
    c = pl.program_id(1)
    q = SSD_CHUNK

    @pl.when(c == 0)
    def _():
        xpad[0:8, :] = hist_ref[0]
        ht[...] = h0_ref[0].T

    xpad[8:8 + q, :] = xbc_ref[0]
    conv = cb_ref[...]
    for k in range(CONV_W):
        conv = conv + xpad[5 + k:5 + k + q, :] * cw_ref[k:k + 1, :]
    xc = _silu(conv)
    xs = xc[:, 0:D_SSD]
    bm = xc[:, D_SSD:D_SSD + 256]
    cm = xc[:, D_SSD + 256:D_SSD + 512]

    row = lax.broadcasted_iota(jnp.int32, (q, q), 0)
    col = lax.broadcasted_iota(jnp.int32, (q, q), 1)
    nvalid = jnp.where(c == nchunks - 1, valid_last, q)
    xdt_in = dtg_ref[0] + dtb_ref[...]
    dt = jnp.maximum(xdt_in, 0.0) + jnp.log1p(jnp.exp(-jnp.abs(xdt_in)))
    dt = jnp.where(row < nvalid, dt, 0.0)
    dta = dt * a_ref[...]
    causal = row >= col
    cum = jnp.dot(causal.astype(F32), dta, preferred_element_type=F32, precision=lax.Precision.HIGHEST)
    cum_t = cum.T
    ecum = jnp.exp(cum)
    toend = jnp.exp(cum[q - 1:q, :] - cum)
    dt_e = _expand_heads(dt)
    ecum_e = _expand_heads(ecum)
    toend_e = _expand_heads(toend)
    xdt = xs * dt_e
    xdt_b = xdt.astype(BF16)
    xw_b = (xdt * toend_e).astype(BF16)
    lane = lax.broadcasted_iota(jnp.int32, (q, LANES), 1)

    y_intra = []
    y_inter = []
    for g in range(SSD_GROUPS):
        bg = bm[:, g * 128:(g + 1) * 128]
        cg_b = cm[:, g * 128:(g + 1) * 128].astype(BF16)
        bg_b = bg.astype(BF16)
        cb = _dot_nt(cg_b, bg_b)
        htg = ht[:, g * 256:(g + 1) * 256]
        y_inter.append(_dot(cg_b, htg.astype(BF16)))
        for kk in range(2):
            slab = xdt_b[:, (2 * g + kk) * LANES:(2 * g + kk + 1) * LANES]
            res = []
            for hh in range(2):
                h = 4 * g + 2 * kk + hh
                seg = cum[:, h:h + 1] - cum_t[h:h + 1, :]
                decay = jnp.exp(jnp.where(causal, seg, NEG))
                res.append(_dot((decay * cb).astype(BF16), slab))
            y_intra.append(jnp.where(lane < SSD_HEAD_DIM, res[0], res[1]))
        s_new = _dot(bg.T.astype(BF16), xw_b[:, g * 256:(g + 1) * 256])
        ht[:, g * 256:(g + 1) * 256] = htg * ecum_e[q - 1:q, g * 256:(g + 1) * 256] + s_new

    y = jnp.concatenate(y_intra, axis=1) + jnp.concatenate(y_inter, axis=1) * ecum_e + dsk_ref[...] * xs
    y = y * _silu(z_ref[0])
    y_ref[0] = _rms(y, ng_ref[...])

    xpad[0:8, :] = xpad[q:q + 8, :]

    @pl.when(c == nchunks - 1)
    def _():
        hist_o_ref[0] = xpad[valid_last:valid_last + 8, :]
        hfin_ref[0] = ht[...].T


def _ssd(z, xbc, dtg, hist8, h0, cw8, cb, dtb, a, dsk, ng, valid_last):
    b, l = z.shape[:2]
    nchunks = l // SSD_CHUNK
    rows = lambda n: pl.BlockSpec((1, SSD_CHUNK, n), lambda i, c: (i, c, 0))
    perb = lambda r, n: pl.BlockSpec((1, r, n), lambda i, c: (i, 0, 0))
    full = lambda arr: pl.BlockSpec(arr.shape, lambda i, c: (0,) * arr.ndim)
    kern = functools.partial(_ssd_kernel, nchunks=nchunks, valid_last=valid_last)
    return pl.pallas_call(
        kern,
        grid=(b, nchunks),
        in_specs=[rows(512), rows(1024), rows(128), perb(8, 1024), perb(512, 128),
                  full(cw8), full(cb), full(dtb), full(a), full(dsk), full(ng)],
        out_specs=[rows(512), perb(8, 1024), perb(512, 128)],
        out_shape=[jax.ShapeDtypeStruct((b, l, 512), F32),
                   jax.ShapeDtypeStruct((b, 8, 1024), F32),
                   jax.ShapeDtypeStruct((b, 512, 128), F32)],
        scratch_shapes=[pltpu.VMEM((SSD_CHUNK + 8, CONV_DIM), F32), pltpu.VMEM((SSD_STATE, D_SSD), F32)],
        compiler_params=_cparams(("arbitrary", "arbitrary")),
    )(z, xbc, dtg, hist8, h0, cw8, cb, dtb, a, dsk, ng)


CMP_CHUNK = 256
SUM_ROWS = 16
NSA_TK = 512
CMP_GROUP_PAGES = 16


def _compress_kernel(pt_ref, pgk_ref, pgv_ref, w1_ref, pe_ref, b1_ref, w2_ref, w2t_ref, kchi_ref, kclo_ref, vct_ref,
                     xs, ps, *, npages):
    p = pl.program_id(1)
    sub = p % CMP_GROUP_PAGES
    hb = PAGE // CMP_STRIDE
    r0 = pl.multiple_of(sub * hb, hb)
    for kv, pg_ref in enumerate((pgk_ref, pgv_ref)):
        for pos in range(CMP_STRIDE):
            xs[kv, pl.ds(r0, hb), pos * LANES:(pos + 1) * LANES] = pg_ref[0, pl.ds(pos, hb, stride=CMP_STRIDE), :]

    @pl.when(sub == CMP_GROUP_PAGES - 1)
    def _():
        rows = CMP_GROUP_PAGES * hb
        g0 = pl.multiple_of((p // CMP_GROUP_PAGES) * rows, rows)
        for kv in range(2):
            x = xs[kv]
            for s in range(2):
                ps[kv, s, pl.ds(g0, rows), :] = _dot((x + pe_ref[kv, s:s + 1, :]).astype(BF16), w1_ref[kv, s])

    @pl.when(p == npages - 1)
    def _():
        nh = npages * hb
        act = [jax.nn.gelu(b1_ref[kv:kv + 1, :] + ps[kv, 0] + pltpu.roll(ps[kv, 1], nh - 1, 0)).astype(BF16)
               for kv in range(2)]
        kc = _dot(act[0], w2_ref[0])
        hi = kc.astype(BF16)
        kchi_ref[0] = hi
        kclo_ref[0] = (kc - hi.astype(F32)).astype(BF16)
        vct_ref[0] = _dot_nt(w2t_ref[...], act[1]).astype(BF16)


def _compress(pages, table, w1e, pee, b1e, w2e):
    b, npages = table.shape
    w2t = w2e[1].T
    nh = npages * (PAGE // CMP_STRIDE)
    kern = functools.partial(_compress_kernel, npages=npages)
    full = lambda arr: pl.BlockSpec(arr.shape, lambda i, p, pt: (0,) * arr.ndim)
    return pl.pallas_call(
        kern,
        grid_spec=pltpu.PrefetchScalarGridSpec(
            num_scalar_prefetch=1,
            grid=(b, npages),
            in_specs=[pl.BlockSpec((1, PAGE, LANES), lambda i, p, pt: (pt[i, p], 0, 0)),
                      pl.BlockSpec((1, PAGE, LANES), lambda i, p, pt: (pt[i, p], 0, 1)),
                      full(w1e), full(pee), full(b1e), full(w2e), full(w2t)],
            out_specs=[pl.BlockSpec((1, nh, LANES), lambda i, p, pt: (i, 0, 0)),
                       pl.BlockSpec((1, nh, LANES), lambda i, p, pt: (i, 0, 0)),
                       pl.BlockSpec((1, LANES, nh), lambda i, p, pt: (i, 0, 0))],
            scratch_shapes=[pltpu.VMEM((2, CMP_GROUP_PAGES * 8, CMP_STRIDE * LANES), F32),
                            pltpu.VMEM((2, 2, nh, 256), F32)]),
        out_shape=[jax.ShapeDtypeStruct((b, nh, LANES), BF16), jax.ShapeDtypeStruct((b, nh, LANES), BF16),
                   jax.ShapeDtypeStruct((b, LANES, nh), BF16)],
        compiler_params=_cparams(("arbitrary", "arbitrary")),
    )(table, pages, pages, w1e, pee, b1e, w2e, w2t)


def _kvfmt_kernel(kv_ref, win_ref, k_ref, vt_ref, kw_ref, vwt_ref):
    p = pl.program_id(0)
    src = kv_ref[...]
    r = lax.broadcasted_iota(jnp.int32, (PAGE, LANES), 0)
    lane = lax.broadcasted_iota(jnp.int32, (PAGE, LANES), 1)
    blk = (2 * p + r // SLC_BLOCK) % SLAB_BLOCKS
    k_ref[:, 0:128] = src[:, 0:128].astype(BF16)
    k_ref[:, 128:256] = jnp.where(lane == blk, 1.0, 0.0).astype(BF16)
    v_t = src[:, 128:256].T.astype(BF16)
    for g in range(KV_HEADS):
        vt_ref[0, g, 0:HEAD_DIM, :] = v_t[g * HEAD_DIM:(g + 1) * HEAD_DIM, :]
        vt_ref[0, g, HEAD_DIM:HEAD_DIM + SUM_ROWS, :] = jnp.ones((SUM_ROWS, PAGE), BF16)
    kw_ref[...] = win_ref[:, 0:128].astype(BF16)
    vwt_ref[0] = win_ref[:, 128:256].T.astype(BF16)


def _kvfmt(kv, win, tk):
    l = kv.shape[0]
    per = tk // PAGE
    return pl.pallas_call(
        _kvfmt_kernel,
        grid=(l // PAGE,),
        in_specs=[pl.BlockSpec((PAGE, 256), lambda p: (p, 1)), pl.BlockSpec((PAGE, 256), lambda p: (p, 0))],
        out_specs=[pl.BlockSpec((PAGE, 256), lambda p: (p, 0)),
                   pl.BlockSpec((1, KV_HEADS, HEAD_DIM + SUM_ROWS, PAGE), lambda p: (p // per, 0, 0, p % per)),
                   pl.BlockSpec((PAGE, LANES), lambda p: (p, 0)),
                   pl.BlockSpec((1, LANES, PAGE), lambda p: (p, 0, 0))],
        out_shape=[jax.ShapeDtypeStruct((l, 256), BF16),
                   jax.ShapeDtypeStruct((l // tk, KV_HEADS, HEAD_DIM + SUM_ROWS, tk), BF16),
                   jax.ShapeDtypeStruct((l, LANES), BF16),
                   jax.ShapeDtypeStruct((l // PAGE, LANES, PAGE), BF16)],
        compiler_params=_cparams(("arbitrary",)),
    )(kv, win)


def _group_rows(qfull, g, tq):
    lane = lax.broadcasted_iota(jnp.int32, (tq, LANES), 1)
    keep = (lane >= HEAD_DIM) if g else (lane < HEAD_DIM)
    pieces = []
    for r in range(4):
        h = 4 * g + r
        slab = qfull[:, (h // 2) * LANES:(h // 2 + 1) * LANES]
        if h % 2 != g:
            slab = pltpu.roll(slab, HEAD_DIM, 1)
        pieces.append(jnp.where(keep, slab, 0.0))
    return jnp.concatenate(pieces, axis=0)


def _masked_softmax(s, mask):
    s = jnp.where(mask, s, NEG)
    m = jnp.max(s, axis=-1, keepdims=True)
    e = jnp.where(mask, jnp.exp(s - m), 0.0)
    return e / jnp.maximum(jnp.sum(e, axis=-1, keepdims=True), 1e-30)


def _importance(p4, ov):
    h1, h2, h3 = _split3(p4)
    return _dot(h1, ov) + (_dot(h2, ov) + _dot(h3, ov))


def _topk_bias(imp, qpos, ns):
    tq, nsp = imp.shape
    jf = lax.broadcasted_iota(jnp.int32, (tq, nsp), 1)
    jff = jf.astype(F32)
    cur = qpos >> 6
    valid = jf <= cur
    forced = valid & ((jf == 0) | (jf == cur) | (jf == cur - 1))
    score = jnp.where(forced, FORCE_BONUS, jnp.where(valid, imp, -1.0))
    score = jnp.where(jf < ns, score, -2.0)

    def pick(_, carry):
        sc, sb = carry
        mx = jnp.max(sc, axis=-1, keepdims=True)
        jm = jnp.min(jnp.where(sc == mx, jff, float(nsp)), axis=-1, keepdims=True)
        hit = jff == jm
        return jnp.where(hit, -3.0, sc), jnp.where(hit, 0.0, sb)

    return lax.fori_loop(0, TOP_N, pick, (score, jnp.full((tq, nsp), NEG, F32)))[1]


def _gate_cols(sg, g, k, tq):
    return jnp.concatenate([sg[:, 8 + 3 * (4 * g + r) + k:8 + 3 * (4 * g + r) + k + 1] for r in range(4)], axis=0)


def _emit_heads(outs, tq, y_ref):
    lane = lax.broadcasted_iota(jnp.int32, (tq, LANES), 1)
    for k in range(4):
        g = k // 2
        r0 = (2 * k) % 4
        a = outs[g][r0 * tq:(r0 + 1) * tq]
        b = outs[g][(r0 + 1) * tq:(r0 + 2) * tq]
        if g == 0:
            slab = jnp.where(lane < HEAD_DIM, a, pltpu.roll(b, HEAD_DIM, 1))
        else:
            slab = jnp.where(lane < HEAD_DIM, pltpu.roll(a, HEAD_DIM, 1), b)
        y_ref[0, :, k * LANES:(k + 1) * LANES] = slab


def _masked_softmax_t(s, mask):
    s = jnp.where(mask, s, NEG)
    m = jnp.max(s, axis=0, keepdims=True)
    e = jnp.exp(s - m)
    inv = jnp.where(m > 0.5 * NEG, 1.0 / jnp.maximum(jnp.sum(e, axis=0, keepdims=True), 1e-30), 0.0)
    return e * inv


def _topk_bias_t(imp, qpos, ns):
    nsp, tq = imp.shape
    jf = lax.broadcasted_iota(jnp.int32, (nsp, tq), 0)
    jff = jf.astype(F32)
    cur = qpos >> 6
    valid = jf <= cur
    forced = valid & ((jf == 0) | (jf == cur) | (jf == cur - 1))
    score = jnp.where(forced, FORCE_BONUS, jnp.where(valid, imp, -1.0))
    score = jnp.where(jf < ns, score, -2.0)

    def pick(_, carry):
        sc, sb = carry
        mx = jnp.max(sc, axis=0, keepdims=True)
        jm = jnp.min(jnp.where(sc == mx, jff, float(nsp)), axis=0, keepdims=True)
        hit = jff == jm
        return jnp.where(hit, -3.0, sc), jnp.where(hit, 0.0, sb)

    return lax.fori_loop(0, TOP_N, pick, (score, jnp.full((nsp, tq), NEG, F32)))[1]


def _nsa_kernel(qn_ref, qr_ref, dtg_ref, kchi_ref, kclo_ref, vct_ref, kaug_ref, vt_ref, kwin_ref, vwt_ref,
                y_ref, qat, p4_sc, cs_sc, sa_sc, sb_sc, m_sc, acc_sc, *, tq, ns, nsp, tk, wl):
    t = pl.program_id(1)
    scale = HEAD_DIM ** -0.5
    r4 = 4 * tq
    nslab = nsp // LANES
    lane4 = lax.broadcasted_iota(jnp.int32, (1, r4), 1)
    posr = t * tq + (lane4 & (tq - 1))
    qpos = t * tq + lax.broadcasted_iota(jnp.int32, (1, tq), 1)

    qn = qn_ref[0] * scale
    qr = qr_ref[0] * scale
    ncb = kchi_ref.shape[1]
    cch = cs_sc.shape[1]
    cmax = (t * tq + tq - CMP_BLOCK) // CMP_STRIDE
    nch = jnp.minimum(cmax // cch + 1, ncb // cch)

    ws = pl.multiple_of(jnp.maximum(t * tq - WINDOW, 0), LANES)
    kw = kwin_ref[pl.ds(ws, wl), :]
    kposw = ws + lax.broadcasted_iota(jnp.int32, (wl, 1), 0)
    dpos = posr - kposw
    mask_w = (dpos >= 0) & (dpos < WINDOW)
    wpage = ws // PAGE

    p4_sc[0:8, :] = jnp.zeros((8, tq), F32)
    p4_sc[8 + ncb:, :] = jnp.zeros((p4_sc.shape[0] - 8 - ncb, tq), F32)
    o_cmp = []
    o_win = []
    imps = []
    for g in range(KV_HEADS):
        qn_t = _group_rows(qn, g, tq).T
        qr_t = _group_rows(qr, g, tq).T
        qh = qn_t.astype(BF16)
        ql = (qn_t - qh.astype(F32)).astype(BF16)

        def cmp_scores(k, m):
            r0 = pl.multiple_of(k * cch, cch)
            kh = kchi_ref[0, pl.ds(r0, cch), :]
            kl = kclo_ref[0, pl.ds(r0, cch), :]
            s = _dot(kh, qh) + (_dot(kh, ql) + _dot(kl, qh))
            c_end = (r0 + lax.broadcasted_iota(jnp.int32, (cch, 1), 0)) * CMP_STRIDE + (CMP_BLOCK - 1)
            s = jnp.where(c_end <= posr, s, NEG)
            cs_sc[k] = s
            return jnp.maximum(m, jnp.max(s, axis=0, keepdims=True))

        m_c = lax.fori_loop(0, nch, cmp_scores, jnp.full((1, r4), NEG, F32))

        def cmp_exp(k, l):
            e = jnp.exp(cs_sc[k] - m_c)
            cs_sc[k] = e
            return l + jnp.sum(e, axis=0, keepdims=True)

        l_c = lax.fori_loop(0, nch, cmp_exp, jnp.zeros((1, r4), F32))
        inv_c = jnp.where(m_c > 0.5 * NEG, 1.0 / jnp.maximum(l_c, 1e-30), 0.0)

        def cmp_out(k, o):
            r0 = pl.multiple_of(k * cch, cch)
            p = cs_sc[k] * inv_c
            p4_sc[pl.ds(8 + r0, cch), :] = p[:, 0:tq] + p[:, tq:2 * tq] + p[:, 2 * tq:3 * tq] + p[:, 3 * tq:4 * tq]
            return o + _dot(vct_ref[k], p.astype(BF16))

        o_cmp.append(lax.fori_loop(0, nch, cmp_out, jnp.zeros((LANES, r4), F32)))

        def cmp_zero(k, carry):
            r0 = pl.multiple_of(k * cch, cch)
            p4_sc[pl.ds(8 + r0, cch), :] = jnp.zeros((cch, tq), F32)
            return carry

        lax.fori_loop(nch, ncb // cch, cmp_zero, 0)
        per = SLC_BLOCK // CMP_STRIDE
        imp = p4_sc[pl.ds(8 - 1, nsp, stride=per), :]
        for k in range(per):
            imp = imp + p4_sc[pl.ds(8 + k, nsp, stride=per), :]
        imps.append(imp)
        qs_b = (qr_t * LOG2E).astype(BF16)
        for sl in range(nslab):
            qat[sl, g, 0:LANES, :] = qs_b
        p_w = _masked_softmax_t(_dot(kw, qr_t.astype(BF16)), mask_w).astype(BF16)
        o_w = _dot(vwt_ref[wpage], p_w[0:PAGE])
        for i in range(1, wl // PAGE):
            o_w = o_w + _dot(vwt_ref[wpage + i], p_w[i * PAGE:(i + 1) * PAGE])
        o_win.append(o_w)

    def select(rows):
        for g in range(KV_HEADS):
            selb = _topk_bias_t(imps[g][0:rows], qpos, min(ns, rows))
            if rows < nsp:
                selb = jnp.concatenate([selb, jnp.full((nsp - rows, tq), NEG, F32)], axis=0)
            for sl in range(nslab):
                piece = selb[sl * LANES:(sl + 1) * LANES, :]
                qat[sl, g, LANES:2 * LANES, :] = jnp.concatenate([piece] * 4, axis=1).astype(BF16)

    half = nsp // 2
    if half % 8 == 0 and half >= TOP_N:
        last_block = (t * tq + tq - 1) // SLC_BLOCK

        @pl.when(last_block < half)
        def _():
            select(half)

        @pl.when(last_block >= half)
        def _():
            select(nsp)
    else:
        select(nsp)

    m_sc[...] = jnp.full(m_sc.shape, 2.0 * NEG, F32)
    acc_sc[...] = jnp.zeros(acc_sc.shape, F32)
    ndiag = (t * tq + tq - 1) // tk

    def scores(jt, s_buf):
        k0 = pl.multiple_of(jt * tk, tk)
        kt = kaug_ref[pl.ds(k0, tk), :]
        sb = k0 // (SLAB_BLOCKS * SLC_BLOCK)
        for g in range(KV_HEADS):
            s_buf[g] = _dot(kt, qat[sb, g])

    def accumulate(jt, s_buf, causal):
        for g in range(KV_HEADS):
            s = s_buf[g]
            if causal:
                kpos = jt * tk + lax.broadcasted_iota(jnp.int32, (tk, 1), 0)
                s = jnp.where(kpos <= posr, s, NEG)
            m_old = m_sc[g]
            m_new = jnp.maximum(m_old, jnp.max(s, axis=0, keepdims=True))
            pe = jnp.exp2(s - m_new).astype(BF16)
            acc_sc[g] = jnp.exp2(m_old - m_new) * acc_sc[g] + _dot(vt_ref[jt, g], pe)
            m_sc[g] = m_new

    scores(0, sa_sc)

    def tile_pair(p, carry):
        jt = 2 * p
        scores(jt + 1, sb_sc)
        accumulate(jt, sa_sc, False)
        scores(jt + 2, sa_sc)
        accumulate(jt + 1, sb_sc, False)
        return carry

    npair = ndiag // 2
    lax.fori_loop(0, npair, tile_pair, 0)

    @pl.when(ndiag % 2 == 0)
    def _():
        accumulate(ndiag, sa_sc, True)

    @pl.when(ndiag % 2 == 1)
    def _():
        scores(ndiag, sb_sc)
        accumulate(ndiag - 1, sa_sc, False)
        accumulate(ndiag, sb_sc, True)

    sg_t = jax.nn.sigmoid(dtg_ref[0]).T
    outs = []
    for g in range(KV_HEADS):
        gate = [jnp.concatenate([sg_t[8 + 3 * (4 * g + r) + k:8 + 3 * (4 * g + r) + k + 1, :] for r in range(4)], axis=1)
                for k in range(3)]
        o_sel = acc_sc[g, 0:HEAD_DIM, :] / acc_sc[g, HEAD_DIM:HEAD_DIM + 1, :]
        zero = jnp.zeros((HEAD_DIM, r4), F32)
        o_sel = jnp.concatenate([o_sel, zero] if g == 0 else [zero, o_sel], axis=0)
        o_t = gate[0] * o_cmp[g] + gate[1] * o_sel + gate[2] * o_win[g]
        outs.append(o_t.T)
    _emit_heads(outs, tq, y_ref)


def _nsa(qn, qr, dtg, kchi, kclo, vct, kaug, vt, kwin, vwt, *, tq, ns, nsp, tk, wl):
    b, lq = qn.shape[:2]
    ncb = kchi.shape[1]
    cch = min(CMP_CHUNK, ncb)
    assert nsp * (SLC_BLOCK // CMP_STRIDE) >= ncb and ncb % cch == 0
    vct = vct[0].reshape(LANES, ncb // cch, cch).transpose(1, 0, 2)
    nslab = nsp // LANES
    kern = functools.partial(_nsa_kernel, tq=tq, ns=ns, nsp=nsp, tk=tk, wl=wl)
    rows = lambda n: pl.BlockSpec((1, tq, n), lambda i, t: (i, t, 0))
    whole = lambda arr: pl.BlockSpec(arr.shape, lambda i, t: (0,) * arr.ndim, pipeline_mode=pl.Buffered(1))
    return pl.pallas_call(
        kern,
        grid=(b, lq // tq),
        in_specs=[rows(512), rows(512), rows(128), whole(kchi), whole(kclo), whole(vct),
                  whole(kaug), whole(vt), whole(kwin), whole(vwt)],
        out_specs=rows(512),
        out_shape=jax.ShapeDtypeStruct((b, lq, 512), F32),
        scratch_shapes=[pltpu.VMEM((nslab, KV_HEADS, 2 * LANES, 4 * tq), BF16),
                        pltpu.VMEM((nsp * (SLC_BLOCK // CMP_STRIDE) + 16, tq), F32),
                        pltpu.VMEM((ncb // cch, cch, 4 * tq), F32),
                        pltpu.VMEM((KV_HEADS, tk, 4 * tq), F32),
                        pltpu.VMEM((KV_HEADS, tk, 4 * tq), F32),
                        pltpu.VMEM((KV_HEADS, 1, 4 * tq), F32),
                        pltpu.VMEM((KV_HEADS, HEAD_DIM + SUM_ROWS, 4 * tq), F32)],
        compiler_params=_cparams(("arbitrary", "arbitrary")),
    )(qn, qr, dtg, kchi, kclo, vct, kaug, vt, kwin, vwt)


SAMPLE_PAGES_PER_STEP = 16
SAMPLE_TQ = 8


def _nsa_sample_kernel(pt_ref, *refs, npages, past, ns, nsp):
    pps = SAMPLE_PAGES_PER_STEP
    tq = SAMPLE_TQ
    pages = refs[:pps]
    (tail_ref, qn_ref, qr_ref, dtg_ref, w1_ref, pe_ref, b1_ref, w2_ref, ov_ref, cwin_ref, wtail_ref,
     y_ref, tsc, xs, ps, s_all, vt_all, bias_sc) = refs[pps:]
    j = pl.program_id(1)
    ngroups = npages // pps
    hb = PAGE // CMP_STRIDE
    scale = HEAD_DIM ** -0.5
    qr_aug = jnp.concatenate([_group_rows(qr_ref[0] * scale, g, tq) for g in range(KV_HEADS)], axis=0).astype(BF16)

    for i, pg in enumerate(pages):
        for kv in range(2):
            tsc[i, kv] = pg[0, 0, kv * LANES:(kv + 1) * LANES, :].T
            for pos in range(CMP_STRIDE):
                xs[kv, i * hb:(i + 1) * hb, pos * LANES:(pos + 1) * LANES] = tsc[i, kv, pl.ds(pos, hb, stride=CMP_STRIDE), :]
        page_idx = j * pps + i
        s_all[page_idx] = _dot(qr_aug, pg[0, 0, 256:384, :].astype(BF16))
        vt_all[page_idx] = pg[0, 0, 384:512, :].astype(BF16)
    rows = pps * hb
    g0 = pl.multiple_of(j * rows, rows)
    for kv in range(2):
        x = xs[kv]
        for s in range(2):
            ps[kv, s, pl.ds(g0, rows), :] = _dot((x + pe_ref[kv, s:s + 1, :]).astype(BF16), w1_ref[kv, s])

    @pl.when(j == ngroups - 1)
    def _():
        nh = npages * hb
        kcvc = []
        for kv in range(2):
            hid = b1_ref[kv:kv + 1, :] + ps[kv, 0] + pltpu.roll(ps[kv, 1], nh - 1, 0)
            kcvc.append(_dot(jax.nn.gelu(hid).astype(BF16), w2_ref[kv]))
        kc, vc = kcvc
        qpos = past + lax.broadcasted_iota(jnp.int32, (tq, 1), 0)
        pos_r = jnp.concatenate([qpos] * (4 * KV_HEADS), axis=0)
        nrow = 4 * KV_HEADS * tq
        s_all[npages] = _dot(qr_aug, tail_ref[0, 256:384, :].astype(BF16))
        vt_all[npages] = tail_ref[0, 384:512, :].astype(BF16)

        qn_aug = jnp.concatenate([_group_rows(qn_ref[0] * scale, g, tq) for g in range(KV_HEADS)], axis=0)
        c_end = lax.broadcasted_iota(jnp.int32, (1, nh), 1) * CMP_STRIDE + (CMP_BLOCK - 1)
        p_c = _masked_softmax(_dot_nt_hi(qn_aug, kc), c_end <= pos_r)
        o_c = _dot(p_c.astype(BF16), vc.astype(BF16))
        imps = []
        for g in range(KV_HEADS):
            base = 4 * tq * g
            p4 = (p_c[base:base + tq] + p_c[base + tq:base + 2 * tq]
                  + p_c[base + 2 * tq:base + 3 * tq] + p_c[base + 3 * tq:base + 4 * tq])
            imps.append(_importance(p4, ov_ref[...]))
        selb = _topk_bias(jnp.concatenate(imps, axis=0), jnp.concatenate([qpos] * KV_HEADS, axis=0), ns)
        lane_b = lax.broadcasted_iota(jnp.int32, (KV_HEADS * tq, LANES), 1)
        per_page = PAGE // SLC_BLOCK
        assert per_page == 2 and nsp >= per_page * (npages + 1)
        for pp in range(npages + 1):
            lo = jnp.broadcast_to(selb[:, 2 * pp:2 * pp + 1], (KV_HEADS * tq, LANES))
            hi = jnp.broadcast_to(selb[:, 2 * pp + 1:2 * pp + 2], (KV_HEADS * tq, LANES))
            bias_sc[pp] = jnp.where(lane_b < SLC_BLOCK, lo, hi)

        def biased(pidx):
            b = bias_sc[pidx]
            b_r = jnp.concatenate([b[0:tq]] * 4 + [b[tq:2 * tq]] * 4, axis=0)
            return s_all[pidx] + b_r

        def p1(pidx, m):
            s = biased(pidx)
            s_all[pidx] = s
            return jnp.maximum(m, s)

        m = lax.fori_loop(0, npages, p1, jnp.full((nrow, LANES), 2.0 * NEG, F32))
        kpos_t = past + lax.broadcasted_iota(jnp.int32, (1, PAGE), 1)
        s_t = jnp.where(kpos_t <= pos_r, biased(npages), NEG)
        s_all[npages] = s_t
        mrow = jnp.max(jnp.maximum(m, s_t), axis=-1, keepdims=True)

        def p2(pidx, carry):
            l, acc = carry
            p = jnp.exp(s_all[pidx] - mrow)
            return l + p, acc + _dot_nt(p.astype(BF16), vt_all[pidx])

        zero = jnp.zeros((nrow, LANES), F32)
        l, acc = p2(npages, lax.fori_loop(0, npages, p2, (zero, zero), unroll=8))
        o_s = acc / jnp.sum(l, axis=-1, keepdims=True)

        wbuf = cwin_ref.shape[3]
        kw_b = cwin_ref[0, 0, 0:128, :].astype(BF16)
        vw_b = cwin_ref[0, 0, 128:256, :].astype(BF16)
        kt_b = wtail_ref[0, 0:128, :].astype(BF16)
        vt_b = wtail_ref[0, 128:256, :].astype(BF16)
        s_w = jnp.concatenate([_dot(qr_aug, kw_b), _dot(qr_aug, kt_b)], axis=1)
        kposw = jnp.concatenate([past - wbuf + lax.broadcasted_iota(jnp.int32, (1, wbuf), 1), kpos_t], axis=1)
        dpos = pos_r - kposw
        p_w = _masked_softmax(s_w, (dpos >= 0) & (dpos < WINDOW) & (kposw >= 0))
        o_w = _dot_nt(p_w[:, 0:wbuf].astype(BF16), vw_b) + _dot_nt(p_w[:, wbuf:].astype(BF16), vt_b)

        sg = jax.nn.sigmoid(dtg_ref[0])
        outs = []
        for g in range(KV_HEADS):
            sl = slice(4 * tq * g, 4 * tq * (g + 1))
            outs.append(_gate_cols(sg, g, 0, tq) * o_c[sl] + _gate_cols(sg, g, 1, tq) * o_s[sl]
                        + _gate_cols(sg, g, 2, tq) * o_w[sl])
        _emit_heads(outs, tq, y_ref)


def _nsa_sample(layer, cache_t, table, tail_t, qn, qr, dtg, w1e, pee, b1e, w2e, ov, cwin_t, wtail_t, *, past, ns):
    b, npages = table.shape
    pps = SAMPLE_PAGES_PER_STEP
    tq = SAMPLE_TQ
    nsp = ov.shape[1]
    nh = npages * (PAGE // CMP_STRIDE)
    nrow = 4 * KV_HEADS * tq
    kern = functools.partial(_nsa_sample_kernel, npages=npages, past=past, ns=ns, nsp=nsp)
    const = lambda arr: pl.BlockSpec(arr.shape, lambda i, j, pt: (0,) * arr.ndim, pipeline_mode=pl.Buffered(1))
    perb = lambda arr: pl.BlockSpec((1,) + arr.shape[1:], lambda i, j, pt: (i,) + (0,) * (arr.ndim - 1))

    def page_spec(k):
        return pl.BlockSpec((1, 1, 512, PAGE), lambda i, j, pt: (layer, pt[i, j * pps + k], 0, 0))

    return pl.pallas_call(
        kern,
        grid_spec=pltpu.PrefetchScalarGridSpec(
            num_scalar_prefetch=1,
            grid=(b, npages // pps),
            in_specs=[page_spec(k) for k in range(pps)] + [
                perb(tail_t), perb(qn), perb(qr), perb(dtg), const(w1e), const(pee), const(b1e), const(w2e),
                const(ov),
                pl.BlockSpec((1, 1) + cwin_t.shape[2:], lambda i, j, pt: (layer, i, 0, 0)), perb(wtail_t)],
            out_specs=pl.BlockSpec((1, tq, 512), lambda i, j, pt: (i, 0, 0)),
            scratch_shapes=[pltpu.VMEM((pps, 2, PAGE, LANES), F32),
                            pltpu.VMEM((2, pps * 8, CMP_STRIDE * LANES), F32),
                            pltpu.VMEM((2, 2, nh, 256), F32),
                            pltpu.VMEM((npages + 1, nrow, LANES), F32),
                            pltpu.VMEM((npages + 1, LANES, PAGE), BF16),
                            pltpu.VMEM((npages + 1, KV_HEADS * tq, LANES), F32)]),
        out_shape=jax.ShapeDtypeStruct((b, tq, 512), F32),
        compiler_params=_cparams(("arbitrary", "arbitrary")),
    )(table, *([cache_t] * pps), tail_t, qn, qr, dtg, w1e, pee, b1e, w2e, ov, cwin_t, wtail_t)


def _oproj_kernel(x_ref, ys_ref, yn_ref, w_ref, g_ref, r_ref, x1_ref, h_ref, lg_ref):
    a = jnp.concatenate([ys_ref[...], yn_ref[...]], axis=1).astype(BF16)
    x1 = x_ref[...] + _dot(a, w_ref[...])
    x1_ref[...] = x1
    h = _rms(x1, g_ref[...])
    h_ref[...] = h.astype(h_ref.dtype)
    lg_ref[...] = _dot_hi(h, r_ref[...])


def _oproj(x, ys, yn, w, g, router, tm, h_dtype):
    t = x.shape[0]
    row = lambda n: pl.BlockSpec((tm, n), lambda i: (i, 0))
    full = lambda a: pl.BlockSpec(a.shape, lambda i: (0,) * a.ndim)
    return pl.pallas_call(
        _oproj_kernel,
        grid=(t // tm,),
        in_specs=[row(D_MODEL), row(512), row(512), full(w), full(g), full(router)],
        out_specs=[row(D_MODEL), row(D_MODEL), row(LANES)],
        out_shape=[jax.ShapeDtypeStruct((t, D_MODEL), F32), jax.ShapeDtypeStruct((t, D_MODEL), h_dtype),
                   jax.ShapeDtypeStruct((t, LANES), F32)],
        compiler_params=_cparams(("arbitrary",)),
    )(x, ys, yn, w, g, router)


def _ffn_kernel(h_ref, x1_ref, wg_ref, wu_ref, wd_ref, o_ref):
    f = pl.program_id(1)

    @pl.when(f == 0)
    def _():
        o_ref[...] = x1_ref[...]

    hb = h_ref[...]
    act = _silu(_dot(hb, wg_ref[...])) * _dot(hb, wu_ref[...])
    o_ref[...] += _dot(act.astype(BF16), wd_ref[...])


def _ffn(h, x1, wg, wu, wd, tm, tf):
    t = h.shape[0]
    dff = wg.shape[1]
    return pl.pallas_call(
        _ffn_kernel,
        grid=(t // tm, dff // tf),
        in_specs=[pl.BlockSpec((tm, D_MODEL), lambda i, f: (i, 0)),
                  pl.BlockSpec((tm, D_MODEL), lambda i, f: (i, 0)),
                  pl.BlockSpec((D_MODEL, tf), lambda i, f: (0, f)),
                  pl.BlockSpec((D_MODEL, tf), lambda i, f: (0, f)),
                  pl.BlockSpec((tf, D_MODEL), lambda i, f: (f, 0))],
        out_specs=pl.BlockSpec((tm, D_MODEL), lambda i, f: (i, 0)),
        out_shape=jax.ShapeDtypeStruct((t, D_MODEL), F32),
        compiler_params=_cparams(("arbitrary", "arbitrary")),
    )(h, x1, wg, wu, wd)


def _moe_kernel(be_ref, nb_ref, tok_ref, h_hbm, wg_ref, wu_ref, wd_ref, o_ref, xbuf, xb16, sem, *, tmb):
    b = pl.program_id(0)
    f = pl.program_id(1)
    nb = nb_ref[0]

    def request_rows(blk, slot):
        def body(r, carry):
            tok = tok_ref[blk * tmb + r]
            pltpu.make_async_copy(h_hbm.at[pl.ds(tok, 1), :], xbuf.at[slot, pl.ds(r, 1), :], sem.at[slot]).start()
            return carry
        lax.fori_loop(0, tmb, body, 0, unroll=8)

    @pl.when(f == 0)
    def _():
        o_ref[...] = jnp.zeros(o_ref.shape, F32)

        @pl.when((b == 0) & (nb > 0))
        def _():
            request_rows(0, 0)

        @pl.when(b + 1 < nb)
        def _():
            request_rows(b + 1, (b + 1) & 1)

        @pl.when(b < nb)
        def _():
            slot = b & 1
            pltpu.make_async_copy(xbuf.at[slot], xbuf.at[slot], sem.at[slot]).wait()
            xb16[...] = xbuf[slot].astype(BF16)

    @pl.when(b < nb)
    def _():
        hb = xb16[...]
        act = _silu(_dot(hb, wg_ref[0].astype(BF16))) * _dot(hb, wu_ref[0].astype(BF16))
        o_ref[...] += _dot(act.astype(BF16), wd_ref[0].astype(BF16))


def _moe(blk_e, nblk_used, buf_tok, h_pad, wg, wu, wd, tmb, tf):
    n_blk = buf_tok.shape[0] // tmb
    dex = wg.shape[2]
    nf = dex // tf

    def feff(b, f, nb):
        return jnp.where(b < nb[0], f, nf - 1)

    return pl.pallas_call(
        functools.partial(_moe_kernel, tmb=tmb),
        grid_spec=pltpu.PrefetchScalarGridSpec(
            num_scalar_prefetch=3,
            grid=(n_blk, nf),
            in_specs=[pl.BlockSpec(memory_space=pl.ANY),
                      pl.BlockSpec((1, D_MODEL, tf), lambda b, f, be, nb, tok: (be[b], 0, feff(b, f, nb))),
                      pl.BlockSpec((1, D_MODEL, tf), lambda b, f, be, nb, tok: (be[b], 0, feff(b, f, nb))),
                      pl.BlockSpec((1, tf, D_MODEL), lambda b, f, be, nb, tok: (be[b], feff(b, f, nb), 0))],
            out_specs=pl.BlockSpec((tmb, D_MODEL), lambda b, f, be, nb, tok: (b, 0)),
            scratch_shapes=[pltpu.VMEM((2, tmb, D_MODEL), F32), pltpu.VMEM((tmb, D_MODEL), BF16),
                            pltpu.SemaphoreType.DMA((2,))]),
        out_shape=jax.ShapeDtypeStruct((n_blk * tmb, D_MODEL), F32),
        compiler_params=_cparams(("arbitrary", "arbitrary")),
    )(blk_e, nblk_used, buf_tok, h_pad, wg, wu, wd)


def _moe_layer(h, x1, logits, wg, wu, wd, tmb, tf):
    t = h.shape[0]
    top_v, top_e = lax.top_k(logits[:, :N_EXPERTS], TOP_K)
    gate = jax.nn.softmax(top_v, axis=-1)
    e_flat = top_e.reshape(-1)
    npair = t * TOP_K
    order = jnp.argsort(e_flat)
    rank = jnp.argsort(order).astype(jnp.int32)
    counts = jnp.bincount(e_flat, length=N_EXPERTS).astype(jnp.int32)
    starts = jnp.cumsum(counts) - counts
    padded = (counts + tmb - 1) // tmb * tmb
    pends = jnp.cumsum(padded)
    pstarts = pends - padded
    dest = (pstarts[e_flat] + rank - starts[e_flat]).astype(jnp.int32).reshape(t, TOP_K)
    n_blk = -(-npair // tmb) + N_EXPERTS
    blk_e = jnp.minimum(jnp.searchsorted(pends, jnp.arange(n_blk, dtype=jnp.int32) * tmb, side='right'),
                        N_EXPERTS - 1).astype(jnp.int32)
    nblk_used = (pends[-1] // tmb).astype(jnp.int32).reshape(1)
    e_p = jnp.repeat(blk_e, tmb)
    off = jnp.arange(n_blk * tmb, dtype=jnp.int32) - pstarts[e_p]
    src = order[jnp.clip(starts[e_p] + off, 0, npair - 1)].astype(jnp.int32)
    buf_tok = jnp.where(off < counts[e_p], src // TOP_K, t)
    h_pad = jnp.concatenate([h, jnp.zeros((8, D_MODEL), h.dtype)], axis=0)
    out = _moe(blk_e, nblk_used, buf_tok.astype(jnp.int32), h_pad, wg, wu, wd, tmb, tf)
    return x1 + out[dest[:, 0]] * gate[:, 0:1] + out[dest[:, 1]] * gate[:, 1:2]


def _final_kernel(x_ref, g_ref, o_ref):
    o_ref[...] = _rms(x_ref[...], g_ref[...])


def _final_norm(x, g, tm):
    t = x.shape[0]
    return pl.pallas_call(
        _final_kernel,
        grid=(t // tm,),
        in_specs=[pl.BlockSpec((tm, D_MODEL), lambda i: (i, 0)), pl.BlockSpec(g.shape, lambda i: (0, 0))],
        out_specs=pl.BlockSpec((tm, D_MODEL), lambda i: (i, 0)),
        out_shape=jax.ShapeDtypeStruct((t, D_MODEL), F32),
        compiler_params=_cparams(("arbitrary",)),
    )(x, g)


def _rope_tables(pos):
    inv = 1.0 / (ROPE_THETA ** (jnp.arange(0, HEAD_DIM, 2, dtype=F32) / HEAD_DIM))
    ang = pos.astype(F32)[:, None] * inv[None, :]
    cos = jnp.cos(ang)
    sin = jnp.sin(ang)
    return jnp.concatenate([cos, cos, cos, cos], axis=1), jnp.concatenate([-sin, sin, -sin, sin], axis=1)


def _overlap(ncb, nsp, ns):
    c = jnp.arange(ncb, dtype=jnp.int32)[:, None] * CMP_STRIDE
    s = jnp.arange(nsp, dtype=jnp.int32)[None, :] * SLC_BLOCK
    ov = (c <= s + SLC_BLOCK - 1) & (c + CMP_BLOCK - 1 >= s) & (jnp.arange(nsp)[None, :] < ns)
    return ov.astype(BF16)


def _pad_lanes(v, n):
    return jnp.zeros((1, n), F32).at[0, :v.shape[0]].set(v.astype(F32))


def kernel(x_prompt, x_sample, cache_kv, cache_win, state_ssm, state_conv, page_table, ln_mix, w_in, conv_w,
           conv_b, dt_bias, a_log, d_skip, ssd_norm, cmp_pos, cmp_w1, cmp_b1, cmp_w2, w_out, ln_ffn,
           ffn_w_gate, ffn_w_up, ffn_w_down, moe_router, moe_w_gate, moe_w_up, moe_w_down, ln_final):
    depth = w_in.shape[0]
    bsz, seq = x_prompt.shape[:2]
    dbsz, dseq = x_sample.shape[:2]
    n_pages = page_table.shape[1]
    past = n_pages * cache_kv.shape[2]
    win_buf = cache_win.shape[2]
    npool = cache_kv.shape[1]
    assert bsz == 1 and cache_kv.shape[2] == PAGE and seq % PAGE == 0 and seq >= WINDOW + PAGE
    assert win_buf == WINDOW and past % PAGE == 0
    assert not any(past <= CMP_STRIDE * c + CMP_BLOCK - 1 <= past + dseq - 1
                   for c in range(past // CMP_STRIDE - 2, past // CMP_STRIDE + 2))

    ts = dbsz * dseq
    ns_p = seq // SLC_BLOCK
    ns_s = -(-(past + dseq) // SLC_BLOCK)
    nsp_p = -(-ns_p // LANES) * LANES
    nsp_s = -(-ns_s // LANES) * LANES
    assert dseq <= SAMPLE_TQ and n_pages % SAMPLE_PAGES_PER_STEP == 0
    cache_t = jnp.transpose(cache_kv, (0, 1, 3, 4, 5, 2)).reshape(depth, npool, 512, PAGE)
    cwin_t = jnp.transpose(cache_win, (0, 1, 3, 4, 5, 2)).reshape(depth, dbsz, 256, win_buf)

    cs_p, sn_p = _rope_tables(jnp.arange(seq, dtype=jnp.int32))
    cs_s, sn_s = _rope_tables(past + jnp.arange(ts, dtype=jnp.int32) % dseq)
    ov_s = _overlap(past // CMP_STRIDE, nsp_s, ns_s)
    ident = jnp.arange(seq // PAGE, dtype=jnp.int32).reshape(1, -1)

    xp = x_prompt.reshape(seq, D_MODEL)
    xs = x_sample.reshape(ts, D_MODEL)
    outs = {k: [] for k in ('kv_p', 'kv_s', 'win_p', 'win_s', 'ssm_p', 'ssm_s', 'conv_p', 'conv_s')}

    for l in range(depth):
        w = w_in[l]
        wr = jnp.concatenate([w[:, :1536], w[:, 1544:2824], w[:, 1536:1544], w[:, 2824:2848],
                              jnp.zeros((D_MODEL, C_END - 2848), F32)], axis=1).astype(BF16)
        g_mix = ln_mix[l].reshape(1, -1)
        cw8 = jnp.zeros((8, CONV_DIM), F32).at[:CONV_W].set(conv_w[l])
        cb = conv_b[l].reshape(1, -1)
        dtb = _pad_lanes(dt_bias[l], LANES)
        a_neg = _pad_lanes(-jnp.exp(a_log[l].astype(F32)), LANES)
        dsk = jnp.repeat(d_skip[l].astype(F32), SSD_HEAD_DIM).reshape(1, -1)
        ng = ssd_norm[l].reshape(1, -1)
        w1 = cmp_w1[l].reshape(2, 2, CMP_STRIDE, HEAD_DIM, CMP_HIDDEN)
        eye = jnp.eye(KV_HEADS, dtype=F32)
        w1e = jnp.einsum('vspdh,gk->vspgdkh', w1, eye).reshape(2, 2, CMP_STRIDE * LANES, 256).astype(BF16)
        pe = cmp_pos[l].reshape(2, 2, CMP_STRIDE, 1, HEAD_DIM)
        pee = jnp.broadcast_to(pe, (2, 2, CMP_STRIDE, KV_HEADS, HEAD_DIM)).reshape(2, 2, CMP_STRIDE * LANES)
        b1e = jnp.concatenate([cmp_b1[l], cmp_b1[l]], axis=1)
        w2e = jnp.einsum('vhd,gk->vghkd', cmp_w2[l], eye).reshape(2, 256, LANES).astype(BF16)
        wo = w_out[l].astype(BF16)
        g_ffn = ln_ffn[l].reshape(1, -1)
        if l % 2 == 0:
            router = jnp.zeros((D_MODEL, LANES), F32)
        else:
            router = jnp.zeros((D_MODEL, LANES), F32).at[:, :N_EXPERTS].set(moe_router[l // 2])

        z, xbc, dtg, q, qr, kv, win = _front(xp, g_mix, wr, cs_p, sn_p, 512)
        y_ssd, hist, hfin = _ssd(z[None], xbc[None], dtg[None], jnp.zeros((1, 8, CONV_DIM), F32),
                                 jnp.zeros((1, D_SSD, SSD_STATE), F32), cw8, cb, dtb, a_neg, dsk, ng, SSD_CHUNK)
        kv_pages = kv.reshape(seq // PAGE, PAGE, 512)
        kchi, kclo, vct = _compress(kv_pages, ident, w1e, pee, b1e, w2e)
        kaug, vt, kwin, vwt = _kvfmt(kv, win, NSA_TK)
        y_nsa = _nsa(q[None], qr[None], dtg[None], kchi, kclo, vct, kaug, vt, kwin, vwt,
                     tq=128, ns=ns_p, nsp=nsp_p, tk=NSA_TK, wl=WINDOW + 128)
        h_dtype = BF16 if l % 2 == 0 else F32
        x1, h, logits = _oproj(xp, y_ssd[0], y_nsa[0], wo, g_ffn, router, 512, h_dtype)
        outs['kv_p'].append(kv.reshape(1, seq, 4, KV_HEADS, HEAD_DIM))
        outs['win_p'].append(win[seq - WINDOW:].reshape(1, WINDOW, 2, KV_HEADS, HEAD_DIM))
        outs['ssm_p'].append(hfin.reshape(1, SSD_HEADS, SSD_HEAD_DIM, SSD_STATE))
        outs['conv_p'].append(hist[:, 5:8])

        zs, xbcs, dtgs, qs, qrs, kvs, wins = _front(xs, g_mix, wr, cs_s, sn_s, ts)
        padl = lambda v: jnp.pad(v.reshape(dbsz, dseq, -1), ((0, 0), (0, SSD_CHUNK - dseq), (0, 0)))
        hist8 = jnp.pad(state_conv[l], ((0, 0), (8 - (CONV_W - 1), 0), (0, 0)))
        y_ssd_s, hist_s, hfin_s = _ssd(padl(zs), padl(xbcs), padl(dtgs), hist8,
                                       state_ssm[l].reshape(dbsz, D_SSD, SSD_STATE),
                                       cw8, cb, dtb, a_neg, dsk, ng, dseq)
        tail_t = jnp.pad(jnp.transpose(kvs.reshape(dbsz, dseq, 512), (0, 2, 1)), ((0, 0), (0, 0), (0, PAGE - dseq)))
        wtail_t = jnp.pad(jnp.transpose(wins.reshape(dbsz, dseq, 256), (0, 2, 1)), ((0, 0), (0, 0), (0, PAGE - dseq)))
        padq = lambda v: jnp.pad(v.reshape(dbsz, dseq, -1), ((0, 0), (0, SAMPLE_TQ - dseq), (0, 0)))
        y_nsa_s = _nsa_sample(l, cache_t, page_table, tail_t, padq(qs), padq(qrs), padq(dtgs), w1e, pee, b1e, w2e,
                              ov_s, cwin_t, wtail_t, past=past, ns=ns_s)
        y_nsa_s = y_nsa_s[:, :dseq].reshape(ts, D_NSA)
        x1s, hs, logits_s = _oproj(xs, y_ssd_s[:, :dseq].reshape(ts, D_SSD), y_nsa_s, wo, g_ffn, router, ts, h_dtype)
        outs['kv_s'].append(kvs.reshape(dbsz, dseq, 4, KV_HEADS, HEAD_DIM))
        win_all = jnp.concatenate([cache_win[l], wins.reshape(dbsz, dseq, 2, KV_HEADS, HEAD_DIM)], axis=1)
        outs['win_s'].append(win_all[:, dseq:])
        outs['ssm_s'].append(hfin_s.reshape(dbsz, SSD_HEADS, SSD_HEAD_DIM, SSD_STATE))
        outs['conv_s'].append(hist_s[:, 5:8])

        if l % 2 == 0:
            wg = ffn_w_gate[l // 2].astype(BF16)
            wu = ffn_w_up[l // 2].astype(BF16)
            wd = ffn_w_down[l // 2].astype(BF16)
            xp = _ffn(h, x1, wg, wu, wd, 1024, 256)
            xs = _ffn(hs, x1s, wg, wu, wd, ts, 256)
        else:
            wg = moe_w_gate[l // 2]
            wu = moe_w_up[l // 2]
            wd = moe_w_down[l // 2]
            xp = _moe_layer(h, x1, logits, wg, wu, wd, 1024, 512)
            xs = _moe_layer(hs, x1s, logits_s, wg, wu, wd, 128, 512)

    g_fin = ln_final.reshape(1, -1)
    y_prompt = _final_norm(xp, g_fin, 1024).reshape(bsz, seq, D_MODEL)
    y_sample = _final_norm(xs, g_fin, ts).reshape(dbsz, dseq, D_MODEL)
    st = lambda k: jnp.stack(outs[k])
    return (y_prompt, y_sample, st('kv_p'), st('kv_s'), st('win_p'), st('win_s'),
            st('ssm_p'), st('ssm_s'), st('conv_p'), st('conv_s'))
```

```python
import functools

import jax
import jax.numpy as jnp
from jax import lax
from jax.experimental import pallas as pl
from jax.experimental.pallas import tpu as pltpu

F32 = jnp.float32
BF16 = jnp.bfloat16

D_MODEL = 1024
SSD_HEADS = 8
SSD_HEAD_DIM = 64
D_SSD = 512
SSD_GROUPS = 2
SSD_STATE = 128
CONV_W = 4
CONV_DIM = 1024
SSD_CHUNK = 128
NSA_HEADS = 8
HEAD_DIM = 64
KV_HEADS = 2
D_NSA = 512
CMP_BLOCK = 32
CMP_STRIDE = 16
CMP_HIDDEN = 128
SLC_BLOCK = 64
TOP_N = 16
WINDOW = 512
ROPE_THETA = 10000.0
FORCE_BONUS = 1.0e4
N_EXPERTS = 8
TOP_K = 2
EPS = 1e-6

PAGE = 128
LANES = 128
LOG2E = 1.4426950408889634
NEG = -1.0e30
SLAB_BLOCKS = 128
VMEM_LIMIT = 56 * 1024 * 1024

C_Z, C_XBC, C_Q, C_KV, C_DT, C_END = 0, 512, 1536, 2048, 2816, 2944


def _cparams(sem):
    return pltpu.CompilerParams(dimension_semantics=sem, vmem_limit_bytes=VMEM_LIMIT)


def _dot(a, b):
    return jnp.dot(a, b, preferred_element_type=F32)


def _dot_nt(a, b):
    return lax.dot_general(a, b, (((1,), (1,)), ((), ())), preferred_element_type=F32)


def _split3(x):
    h1 = x.astype(BF16)
    r1 = x - h1.astype(F32)
    h2 = r1.astype(BF16)
    h3 = (r1 - h2.astype(F32)).astype(BF16)
    return h1, h2, h3


def _dot_nt_hi(a, b):
    a1, a2, _ = _split3(a)
    b1, b2, _ = _split3(b)
    return _dot_nt(a1, b1) + (_dot_nt(a1, b2) + _dot_nt(a2, b1))


def _dot_hi(a, b):
    a1, a2, _ = _split3(a)
    b1, b2, _ = _split3(b)
    return _dot(a1, b1) + (_dot(a1, b2) + _dot(a2, b1))


def _silu(x):
    return x * jax.nn.sigmoid(x)


def _rms(x, g):
    return x * lax.rsqrt(jnp.mean(x * x, axis=-1, keepdims=True) + EPS) * g


def _front_kernel(x_ref, g_ref, w_ref, cs_ref, sn_ref,
                  z_ref, xbc_ref, dtg_ref, q_ref, qr_ref, kv_ref, win_ref):
    xn = _rms(x_ref[...], g_ref[...])
    p = _dot(xn.astype(BF16), w_ref[...])
    cs = cs_ref[...]
    sn = sn_ref[...]
    lane = lax.broadcasted_iota(jnp.int32, cs.shape, 1)
    first = (lane % HEAD_DIM) < (HEAD_DIM // 2)

    def rope(v):
        sw = jnp.where(first, pltpu.roll(v, LANES - HEAD_DIM // 2, 1), pltpu.roll(v, HEAD_DIM // 2, 1))
        return v * cs + sw * sn

    z_ref[...] = p[:, C_Z:C_XBC]
    xbc_ref[...] = p[:, C_XBC:C_Q]
    dtg_ref[...] = p[:, C_DT:C_END]
    q_ref[...] = p[:, C_Q:C_KV]
    for s in range(4):
        qr_ref[:, s * LANES:(s + 1) * LANES] = rope(p[:, C_Q + s * LANES:C_Q + (s + 1) * LANES])
    kv_ref[:, 0:256] = p[:, C_KV:C_KV + 256]
    kv_ref[:, 256:384] = rope(p[:, C_KV + 256:C_KV + 384])
    kv_ref[:, 384:512] = p[:, C_KV + 384:C_KV + 512]
    win_ref[:, 0:128] = rope(p[:, C_KV + 512:C_KV + 640])
    win_ref[:, 128:256] = p[:, C_KV + 640:C_KV + 768]


def _front(x, g, w, cs, sn, tm):
    t = x.shape[0]
    row = lambda n: pl.BlockSpec((tm, n), lambda i: (i, 0))
    full = lambda a: pl.BlockSpec(a.shape, lambda i: (0,) * a.ndim)
    outs = [512, 1024, 128, 512, 512, 512, 256]
    return pl.pallas_call(
        _front_kernel,
        grid=(t // tm,),
        in_specs=[row(D_MODEL), full(g), full(w), row(LANES), row(LANES)],
        out_specs=[row(n) for n in outs],
        out_shape=[jax.ShapeDtypeStruct((t, n), F32) for n in outs],
        compiler_params=_cparams(("arbitrary",)),
    )(x, g, w, cs, sn)


def _expand_heads(v):
    r = v.shape[0]
    lane = lax.broadcasted_iota(jnp.int32, (r, LANES), 1)
    outs = []
    for k in range(4):
        a = jnp.broadcast_to(v[:, 2 * k:2 * k + 1], (r, LANES))
        b = jnp.broadcast_to(v[:, 2 * k + 1:2 * k + 2], (r, LANES))
        outs.append(jnp.where(lane < SSD_HEAD_DIM, a, b))
    return jnp.concatenate(outs, axis=1)


def _ssd_kernel(z_ref, xbc_ref, dtg_ref, hist_ref, h0_ref, cw_ref, cb_ref, dtb_ref, a_ref, dsk_ref, ng_ref,
                y_ref, hist_o_ref, hfin_ref, xpad, ht, *, nchunks, valid_last):
    c = pl.program_id(1)
    q = SSD_CHUNK

    @pl.when(c == 0)
    def _():
        xpad[0:8, :] = hist_ref[0]
        ht[...] = h0_ref[0].T

    xpad[8:8 + q, :] = xbc_ref[0]
    conv = cb_ref[...]
    for k in range(CONV_W):
        conv = conv + xpad[5 + k:5 + k + q, :] * cw_ref[k:k + 1, :]
    xc = _silu(conv)
    xs = xc[:, 0:D_SSD]
    bm = xc[:, D_SSD:D_SSD + 256]
    cm = xc[:, D_SSD + 256:D_SSD + 512]

    row = lax.broadcasted_iota(jnp.int32, (q, q), 0)
    col = lax.broadcasted_iota(jnp.int32, (q, q), 1)
    nvalid = jnp.where(c == nchunks - 1, valid_last, q)
    xdt_in = dtg_ref[0] + dtb_ref[...]
    dt = jnp.maximum(xdt_in, 0.0) + jnp.log1p(jnp.exp(-jnp.abs(xdt_in)))
    dt = jnp.where(row < nvalid, dt, 0.0)
    dta = dt * a_ref[...]
    causal = row >= col
    cum = jnp.dot(causal.astype(F32), dta, preferred_element_type=F32, precision=lax.Precision.HIGHEST)
    cum_t = cum.T
    ecum = jnp.exp(cum)
    toend = jnp.exp(cum[q - 1:q, :] - cum)
    dt_e = _expand_heads(dt)
    ecum_e = _expand_heads(ecum)
    toend_e = _expand_heads(toend)
    xdt = xs * dt_e
    xdt_b = xdt.astype(BF16)
    xw_b = (xdt * toend_e).astype(BF16)
    lane = lax.broadcasted_iota(jnp.int32, (q, LANES), 1)

    y_intra = []
    y_inter = []
    for g in range(SSD_GROUPS):
        bg = bm[:, g * 128:(g + 1) * 128]
        cg_b = cm[:, g * 128:(g + 1) * 128].astype(BF16)
        bg_b = bg.astype(BF16)
        cb = _dot_nt(cg_b, bg_b)
        htg = ht[:, g * 256:(g + 1) * 256]
        y_inter.append(_dot(cg_b, htg.astype(BF16)))
        for kk in range(2):
            slab = xdt_b[:, (2 * g + kk) * LANES:(2 * g + kk + 1) * LANES]
            res = []
            for hh in range(2):
                h = 4 * g + 2 * kk + hh
                seg = cum[:, h:h + 1] - cum_t[h:h + 1, :]
                decay = jnp.exp(jnp.where(causal, seg, NEG))
                res.append(_dot((decay * cb).astype(BF16), slab))
            y_intra.append(jnp.where(lane < SSD_HEAD_DIM, res[0], res[1]))
        s_new = _dot(bg.T.astype(BF16), xw_b[:, g * 256:(g + 1) * 256])
        ht[:, g * 256:(g + 1) * 256] = htg * ecum_e[q - 1:q, g * 256:(g + 1) * 256] + s_new

    y = jnp.concatenate(y_intra, axis=1) + jnp.concatenate(y_inter, axis=1) * ecum_e + dsk_ref[...] * xs
    y = y * _silu(z_ref[0])
    y_ref[0] = _rms(y, ng_ref[...])

    xpad[0:8, :] = xpad[q:q + 8, :]

    @pl.when(c == nchunks - 1)
    def _():
        hist_o_ref[0] = xpad[valid_last:valid_last + 8, :]
        hfin_ref[0] = ht[...].T


def _ssd(z, xbc, dtg, hist8, h0, cw8, cb, dtb, a, dsk, ng, valid_last):
    b, l = z.shape[:2]
    nchunks = l // SSD_CHUNK
    rows = lambda n: pl.BlockSpec((1, SSD_CHUNK, n), lambda i, c: (i, c, 0))
    perb = lambda r, n: pl.BlockSpec((1, r, n), lambda i, c: (i, 0, 0))
    full = lambda arr: pl.BlockSpec(arr.shape, lambda i, c: (0,) * arr.ndim)
    kern = functools.partial(_ssd_kernel, nchunks=nchunks, valid_last=valid_last)
    return pl.pallas_call(
        kern,
        grid=(b, nchunks),
        in_specs=[rows(512), rows(1024), rows(128), perb(8, 1024), perb(512, 128),
                  full(cw8), full(cb), full(dtb), full(a), full(dsk), full(ng)],
        out_specs=[rows(512), perb(8, 1024), perb(512, 128)],
        out_shape=[jax.ShapeDtypeStruct((b, l, 512), F32),
                   jax.ShapeDtypeStruct((b, 8, 1024), F32),
                   jax.ShapeDtypeStruct((b, 512, 128), F32)],
        scratch_shapes=[pltpu.VMEM((SSD_CHUNK + 8, CONV_DIM), F32), pltpu.VMEM((SSD_STATE, D_SSD), F32)],
        compiler_params=_cparams(("arbitrary", "arbitrary")),
    )(z, xbc, dtg, hist8, h0, cw8, cb, dtb, a, dsk, ng)


CMP_CHUNK = 256
SUM_ROWS = 16
NSA_TK = 512
CMP_STEP_ROWS = 512
CMP_GROUP_HB = 128


def _compress_kernel(pt_ref, pgk_ref, pgv_ref, w1_ref, pe_ref, b1_ref, w2_ref, w2t_ref, kchi_ref, kclo_ref, vct_ref,
                     xs, ps, *, npages):
    p = pl.program_id(1)
    hb = pgk_ref.shape[1] // CMP_STRIDE
    per_group = CMP_GROUP_HB // hb
    sub = p % per_group
    r0 = pl.multiple_of(sub * hb, hb)
    for kv, pg_ref in enumerate((pgk_ref, pgv_ref)):
        for pos in range(CMP_STRIDE):
            xs[kv, pl.ds(r0, hb), pos * LANES:(pos + 1) * LANES] = pg_ref[0, pl.ds(pos, hb, stride=CMP_STRIDE), :]

    @pl.when(sub == per_group - 1)
    def _():
        rows = CMP_GROUP_HB
        g0 = pl.multiple_of((p // per_group) * rows, rows)
        for kv in range(2):
            x = xs[kv]
            for s in range(2):
                ps[kv, s, pl.ds(g0, rows), :] = _dot((x + pe_ref[kv, s:s + 1, :]).astype(BF16), w1_ref[kv, s])

    @pl.when(p == npages - 1)
    def _():
        nh = npages * hb
        act = [jax.nn.gelu(b1_ref[kv:kv + 1, :] + ps[kv, 0] + pltpu.roll(ps[kv, 1], nh - 1, 0)).astype(BF16)
               for kv in range(2)]
        kc = _dot(act[0], w2_ref[0])
        hi = kc.astype(BF16)
        kchi_ref[0] = hi
        kclo_ref[0] = (kc - hi.astype(F32)).astype(BF16)
        vct_ref[0] = _dot_nt(w2t_ref[...], act[1]).astype(BF16)


def _compress(pages, table, w1e, pee, b1e, w2e):
    b, npages = table.shape
    rows = pages.shape[1]
    assert CMP_GROUP_HB % (rows // CMP_STRIDE) == 0 and (npages * rows // CMP_STRIDE) % CMP_GROUP_HB == 0
    w2t = w2e[1].T
    nh = npages * (rows // CMP_STRIDE)
    kern = functools.partial(_compress_kernel, npages=npages)
    full = lambda arr: pl.BlockSpec(arr.shape, lambda i, p, pt: (0,) * arr.ndim)
    return pl.pallas_call(
        kern,
        grid_spec=pltpu.PrefetchScalarGridSpec(
            num_scalar_prefetch=1,
            grid=(b, npages),
            in_specs=[pl.BlockSpec((1, rows, LANES), lambda i, p, pt: (pt[i, p], 0, 0)),
                      pl.BlockSpec((1, rows, LANES), lambda i, p, pt: (pt[i, p], 0, 1)),
                      full(w1e), full(pee), full(b1e), full(w2e), full(w2t)],
            out_specs=[pl.BlockSpec((1, nh, LANES), lambda i, p, pt: (i, 0, 0)),
                       pl.BlockSpec((1, nh, LANES), lambda i, p, pt: (i, 0, 0)),
                       pl.BlockSpec((1, LANES, nh), lambda i, p, pt: (i, 0, 0))],
            scratch_shapes=[pltpu.VMEM((2, CMP_GROUP_HB, CMP_STRIDE * LANES), F32),
                            pltpu.VMEM((2, 2, nh, 256), F32)]),
        out_shape=[jax.ShapeDtypeStruct((b, nh, LANES), BF16), jax.ShapeDtypeStruct((b, nh, LANES), BF16),
                   jax.ShapeDtypeStruct((b, LANES, nh), BF16)],
        compiler_params=_cparams(("arbitrary", "arbitrary")),
    )(table, pages, pages, w1e, pee, b1e, w2e, w2t)


def _kvfmt_kernel(kv_ref, win_ref, k_ref, vt_ref, kw_ref, vwt_ref):
    p = pl.program_id(0)
    rows = kv_ref.shape[0]
    src = kv_ref[...]
    r = lax.broadcasted_iota(jnp.int32, (rows, LANES), 0)
    lane = lax.broadcasted_iota(jnp.int32, (rows, LANES), 1)
    blk = (p * (rows // SLC_BLOCK) + r // SLC_BLOCK) % SLAB_BLOCKS
    k_ref[:, 0:128] = src[:, 0:128].astype(BF16)
    k_ref[:, 128:256] = jnp.where(lane == blk, 1.0, 0.0).astype(BF16)
    kw_ref[...] = win_ref[:, 0:128].astype(BF16)
    for i in range(rows // PAGE):
        v_t = src[i * PAGE:(i + 1) * PAGE, 128:256].T.astype(BF16)
        for g in range(KV_HEADS):
            vt_ref[0, g, 0:HEAD_DIM, i * PAGE:(i + 1) * PAGE] = v_t[g * HEAD_DIM:(g + 1) * HEAD_DIM, :]
        vwt_ref[i] = win_ref[i * PAGE:(i + 1) * PAGE, 128:256].T.astype(BF16)
    for g in range(KV_HEADS):
        vt_ref[0, g, HEAD_DIM:HEAD_DIM + SUM_ROWS, :] = jnp.ones((SUM_ROWS, rows), BF16)


def _kvfmt(kv, win, tk):
    l = kv.shape[0]
    return pl.pallas_call(
        _kvfmt_kernel,
        grid=(l // tk,),
        in_specs=[pl.BlockSpec((tk, 256), lambda p: (p, 1)), pl.BlockSpec((tk, 256), lambda p: (p, 0))],
        out_specs=[pl.BlockSpec((tk, 256), lambda p: (p, 0)),
                   pl.BlockSpec((1, KV_HEADS, HEAD_DIM + SUM_ROWS, tk), lambda p: (p, 0, 0, 0)),
                   pl.BlockSpec((tk, LANES), lambda p: (p, 0)),
                   pl.BlockSpec((tk // PAGE, LANES, PAGE), lambda p: (p, 0, 0))],
        out_shape=[jax.ShapeDtypeStruct((l, 256), BF16),
                   jax.ShapeDtypeStruct((l // tk, KV_HEADS, HEAD_DIM + SUM_ROWS, tk), BF16),
                   jax.ShapeDtypeStruct((l, LANES), BF16),
                   jax.ShapeDtypeStruct((l // PAGE, LANES, PAGE), BF16)],
        compiler_params=_cparams(("arbitrary",)),
    )(kv, win)


def _group_rows(qfull, g, tq):
    lane = lax.broadcasted_iota(jnp.int32, (tq, LANES), 1)
    keep = (lane >= HEAD_DIM) if g else (lane < HEAD_DIM)
    pieces = []
    for r in range(4):
        h = 4 * g + r
        slab = qfull[:, (h // 2) * LANES:(h // 2 + 1) * LANES]
        if h % 2 != g:
            slab = pltpu.roll(slab, HEAD_DIM, 1)
        pieces.append(jnp.where(keep, slab, 0.0))
    return jnp.concatenate(pieces, axis=0)


def _masked_softmax(s, mask):
    s = jnp.where(mask, s, NEG)
    m = jnp.max(s, axis=-1, keepdims=True)
    e = jnp.where(mask, jnp.exp(s - m), 0.0)
    return e / jnp.maximum(jnp.sum(e, axis=-1, keepdims=True), 1e-30)


def _importance(p4, ov):
    h1, h2, h3 = _split3(p4)
    return _dot(h1, ov) + (_dot(h2, ov) + _dot(h3, ov))


def _topk_bias(imp, qpos, ns):
    tq, nsp = imp.shape
    jf = lax.broadcasted_iota(jnp.int32, (tq, nsp), 1)
    jff = jf.astype(F32)
    cur = qpos >> 6
    valid = jf <= cur
    forced = valid & ((jf == 0) | (jf == cur) | (jf == cur - 1))
    score = jnp.where(forced, FORCE_BONUS, jnp.where(valid, imp, -1.0))
    score = jnp.where(jf < ns, score, -2.0)

    def pick(_, carry):
        sc, sb = carry
        mx = jnp.max(sc, axis=-1, keepdims=True)
        jm = jnp.min(jnp.where(sc == mx, jff, float(nsp)), axis=-1, keepdims=True)
        hit = jff == jm
        return jnp.where(hit, -3.0, sc), jnp.where(hit, 0.0, sb)

    return lax.fori_loop(0, TOP_N, pick, (score, jnp.full((tq, nsp), NEG, F32)))[1]


def _gate_cols(sg, g, k, tq):
    return jnp.concatenate([sg[:, 8 + 3 * (4 * g + r) + k:8 + 3 * (4 * g + r) + k + 1] for r in range(4)], axis=0)


def _emit_heads(outs, tq, y_ref):
    lane = lax.broadcasted_iota(jnp.int32, (tq, LANES), 1)
    for k in range(4):
        g = k // 2
        r0 = (2 * k) % 4
        a = outs[g][r0 * tq:(r0 + 1) * tq]
        b = outs[g][(r0 + 1) * tq:(r0 + 2) * tq]
        if g == 0:
            slab = jnp.where(lane < HEAD_DIM, a, pltpu.roll(b, HEAD_DIM, 1))
        else:
            slab = jnp.where(lane < HEAD_DIM, pltpu.roll(a, HEAD_DIM, 1), b)
        y_ref[0, :, k * LANES:(k + 1) * LANES] = slab


def _masked_softmax_t(s, mask):
    s = jnp.where(mask, s, NEG)
    m = jnp.max(s, axis=0, keepdims=True)
    e = jnp.exp(s - m)
    inv = jnp.where(m > 0.5 * NEG, 1.0 / jnp.maximum(jnp.sum(e, axis=0, keepdims=True), 1e-30), 0.0)
    return e * inv


def _topk_bias_t(imp, qpos, ns):
    nsp, tq = imp.shape
    jf = lax.broadcasted_iota(jnp.int32, (nsp, tq), 0)
    jff = jf.astype(F32)
    cur = qpos >> 6
    valid = jf <= cur
    forced = valid & ((jf == 0) | (jf == cur) | (jf == cur - 1))
    score = jnp.where(forced, FORCE_BONUS, jnp.where(valid, imp, -1.0))
    score = jnp.where(jf < ns, score, -2.0)

    def pick(_, carry):
        sc, sb = carry
        mx = jnp.max(sc, axis=0, keepdims=True)
        jm = jnp.min(jnp.where(sc == mx, jff, float(nsp)), axis=0, keepdims=True)
        hit = jff == jm
        return jnp.where(hit, -3.0, sc), jnp.where(hit, 0.0, sb)

    return lax.fori_loop(0, TOP_N, pick, (score, jnp.full((nsp, tq), NEG, F32)))[1]


def _nsa_kernel(qn_ref, qr_ref, dtg_ref, kchi_ref, kclo_ref, vct_ref, kaug_ref, vt_ref, kwin_ref, vwt_ref,
                y_ref, qat, p4_sc, cs_sc, sa_sc, sb_sc, m_sc, acc_sc, *, tq, ns, nsp, tk, wl):
    t = pl.program_id(1)
    scale = HEAD_DIM ** -0.5
    r4 = 4 * tq
    nslab = nsp // LANES
    lane4 = lax.broadcasted_iota(jnp.int32, (1, r4), 1)
    posr = t * tq + (lane4 & (tq - 1))
    qpos = t * tq + lax.broadcasted_iota(jnp.int32, (1, tq), 1)

    qn = qn_ref[0] * scale
    qr = qr_ref[0] * scale
    ncb = kchi_ref.shape[1]
    cch = cs_sc.shape[1]
    cmax = (t * tq + tq - CMP_BLOCK) // CMP_STRIDE
    nch = jnp.minimum(cmax // cch + 1, ncb // cch)

    ws = pl.multiple_of(jnp.maximum(t * tq - WINDOW, 0), LANES)
    kw = kwin_ref[pl.ds(ws, wl), :]
    kposw = ws + lax.broadcasted_iota(jnp.int32, (wl, 1), 0)
    dpos = posr - kposw
    mask_w = (dpos >= 0) & (dpos < WINDOW)
    wpage = ws // PAGE

    p4_sc[0:8, :] = jnp.zeros((8, tq), F32)
    p4_sc[8 + ncb:, :] = jnp.zeros((p4_sc.shape[0] - 8 - ncb, tq), F32)
    o_cmp = []
    o_win = []
    imps = []
    for g in range(KV_HEADS):
        qn_t = _group_rows(qn, g, tq).T
        qr_t = _group_rows(qr, g, tq).T
        qh = qn_t.astype(BF16)
        ql = (qn_t - qh.astype(F32)).astype(BF16)

        def cmp_scores(k, m):
            r0 = pl.multiple_of(k * cch, cch)
            kh = kchi_ref[0, pl.ds(r0, cch), :]
            kl = kclo_ref[0, pl.ds(r0, cch), :]
            s = _dot(kh, qh) + (_dot(kh, ql) + _dot(kl, qh))
            c_end = (r0 + lax.broadcasted_iota(jnp.int32, (cch, 1), 0)) * CMP_STRIDE + (CMP_BLOCK - 1)
            s = jnp.where(c_end <= posr, s, NEG)
            cs_sc[k] = s
            return jnp.maximum(m, jnp.max(s, axis=0, keepdims=True))

        m_c = lax.fori_loop(0, nch, cmp_scores, jnp.full((1, r4), NEG, F32))

        def cmp_exp(k, l):
            e = jnp.exp(cs_sc[k] - m_c)
            cs_sc[k] = e
            return l + jnp.sum(e, axis=0, keepdims=True)

        l_c = lax.fori_loop(0, nch, cmp_exp, jnp.zeros((1, r4), F32))
        inv_c = jnp.where(m_c > 0.5 * NEG, 1.0 / jnp.maximum(l_c, 1e-30), 0.0)

        def cmp_out(k, o):
            r0 = pl.multiple_of(k * cch, cch)
            p = cs_sc[k] * inv_c
            p4_sc[pl.ds(8 + r0, cch), :] = p[:, 0:tq] + p[:, tq:2 * tq] + p[:, 2 * tq:3 * tq] + p[:, 3 * tq:4 * tq]
            return o + _dot(vct_ref[k], p.astype(BF16))

        o_cmp.append(lax.fori_loop(0, nch, cmp_out, jnp.zeros((LANES, r4), F32)))

        def cmp_zero(k, carry):
            r0 = pl.multiple_of(k * cch, cch)
            p4_sc[pl.ds(8 + r0, cch), :] = jnp.zeros((cch, tq), F32)
            return carry

        lax.fori_loop(nch, ncb // cch, cmp_zero, 0)
        per = SLC_BLOCK // CMP_STRIDE
        imp = p4_sc[pl.ds(8 - 1, nsp, stride=per), :]
        for k in range(per):
            imp = imp + p4_sc[pl.ds(8 + k, nsp, stride=per), :]
        imps.append(imp)
        qs_b = (qr_t * LOG2E).astype(BF16)
        for sl in range(nslab):
            qat[sl, g, 0:LANES, :] = qs_b
        p_w = _masked_softmax_t(_dot(kw, qr_t.astype(BF16)), mask_w).astype(BF16)
        o_w = _dot(vwt_ref[wpage], p_w[0:PAGE])
        for i in range(1, wl // PAGE):
            o_w = o_w + _dot(vwt_ref[wpage + i], p_w[i * PAGE:(i + 1) * PAGE])
        o_win.append(o_w)

    def select(rows):
        for g in range(KV_HEADS):
            selb = _topk_bias_t(imps[g][0:rows], qpos, min(ns, rows))
            if rows < nsp:
                selb = jnp.concatenate([selb, jnp.full((nsp - rows, tq), NEG, F32)], axis=0)
            for sl in range(nslab):
                piece = selb[sl * LANES:(sl + 1) * LANES, :]
                qat[sl, g, LANES:2 * LANES, :] = jnp.concatenate([piece] * 4, axis=1).astype(BF16)

    half = nsp // 2
    if half % 8 == 0 and half >= TOP_N:
        last_block = (t * tq + tq - 1) // SLC_BLOCK

        @pl.when(last_block < half)
        def _():
            select(half)

        @pl.when(last_block >= half)
        def _():
            select(nsp)
    else:
        select(nsp)

    m_sc[...] = jnp.full(m_sc.shape, 2.0 * NEG, F32)
    acc_sc[...] = jnp.zeros(acc_sc.shape, F32)
    ndiag = (t * tq + tq - 1) // tk

    def scores(jt, s_buf):
        k0 = pl.multiple_of(jt * tk, tk)
        kt = kaug_ref[pl.ds(k0, tk), :]
        sb = k0 // (SLAB_BLOCKS * SLC_BLOCK)
        for g in range(KV_HEADS):
            s_buf[g] = _dot(kt, qat[sb, g])

    def accumulate(jt, s_buf, causal):
        for g in range(KV_HEADS):
            s = s_buf[g]
            if causal:
                kpos = jt * tk + lax.broadcasted_iota(jnp.int32, (tk, 1), 0)
                s = jnp.where(kpos <= posr, s, NEG)
            m_old = m_sc[g]
            m_new = jnp.maximum(m_old, jnp.max(s, axis=0, keepdims=True))
            pe = jnp.exp2(s - m_new).astype(BF16)
            acc_sc[g] = jnp.exp2(m_old - m_new) * acc_sc[g] + _dot(vt_ref[jt, g], pe)
            m_sc[g] = m_new

    scores(0, sa_sc)

    def tile_pair(p, carry):
        jt = 2 * p
        scores(jt + 1, sb_sc)
        accumulate(jt, sa_sc, False)
        scores(jt + 2, sa_sc)
        accumulate(jt + 1, sb_sc, False)
        return carry

    npair = ndiag // 2
    lax.fori_loop(0, npair, tile_pair, 0)

    @pl.when(ndiag % 2 == 0)
    def _():
        accumulate(ndiag, sa_sc, True)

    @pl.when(ndiag % 2 == 1)
    def _():
        scores(ndiag, sb_sc)
        accumulate(ndiag - 1, sa_sc, False)
        accumulate(ndiag, sb_sc, True)

    sg_t = jax.nn.sigmoid(dtg_ref[0]).T
    outs = []
    for g in range(KV_HEADS):
        gate = [jnp.concatenate([sg_t[8 + 3 * (4 * g + r) + k:8 + 3 * (4 * g + r) + k + 1, :] for r in range(4)], axis=1)
                for k in range(3)]
        o_sel = acc_sc[g, 0:HEAD_DIM, :] / acc_sc[g, HEAD_DIM:HEAD_DIM + 1, :]
        zero = jnp.zeros((HEAD_DIM, r4), F32)
        o_sel = jnp.concatenate([o_sel, zero] if g == 0 else [zero, o_sel], axis=0)
        o_t = gate[0] * o_cmp[g] + gate[1] * o_sel + gate[2] * o_win[g]
        outs.append(o_t.T)
    _emit_heads(outs, tq, y_ref)


def _nsa(qn, qr, dtg, kchi, kclo, vct, kaug, vt, kwin, vwt, *, tq, ns, nsp, tk, wl):
    b, lq = qn.shape[:2]
    ncb = kchi.shape[1]
    cch = min(CMP_CHUNK, ncb)
    assert nsp * (SLC_BLOCK // CMP_STRIDE) >= ncb and ncb % cch == 0
    vct = vct[0].reshape(LANES, ncb // cch, cch).transpose(1, 0, 2)
    nslab = nsp // LANES
    kern = functools.partial(_nsa_kernel, tq=tq, ns=ns, nsp=nsp, tk=tk, wl=wl)
    rows = lambda n: pl.BlockSpec((1, tq, n), lambda i, t: (i, t, 0))
    whole = lambda arr: pl.BlockSpec(arr.shape, lambda i, t: (0,) * arr.ndim, pipeline_mode=pl.Buffered(1))
    return pl.pallas_call(
        kern,
        grid=(b, lq // tq),
        in_specs=[rows(512), rows(512), rows(128), whole(kchi), whole(kclo), whole(vct),
                  whole(kaug), whole(vt), whole(kwin), whole(vwt)],
        out_specs=rows(512),
        out_shape=jax.ShapeDtypeStruct((b, lq, 512), F32),
        scratch_shapes=[pltpu.VMEM((nslab, KV_HEADS, 2 * LANES, 4 * tq), BF16),
                        pltpu.VMEM((nsp * (SLC_BLOCK // CMP_STRIDE) + 16, tq), F32),
                        pltpu.VMEM((ncb // cch, cch, 4 * tq), F32),
                        pltpu.VMEM((KV_HEADS, tk, 4 * tq), F32),
                        pltpu.VMEM((KV_HEADS, tk, 4 * tq), F32),
                        pltpu.VMEM((KV_HEADS, 1, 4 * tq), F32),
                        pltpu.VMEM((KV_HEADS, HEAD_DIM + SUM_ROWS, 4 * tq), F32)],
        compiler_params=_cparams(("arbitrary", "arbitrary")),
    )(qn, qr, dtg, kchi, kclo, vct, kaug, vt, kwin, vwt)


SAMPLE_PAGES_PER_STEP = 16
SAMPLE_TQ = 8


def _nsa_sample_kernel(pt_ref, *refs, npages, past, ns, nsp):
    pps = SAMPLE_PAGES_PER_STEP
    tq = SAMPLE_TQ
    pages = refs[:pps]
    (tail_ref, qn_ref, qr_ref, dtg_ref, w1_ref, pe_ref, b1_ref, w2_ref, ov_ref, cwin_ref, wtail_ref,
     y_ref, tsc, xs, ps, s_all, vt_all, bias_sc) = refs[pps:]
    j = pl.program_id(1)
    ngroups = npages // pps
    hb = PAGE // CMP_STRIDE
    scale = HEAD_DIM ** -0.5
    qr_aug = jnp.concatenate([_group_rows(qr_ref[0] * scale, g, tq) for g in range(KV_HEADS)], axis=0).astype(BF16)

    for i, pg in enumerate(pages):
        for kv in range(2):
            tsc[i, kv] = pg[0, 0, kv * LANES:(kv + 1) * LANES, :].T
            for pos in range(CMP_STRIDE):
                xs[kv, i * hb:(i + 1) * hb, pos * LANES:(pos + 1) * LANES] = tsc[i, kv, pl.ds(pos, hb, stride=CMP_STRIDE), :]
        page_idx = j * pps + i
        s_all[page_idx] = _dot(qr_aug, pg[0, 0, 256:384, :].astype(BF16))
        vt_all[page_idx] = pg[0, 0, 384:512, :].astype(BF16)
    rows = pps * hb
    g0 = pl.multiple_of(j * rows, rows)
    for kv in range(2):
        x = xs[kv]
        for s in range(2):
            ps[kv, s, pl.ds(g0, rows), :] = _dot((x + pe_ref[kv, s:s + 1, :]).astype(BF16), w1_ref[kv, s])

    @pl.when(j == ngroups - 1)
    def _():
        nh = npages * hb
        kcvc = []
        for kv in range(2):
            hid = b1_ref[kv:kv + 1, :] + ps[kv, 0] + pltpu.roll(ps[kv, 1], nh - 1, 0)
            kcvc.append(_dot(jax.nn.gelu(hid).astype(BF16), w2_ref[kv]))
        kc, vc = kcvc
        qpos = past + lax.broadcasted_iota(jnp.int32, (tq, 1), 0)
        pos_r = jnp.concatenate([qpos] * (4 * KV_HEADS), axis=0)
        nrow = 4 * KV_HEADS * tq
        s_all[npages] = _dot(qr_aug, tail_ref[0, 256:384, :].astype(BF16))
        vt_all[npages] = tail_ref[0, 384:512, :].astype(BF16)

        qn_aug = jnp.concatenate([_group_rows(qn_ref[0] * scale, g, tq) for g in range(KV_HEADS)], axis=0)
        c_end = lax.broadcasted_iota(jnp.int32, (1, nh), 1) * CMP_STRIDE + (CMP_BLOCK - 1)
        p_c = _masked_softmax(_dot_nt_hi(qn_aug, kc), c_end <= pos_r)
        o_c = _dot(p_c.astype(BF16), vc.astype(BF16))
        imps = []
        for g in range(KV_HEADS):
            base = 4 * tq * g
            p4 = (p_c[base:base + tq] + p_c[base + tq:base + 2 * tq]
                  + p_c[base + 2 * tq:base + 3 * tq] + p_c[base + 3 * tq:base + 4 * tq])
            imps.append(_importance(p4, ov_ref[...]))
        selb = _topk_bias(jnp.concatenate(imps, axis=0), jnp.concatenate([qpos] * KV_HEADS, axis=0), ns)
        lane_b = lax.broadcasted_iota(jnp.int32, (KV_HEADS * tq, LANES), 1)
        per_page = PAGE // SLC_BLOCK
        assert per_page == 2 and nsp >= per_page * (npages + 1)
        for pp in range(npages + 1):
            lo = jnp.broadcast_to(selb[:, 2 * pp:2 * pp + 1], (KV_HEADS * tq, LANES))
            hi = jnp.broadcast_to(selb[:, 2 * pp + 1:2 * pp + 2], (KV_HEADS * tq, LANES))
            bias_sc[pp] = jnp.where(lane_b < SLC_BLOCK, lo, hi)

        def biased(pidx):
            b = bias_sc[pidx]
            b_r = jnp.concatenate([b[0:tq]] * 4 + [b[tq:2 * tq]] * 4, axis=0)
            return s_all[pidx] + b_r

        def p1(pidx, m):
            s = biased(pidx)
            s_all[pidx] = s
            return jnp.maximum(m, s)

        m = lax.fori_loop(0, npages, p1, jnp.full((nrow, LANES), 2.0 * NEG, F32))
        kpos_t = past + lax.broadcasted_iota(jnp.int32, (1, PAGE), 1)
        s_t = jnp.where(kpos_t <= pos_r, biased(npages), NEG)
        s_all[npages] = s_t
        mrow = jnp.max(jnp.maximum(m, s_t), axis=-1, keepdims=True)

        def p2(pidx, carry):
            l, acc = carry
            p = jnp.exp(s_all[pidx] - mrow)
            return l + p, acc + _dot_nt(p.astype(BF16), vt_all[pidx])

        zero = jnp.zeros((nrow, LANES), F32)
        l, acc = p2(npages, lax.fori_loop(0, npages, p2, (zero, zero), unroll=8))
        o_s = acc / jnp.sum(l, axis=-1, keepdims=True)

        wbuf = cwin_ref.shape[3]
        kw_b = cwin_ref[0, 0, 0:128, :].astype(BF16)
        vw_b = cwin_ref[0, 0, 128:256, :].astype(BF16)
        kt_b = wtail_ref[0, 0:128, :].astype(BF16)
        vt_b = wtail_ref[0, 128:256, :].astype(BF16)
        s_w = jnp.concatenate([_dot(qr_aug, kw_b), _dot(qr_aug, kt_b)], axis=1)
        kposw = jnp.concatenate([past - wbuf + lax.broadcasted_iota(jnp.int32, (1, wbuf), 1), kpos_t], axis=1)
        dpos = pos_r - kposw
        p_w = _masked_softmax(s_w, (dpos >= 0) & (dpos < WINDOW) & (kposw >= 0))
        o_w = _dot_nt(p_w[:, 0:wbuf].astype(BF16), vw_b) + _dot_nt(p_w[:, wbuf:].astype(BF16), vt_b)

        sg = jax.nn.sigmoid(dtg_ref[0])
        outs = []
        for g in range(KV_HEADS):
            sl = slice(4 * tq * g, 4 * tq * (g + 1))
            outs.append(_gate_cols(sg, g, 0, tq) * o_c[sl] + _gate_cols(sg, g, 1, tq) * o_s[sl]
                        + _gate_cols(sg, g, 2, tq) * o_w[sl])
        _emit_heads(outs, tq, y_ref)


def _nsa_sample(layer, cache_t, table, tail_t, qn, qr, dtg, w1e, pee, b1e, w2e, ov, cwin_t, wtail_t, *, past, ns):
    b, npages = table.shape
    pps = SAMPLE_PAGES_PER_STEP
    tq = SAMPLE_TQ
    nsp = ov.shape[1]
    nh = npages * (PAGE // CMP_STRIDE)
    nrow = 4 * KV_HEADS * tq
    kern = functools.partial(_nsa_sample_kernel, npages=npages, past=past, ns=ns, nsp=nsp)
    const = lambda arr: pl.BlockSpec(arr.shape, lambda i, j, pt: (0,) * arr.ndim, pipeline_mode=pl.Buffered(1))
    perb = lambda arr: pl.BlockSpec((1,) + arr.shape[1:], lambda i, j, pt: (i,) + (0,) * (arr.ndim - 1))

    def page_spec(k):
        return pl.BlockSpec((1, 1, 512, PAGE), lambda i, j, pt: (layer, pt[i, j * pps + k], 0, 0))

    return pl.pallas_call(
        kern,
        grid_spec=pltpu.PrefetchScalarGridSpec(
            num_scalar_prefetch=1,
            grid=(b, npages // pps),
            in_specs=[page_spec(k) for k in range(pps)] + [
                perb(tail_t), perb(qn), perb(qr), perb(dtg), const(w1e), const(pee), const(b1e), const(w2e),
                const(ov),
                pl.BlockSpec((1, 1) + cwin_t.shape[2:], lambda i, j, pt: (layer, i, 0, 0)), perb(wtail_t)],
            out_specs=pl.BlockSpec((1, tq, 512), lambda i, j, pt: (i, 0, 0)),
            scratch_shapes=[pltpu.VMEM((pps, 2, PAGE, LANES), F32),
                            pltpu.VMEM((2, pps * 8, CMP_STRIDE * LANES), F32),
                            pltpu.VMEM((2, 2, nh, 256), F32),
                            pltpu.VMEM((npages + 1, nrow, LANES), F32),
                            pltpu.VMEM((npages + 1, LANES, PAGE), BF16),
                            pltpu.VMEM((npages + 1, KV_HEADS * tq, LANES), F32)]),
        out_shape=jax.ShapeDtypeStruct((b, tq, 512), F32),
        compiler_params=_cparams(("arbitrary", "arbitrary")),
    )(table, *([cache_t] * pps), tail_t, qn, qr, dtg, w1e, pee, b1e, w2e, ov, cwin_t, wtail_t)


def _oproj_kernel(x_ref, ys_ref, yn_ref, w_ref, g_ref, r_ref, x1_ref, h_ref, lg_ref):
    a = jnp.concatenate([ys_ref[...], yn_ref[...]], axis=1).astype(BF16)
    x1 = x_ref[...] + _dot(a, w_ref[...])
    x1_ref[...] = x1
    h = _rms(x1, g_ref[...])
    h_ref[...] = h.astype(h_ref.dtype)
    lg_ref[...] = _dot_hi(h, r_ref[...])


def _oproj(x, ys, yn, w, g, router, tm, h_dtype):
    t = x.shape[0]
    row = lambda n: pl.BlockSpec((tm, n), lambda i: (i, 0))
    full = lambda a: pl.BlockSpec(a.shape, lambda i: (0,) * a.ndim)
    return pl.pallas_call(
        _oproj_kernel,
        grid=(t // tm,),
        in_specs=[row(D_MODEL), row(512), row(512), full(w), full(g), full(router)],
        out_specs=[row(D_MODEL), row(D_MODEL), row(LANES)],
        out_shape=[jax.ShapeDtypeStruct((t, D_MODEL), F32), jax.ShapeDtypeStruct((t, D_MODEL), h_dtype),
                   jax.ShapeDtypeStruct((t, LANES), F32)],
        compiler_params=_cparams(("arbitrary",)),
    )(x, ys, yn, w, g, router)


def _ffn_kernel(h_ref, x1_ref, wg_ref, wu_ref, wd_ref, o_ref):
    f = pl.program_id(1)

    @pl.when(f == 0)
    def _():
        o_ref[...] = x1_ref[...]

    hb = h_ref[...]
    act = _silu(_dot(hb, wg_ref[...])) * _dot(hb, wu_ref[...])
    o_ref[...] += _dot(act.astype(BF16), wd_ref[...])


def _ffn(h, x1, wg, wu, wd, tm, tf):
    t = h.shape[0]
    dff = wg.shape[1]
    return pl.pallas_call(
        _ffn_kernel,
        grid=(t // tm, dff // tf),
        in_specs=[pl.BlockSpec((tm, D_MODEL), lambda i, f: (i, 0)),
                  pl.BlockSpec((tm, D_MODEL), lambda i, f: (i, 0)),
                  pl.BlockSpec((D_MODEL, tf), lambda i, f: (0, f)),
                  pl.BlockSpec((D_MODEL, tf), lambda i, f: (0, f)),
                  pl.BlockSpec((tf, D_MODEL), lambda i, f: (f, 0))],
        out_specs=pl.BlockSpec((tm, D_MODEL), lambda i, f: (i, 0)),
        out_shape=jax.ShapeDtypeStruct((t, D_MODEL), F32),
        compiler_params=_cparams(("arbitrary", "arbitrary")),
    )(h, x1, wg, wu, wd)


def _moe_kernel(be_ref, nb_ref, tok_ref, h_hbm, wg_ref, wu_ref, wd_ref, o_ref, xbuf, xb16, sem, *, tmb):
    b = pl.program_id(0)
    f = pl.program_id(1)
    nb = nb_ref[0]

    def request_rows(blk, slot):
        def body(r, carry):
            tok = tok_ref[blk * tmb + r]
            pltpu.make_async_copy(h_hbm.at[pl.ds(tok, 1), :], xbuf.at[slot, pl.ds(r, 1), :], sem.at[slot]).start()
            return carry
        lax.fori_loop(0, tmb, body, 0, unroll=8)

    @pl.when(f == 0)
    def _():
        o_ref[...] = jnp.zeros(o_ref.shape, F32)

        @pl.when((b == 0) & (nb > 0))
        def _():
            request_rows(0, 0)

        @pl.when(b + 1 < nb)
        def _():
            request_rows(b + 1, (b + 1) & 1)

        @pl.when(b < nb)
        def _():
            slot = b & 1
            pltpu.make_async_copy(xbuf.at[slot], xbuf.at[slot], sem.at[slot]).wait()
            xb16[...] = xbuf[slot].astype(BF16)

    @pl.when(b < nb)
    def _():
        hb = xb16[...]
        act = _silu(_dot(hb, wg_ref[0].astype(BF16))) * _dot(hb, wu_ref[0].astype(BF16))
        o_ref[...] += _dot(act.astype(BF16), wd_ref[0].astype(BF16))


def _moe(blk_e, nblk_used, buf_tok, h_pad, wg, wu, wd, tmb, tf):
    n_blk = buf_tok.shape[0] // tmb
    dex = wg.shape[2]
    nf = dex // tf

    def feff(b, f, nb):
        return jnp.where(b < nb[0], f, nf - 1)

    return pl.pallas_call(
        functools.partial(_moe_kernel, tmb=tmb),
        grid_spec=pltpu.PrefetchScalarGridSpec(
            num_scalar_prefetch=3,
            grid=(n_blk, nf),
            in_specs=[pl.BlockSpec(memory_space=pl.ANY),
                      pl.BlockSpec((1, D_MODEL, tf), lambda b, f, be, nb, tok: (be[b], 0, feff(b, f, nb))),
                      pl.BlockSpec((1, D_MODEL, tf), lambda b, f, be, nb, tok: (be[b], 0, feff(b, f, nb))),
                      pl.BlockSpec((1, tf, D_MODEL), lambda b, f, be, nb, tok: (be[b], feff(b, f, nb), 0))],
            out_specs=pl.BlockSpec((tmb, D_MODEL), lambda b, f, be, nb, tok: (b, 0)),
            scratch_shapes=[pltpu.VMEM((2, tmb, D_MODEL), F32), pltpu.VMEM((tmb, D_MODEL), BF16),
                            pltpu.SemaphoreType.DMA((2,))]),
        out_shape=jax.ShapeDtypeStruct((n_blk * tmb, D_MODEL), F32),
        compiler_params=_cparams(("arbitrary", "arbitrary")),
    )(blk_e, nblk_used, buf_tok, h_pad, wg, wu, wd)


def _moe_layer(h, x1, logits, wg, wu, wd, tmb, tf):
    t = h.shape[0]
    top_v, top_e = lax.top_k(logits[:, :N_EXPERTS], TOP_K)
    gate = jax.nn.softmax(top_v, axis=-1)
    e_flat = top_e.reshape(-1)
    npair = t * TOP_K
    order = jnp.argsort(e_flat)
    rank = jnp.argsort(order).astype(jnp.int32)
    counts = jnp.bincount(e_flat, length=N_EXPERTS).astype(jnp.int32)
    starts = jnp.cumsum(counts) - counts
    padded = (counts + tmb - 1) // tmb * tmb
    pends = jnp.cumsum(padded)
    pstarts = pends - padded
    dest = (pstarts[e_flat] + rank - starts[e_flat]).astype(jnp.int32).reshape(t, TOP_K)
    n_blk = -(-npair // tmb) + N_EXPERTS
    blk_e = jnp.minimum(jnp.searchsorted(pends, jnp.arange(n_blk, dtype=jnp.int32) * tmb, side='right'),
                        N_EXPERTS - 1).astype(jnp.int32)
    nblk_used = (pends[-1] // tmb).astype(jnp.int32).reshape(1)
    e_p = jnp.repeat(blk_e, tmb)
    off = jnp.arange(n_blk * tmb, dtype=jnp.int32) - pstarts[e_p]
    src = order[jnp.clip(starts[e_p] + off, 0, npair - 1)].astype(jnp.int32)
    buf_tok = jnp.where(off < counts[e_p], src // TOP_K, t)
    h_pad = jnp.concatenate([h, jnp.zeros((8, D_MODEL), h.dtype)], axis=0)
    out = _moe(blk_e, nblk_used, buf_tok.astype(jnp.int32), h_pad, wg, wu, wd, tmb, tf)
    return x1 + out[dest[:, 0]] * gate[:, 0:1] + out[dest[:, 1]] * gate[:, 1:2]


def _final_kernel(x_ref, g_ref, o_ref):
    o_ref[...] = _rms(x_ref[...], g_ref[...])


def _final_norm(x, g, tm):
    t = x.shape[0]
    return pl.pallas_call(
        _final_kernel,
        grid=(t // tm,),
        in_specs=[pl.BlockSpec((tm, D_MODEL), lambda i: (i, 0)), pl.BlockSpec(g.shape, lambda i: (0, 0))],
        out_specs=pl.BlockSpec((tm, D_MODEL), lambda i: (i, 0)),
        out_shape=jax.ShapeDtypeStruct((t, D_MODEL), F32),
        compiler_params=_cparams(("arbitrary",)),
    )(x, g)


def _rope_tables(pos):
    inv = 1.0 / (ROPE_THETA ** (jnp.arange(0, HEAD_DIM, 2, dtype=F32) / HEAD_DIM))
    ang = pos.astype(F32)[:, None] * inv[None, :]
    cos = jnp.cos(ang)
    sin = jnp.sin(ang)
    return jnp.concatenate([cos, cos, cos, cos], axis=1), jnp.concatenate([-sin, sin, -sin, sin], axis=1)


def _overlap(ncb, nsp, ns):
    c = jnp.arange(ncb, dtype=jnp.int32)[:, None] * CMP_STRIDE
    s = jnp.arange(nsp, dtype=jnp.int32)[None, :] * SLC_BLOCK
    ov = (c <= s + SLC_BLOCK - 1) & (c + CMP_BLOCK - 1 >= s) & (jnp.arange(nsp)[None, :] < ns)
    return ov.astype(BF16)


def _pad_lanes(v, n):
    return jnp.zeros((1, n), F32).at[0, :v.shape[0]].set(v.astype(F32))


def kernel(x_prompt, x_sample, cache_kv, cache_win, state_ssm, state_conv, page_table, ln_mix, w_in, conv_w,
           conv_b, dt_bias, a_log, d_skip, ssd_norm, cmp_pos, cmp_w1, cmp_b1, cmp_w2, w_out, ln_ffn,
           ffn_w_gate, ffn_w_up, ffn_w_down, moe_router, moe_w_gate, moe_w_up, moe_w_down, ln_final):
    depth = w_in.shape[0]
    bsz, seq = x_prompt.shape[:2]
    dbsz, dseq = x_sample.shape[:2]
    n_pages = page_table.shape[1]
    past = n_pages * cache_kv.shape[2]
    win_buf = cache_win.shape[2]
    npool = cache_kv.shape[1]
    assert bsz == 1 and cache_kv.shape[2] == PAGE and seq % PAGE == 0 and seq >= WINDOW + PAGE
    assert win_buf == WINDOW and past % PAGE == 0
    assert not any(past <= CMP_STRIDE * c + CMP_BLOCK - 1 <= past + dseq - 1
                   for c in range(past // CMP_STRIDE - 2, past // CMP_STRIDE + 2))

    ts = dbsz * dseq
    ns_p = seq // SLC_BLOCK
    ns_s = -(-(past + dseq) // SLC_BLOCK)
    nsp_p = -(-ns_p // LANES) * LANES
    nsp_s = -(-ns_s // LANES) * LANES
    assert dseq <= SAMPLE_TQ and n_pages % SAMPLE_PAGES_PER_STEP == 0
    cache_t = jnp.transpose(cache_kv, (0, 1, 3, 4, 5, 2)).reshape(depth, npool, 512, PAGE)
    cwin_t = jnp.transpose(cache_win, (0, 1, 3, 4, 5, 2)).reshape(depth, dbsz, 256, win_buf)

    cs_p, sn_p = _rope_tables(jnp.arange(seq, dtype=jnp.int32))
    cs_s, sn_s = _rope_tables(past + jnp.arange(ts, dtype=jnp.int32) % dseq)
    ov_s = _overlap(past // CMP_STRIDE, nsp_s, ns_s)
    ident = jnp.arange(seq // CMP_STEP_ROWS, dtype=jnp.int32).reshape(1, -1)

    xp = x_prompt.reshape(seq, D_MODEL)
    xs = x_sample.reshape(ts, D_MODEL)
    outs = {k: [] for k in ('kv_p', 'kv_s', 'win_p', 'win_s', 'ssm_p', 'ssm_s', 'conv_p', 'conv_s')}

    for l in range(depth):
        w = w_in[l]
        wr = jnp.concatenate([w[:, :1536], w[:, 1544:2824], w[:, 1536:1544], w[:, 2824:2848],
                              jnp.zeros((D_MODEL, C_END - 2848), F32)], axis=1).astype(BF16)
        g_mix = ln_mix[l].reshape(1, -1)
        cw8 = jnp.zeros((8, CONV_DIM), F32).at[:CONV_W].set(conv_w[l])
        cb = conv_b[l].reshape(1, -1)
        dtb = _pad_lanes(dt_bias[l], LANES)
        a_neg = _pad_lanes(-jnp.exp(a_log[l].astype(F32)), LANES)
        dsk = jnp.repeat(d_skip[l].astype(F32), SSD_HEAD_DIM).reshape(1, -1)
        ng = ssd_norm[l].reshape(1, -1)
        w1 = cmp_w1[l].reshape(2, 2, CMP_STRIDE, HEAD_DIM, CMP_HIDDEN)
        eye = jnp.eye(KV_HEADS, dtype=F32)
        w1e = jnp.einsum('vspdh,gk->vspgdkh', w1, eye).reshape(2, 2, CMP_STRIDE * LANES, 256).astype(BF16)
        pe = cmp_pos[l].reshape(2, 2, CMP_STRIDE, 1, HEAD_DIM)
        pee = jnp.broadcast_to(pe, (2, 2, CMP_STRIDE, KV_HEADS, HEAD_DIM)).reshape(2, 2, CMP_STRIDE * LANES)
        b1e = jnp.concatenate([cmp_b1[l], cmp_b1[l]], axis=1)
        w2e = jnp.einsum('vhd,gk->vghkd', cmp_w2[l], eye).reshape(2, 256, LANES).astype(BF16)
        wo = w_out[l].astype(BF16)
        g_ffn = ln_ffn[l].reshape(1, -1)
        if l % 2 == 0:
            router = jnp.zeros((D_MODEL, LANES), F32)
        else:
            router = jnp.zeros((D_MODEL, LANES), F32).at[:, :N_EXPERTS].set(moe_router[l // 2])

        z, xbc, dtg, q, qr, kv, win = _front(xp, g_mix, wr, cs_p, sn_p, 512)
        y_ssd, hist, hfin = _ssd(z[None], xbc[None], dtg[None], jnp.zeros((1, 8, CONV_DIM), F32),
                                 jnp.zeros((1, D_SSD, SSD_STATE), F32), cw8, cb, dtb, a_neg, dsk, ng, SSD_CHUNK)
        kv_pages = kv.reshape(seq // CMP_STEP_ROWS, CMP_STEP_ROWS, 512)
        kchi, kclo, vct = _compress(kv_pages, ident, w1e, pee, b1e, w2e)
        kaug, vt, kwin, vwt = _kvfmt(kv, win, NSA_TK)
        y_nsa = _nsa(q[None], qr[None], dtg[None], kchi, kclo, vct, kaug, vt, kwin, vwt,
                     tq=128, ns=ns_p, nsp=nsp_p, tk=NSA_TK, wl=WINDOW + 128)
        h_dtype = BF16 if l % 2 == 0 else F32
        x1, h, logits = _oproj(xp, y_ssd[0], y_nsa[0], wo, g_ffn, router, 512, h_dtype)
        outs['kv_p'].append(kv.reshape(1, seq, 4, KV_HEADS, HEAD_DIM))
        outs['win_p'].append(win[seq - WINDOW:].reshape(1, WINDOW, 2, KV_HEADS, HEAD_DIM))
        outs['ssm_p'].append(hfin.reshape(1, SSD_HEADS, SSD_HEAD_DIM, SSD_STATE))
        outs['conv_p'].append(hist[:, 5:8])

        zs, xbcs, dtgs, qs, qrs, kvs, wins = _front(xs, g_mix, wr, cs_s, sn_s, ts)
        padl = lambda v: jnp.pad(v.reshape(dbsz, dseq, -1), ((0, 0), (0, SSD_CHUNK - dseq), (0, 0)))
        hist8 = jnp.pad(state_conv[l], ((0, 0), (8 - (CONV_W - 1), 0), (0, 0)))
        y_ssd_s, hist_s, hfin_s = _ssd(padl(zs), padl(xbcs), padl(dtgs), hist8,
                                       state_ssm[l].reshape(dbsz, D_SSD, SSD_STATE),
                                       cw8, cb, dtb, a_neg, dsk, ng, dseq)
        tail_t = jnp.pad(jnp.transpose(kvs.reshape(dbsz, dseq, 512), (0, 2, 1)), ((0, 0), (0, 0), (0, PAGE - dseq)))
        wtail_t = jnp.pad(jnp.transpose(wins.reshape(dbsz, dseq, 256), (0, 2, 1)), ((0, 0), (0, 0), (0, PAGE - dseq)))
        padq = lambda v: jnp.pad(v.reshape(dbsz, dseq, -1), ((0, 0), (0, SAMPLE_TQ - dseq), (0, 0)))
        y_nsa_s = _nsa_sample(l, cache_t, page_table, tail_t, padq(qs), padq(qrs), padq(dtgs), w1e, pee, b1e, w2e,
                              ov_s, cwin_t, wtail_t, past=past, ns=ns_s)
        y_nsa_s = y_nsa_s[:, :dseq].reshape(ts, D_NSA)
        x1s, hs, logits_s = _oproj(xs, y_ssd_s[:, :dseq].reshape(ts, D_SSD), y_nsa_s, wo, g_ffn, router, ts, h_dtype)
        outs['kv_s'].append(kvs.reshape(dbsz, dseq, 4, KV_HEADS, HEAD_DIM))
        win_all = jnp.concatenate([cache_win[l], wins.reshape(dbsz, dseq, 2, KV_HEADS, HEAD_DIM)], axis=1)
        outs['win_s'].append(win_all[:, dseq:])
        outs['ssm_s'].append(hfin_s.reshape(dbsz, SSD_HEADS, SSD_HEAD_DIM, SSD_STATE))
        outs['conv_s'].append(hist_s[:, 5:8])

        if l % 2 == 0:
            wg = ffn_w_gate[l // 2].astype(BF16)
            wu = ffn_w_up[l // 2].astype(BF16)
            wd = ffn_w_down[l // 2].astype(BF16)
            xp = _ffn(h, x1, wg, wu, wd, 1024, 256)
            xs = _ffn(hs, x1s, wg, wu, wd, ts, 256)
        else:
            wg = moe_w_gate[l // 2]
            wu = moe_w_up[l // 2]
            wd = moe_w_down[l // 2]
            xp = _moe_layer(h, x1, logits, wg, wu, wd, 1024, 512)
            xs = _moe_layer(hs, x1s, logits_s, wg, wu, wd, 128, 512)

    g_fin = ln_final.reshape(1, -1)
    y_prompt = _final_norm(xp, g_fin, 1024).reshape(bsz, seq, D_MODEL)
    y_sample = _final_norm(xs, g_fin, ts).reshape(dbsz, dseq, D_MODEL)
    st = lambda k: jnp.stack(outs[k])
    return (y_prompt, y_sample, st('kv_p'), st('kv_s'), st('win_p'), st('win_s'),
            st('ssm_p'), st('ssm_s'), st('conv_p'), st('conv_s'))
```
